```python
import math
import jax, jax.numpy as jnp
from jax import lax
import numpy as np

D_MODEL = 1024
BATCH = 4
SEQ = 8192
DEPTH = 1

POOL_WIDTH = D_MODEL // 2
POOL_WINDOWS = (2, 4, 8, 16)
POOL_GROUPS = len(POOL_WINDOWS)
POOL_GROUP_DIM = POOL_WIDTH // POOL_GROUPS
SSM_WIDTH = D_MODEL // 2
SSM_GROUP_DIM = 16
SSM_GROUPS = SSM_WIDTH // SSM_GROUP_DIM
SSM_STATE = 64
N_BRANCHES = 2
IN_WIDTH = POOL_WIDTH + SSM_WIDTH + N_BRANCHES * D_MODEL
MEM_LEN = 256
X_HEADS = 4
X_HEAD_DIM = D_MODEL // X_HEADS
N_EXPERT_GROUPS = 4
EXPERTS_PER_GROUP = 4
N_EXPERTS = N_EXPERT_GROUPS * EXPERTS_PER_GROUP
TOP_K_IN_GROUP = 2
D_EXPERT = 512
RMS_EPS = 1e-6
DT_MIN = 1e-3
DT_MAX = 1e-1

kernel_name = "hybrid_pool_s5_xattn_hmoe"


def rms_norm(x, g):
    xf = x.astype(jnp.float32)
    y = xf * lax.rsqrt(jnp.mean(xf * xf, axis=-1, keepdims=True) + RMS_EPS)
    return (y * g.astype(jnp.float32)).astype(x.dtype)


def pool_mixer(z, pool_w, pool_scale):
    b, s, _ = z.shape
    zf = z.astype(jnp.float32).reshape(b, s, POOL_GROUPS, POOL_GROUP_DIM)
    cs = jnp.cumsum(zf, axis=1)
    pos = jnp.arange(1, s + 1, dtype=jnp.float32)
    pooled = []
    for gi, w in enumerate(POOL_WINDOWS):
        c = cs[:, :, gi]
        lag = jnp.pad(c, ((0, 0), (w, 0), (0, 0)))[:, :s]
        cnt = jnp.minimum(pos, float(w))[:, None]
        pooled.append((c - lag) / cnt)
    pooled = jnp.stack(pooled, axis=2)
    mixed = jnp.einsum('bsgc,gcd->bsgd', pooled - zf, pool_w.astype(jnp.float32))
    out = mixed.reshape(b, s, POOL_WIDTH) * pool_scale.astype(jnp.float32)
    return out.astype(z.dtype)


def s5_ssm(v, a_re, a_im, log_dt, b_re, b_im, c_re, c_im, d_skip):
    bsz, s, _ = v.shape
    f32 = jnp.float32
    vf = v.astype(f32).reshape(bsz, s, SSM_GROUPS, SSM_GROUP_DIM)
    ar, ai = a_re.astype(f32), a_im.astype(f32)
    dt = jnp.exp(log_dt.astype(f32))[:, None]
    mag = jnp.exp(ar * dt)
    lb_re, lb_im = mag * jnp.cos(ai * dt), mag * jnp.sin(ai * dt)
    den = ar * ar + ai * ai
    nr, ni = lb_re - 1.0, lb_im
    f_re = ((nr * ar + ni * ai) / den)[..., None]
    f_im = ((ni * ar - nr * ai) / den)[..., None]
    br, bi = b_re.astype(f32), b_im.astype(f32)
    bb_re = f_re * br - f_im * bi
    bb_im = f_re * bi + f_im * br
    u_re = jnp.einsum('bsgc,gnc->sbgn', vf, bb_re)
    u_im = jnp.einsum('bsgc,gnc->sbgn', vf, bb_im)
    a_re_s = jnp.broadcast_to(lb_re, (s, 1, SSM_GROUPS, SSM_STATE))
    a_im_s = jnp.broadcast_to(lb_im, (s, 1, SSM_GROUPS, SSM_STATE))

    def combine(e1, e2):
        a1r, a1i, b1r, b1i = e1
        a2r, a2i, b2r, b2i = e2
        return (a1r * a2r - a1i * a2i,
                a1r * a2i + a1i * a2r,
                a2r * b1r - a2i * b1i + b2r,
                a2r * b1i + a2i * b1r + b2i)

    _, _, h_re, h_im = lax.associative_scan(combine, (a_re_s, a_im_s, u_re, u_im), axis=0)
    y = (jnp.einsum('sbgn,gcn->bsgc', h_re, c_re.astype(f32))
         - jnp.einsum('sbgn,gcn->bsgc', h_im, c_im.astype(f32)))
    y = y + d_skip.astype(f32).reshape(SSM_GROUPS, SSM_GROUP_DIM) * vf
    return y.reshape(bsz, s, SSM_WIDTH)


def memory_cross_attention(hn, memn, w_q, w_kv, w_o):
    b, s, _ = hn.shape
    m = memn.shape[1]
    q = (hn @ w_q).reshape(b, s, X_HEADS, X_HEAD_DIM)
    kv = memn @ w_kv
    k = kv[..., :D_MODEL].reshape(b, m, X_HEADS, X_HEAD_DIM)
    v = kv[..., D_MODEL:].reshape(b, m, X_HEADS, X_HEAD_DIM)
    scores = jnp.einsum('bshd,bmhd->bhsm', q.astype(jnp.float32), k.astype(jnp.float32)) * (X_HEAD_DIM ** -0.5)
    p = jax.nn.softmax(scores, axis=-1).astype(v.dtype)
    o = jnp.einsum('bhsm,bmhd->bshd', p, v).reshape(b, s, D_MODEL)
    return o @ w_o


def hierarchical_moe(zn, rg_w, rg_b, re_w, re_b, w1, w3, w2):
    b, s, d = zn.shape
    z = zn.reshape(b * s, d)
    f32 = jnp.float32
    g_logits = (z @ rg_w).astype(f32) + rg_b.astype(f32)
    g_prob = jax.nn.softmax(g_logits, axis=-1)
    g_idx = jnp.argmax(g_logits, axis=-1)
    g_gate = jnp.take_along_axis(g_prob, g_idx[:, None], axis=1)
    e_logits = jnp.einsum('td,dge->tge', z, re_w).astype(f32) + re_b.astype(f32)
    e_sel = jnp.take_along_axis(e_logits, g_idx[:, None, None], axis=1)[:, 0]
    top_v, top_i = lax.top_k(e_sel, TOP_K_IN_GROUP)
    top_w = jax.nn.softmax(top_v, axis=-1)
    e_w = jnp.sum(jax.nn.one_hot(top_i, EXPERTS_PER_GROUP, dtype=f32) * top_w[..., None], axis=1)
    comb = (jax.nn.one_hot(g_idx, N_EXPERT_GROUPS, dtype=f32)[:, :, None]
            * e_w[:, None, :] * g_gate[:, :, None]).reshape(-1, N_EXPERTS).astype(z.dtype)
    out = jnp.zeros_like(z)
    for e in range(N_EXPERTS):
        hid = jax.nn.silu(z @ w1[e]) * (z @ w3[e])
        out = out + comb[:, e:e + 1] * (hid @ w2[e])
    return out.reshape(b, s, d)


def setup_inputs(seed: int = 0) -> dict:
    key = jax.random.key(seed)
    ks = jax.random.split(key, 32)
    f32 = jnp.float32
    L, D = DEPTH, D_MODEL

    def nrm(k, shape, scale):
        return jax.random.normal(k, shape, f32) * scale

    def gain(k, shape):
        return 1.0 + 0.05 * jax.random.normal(k, shape, f32)

    n_idx = jnp.arange(SSM_STATE, dtype=f32)
    return {
        "x": jax.random.normal(ks[0], (BATCH, SEQ, D), f32),
        "mem": jax.random.normal(ks[1], (BATCH, MEM_LEN, D), f32),
        "norm_mix": gain(ks[2], (L, D)),
        "w_in": nrm(ks[3], (L, D, IN_WIDTH), D ** -0.5),
        "pool_w": nrm(ks[4], (L, POOL_GROUPS, POOL_GROUP_DIM, POOL_GROUP_DIM), POOL_GROUP_DIM ** -0.5),
        "pool_scale": gain(ks[5], (L, POOL_WIDTH)),
        "ssm_a_re": -0.5 + 0.01 * jax.random.normal(ks[6], (L, SSM_GROUPS, SSM_STATE), f32),
        "ssm_a_im": math.pi * n_idx + 0.01 * jax.random.normal(ks[7], (L, SSM_GROUPS, SSM_STATE), f32),
        "ssm_log_dt": jax.random.uniform(ks[8], (L, SSM_GROUPS), f32, math.log(DT_MIN), math.log(DT_MAX)),
        "ssm_b_re": nrm(ks[9], (L, SSM_GROUPS, SSM_STATE, SSM_GROUP_DIM), (2 * SSM_GROUP_DIM) ** -0.5),
        "ssm_b_im": nrm(ks[10], (L, SSM_GROUPS, SSM_STATE, SSM_GROUP_DIM), (2 * SSM_GROUP_DIM) ** -0.5),
        "ssm_c_re": nrm(ks[11], (L, SSM_GROUPS, SSM_GROUP_DIM, SSM_STATE), SSM_STATE ** -0.5),
        "ssm_c_im": nrm(ks[12], (L, SSM_GROUPS, SSM_GROUP_DIM, SSM_STATE), SSM_STATE ** -0.5),
        "ssm_d": nrm(ks[13], (L, SSM_WIDTH), 1.0),
        "glu_w": nrm(ks[14], (L, SSM_WIDTH, SSM_WIDTH), SSM_WIDTH ** -0.5),
        "glu_b": nrm(ks[15], (L, SSM_WIDTH), 0.02),
        "br_pool": nrm(ks[16], (L, POOL_WIDTH, D), POOL_WIDTH ** -0.5),
        "br_ssm": nrm(ks[17], (L, SSM_WIDTH, D), SSM_WIDTH ** -0.5),
        "w_out": nrm(ks[18], (L, D, D), D ** -0.5),
        "norm_x": gain(ks[19], (L, D)),
        "norm_mem": gain(ks[20], (L, D)),
        "x_wq": nrm(ks[21], (L, D, D), D ** -0.5),
        "x_wkv": nrm(ks[22], (L, D, 2 * D), D ** -0.5),
        "x_wo": nrm(ks[23], (L, D, D), D ** -0.5),
        "norm_ffn": gain(ks[24], (L, D)),
        "router_g_w": nrm(ks[25], (L, D, N_EXPERT_GROUPS), D ** -0.5),
        "router_g_b": nrm(ks[26], (L, N_EXPERT_GROUPS), 0.01),
        "router_e_w": nrm(ks[27], (L, D, N_EXPERT_GROUPS, EXPERTS_PER_GROUP), D ** -0.5),
        "router_e_b": nrm(ks[28], (L, N_EXPERT_GROUPS, EXPERTS_PER_GROUP), 0.01),
        "exp_w1": nrm(ks[29], (L, N_EXPERTS, D, D_EXPERT), D ** -0.5),
        "exp_w3": nrm(ks[30], (L, N_EXPERTS, D, D_EXPERT), D ** -0.5),
        "exp_w2": nrm(ks[31], (L, N_EXPERTS, D_EXPERT, D), D_EXPERT ** -0.5),
        "norm_final": gain(jax.random.fold_in(key, 99), (D,)),
    }


def reference(x, mem, norm_mix, w_in, pool_w, pool_scale, ssm_a_re, ssm_a_im, ssm_log_dt,
              ssm_b_re, ssm_b_im, ssm_c_re, ssm_c_im, ssm_d, glu_w, glu_b, br_pool, br_ssm,
              w_out, norm_x, norm_mem, x_wq, x_wkv, x_wo, norm_ffn, router_g_w, router_g_b,
              router_e_w, router_e_b, exp_w1, exp_w3, exp_w2, norm_final):
    h = x
    for l in range(DEPTH):
        u = rms_norm(h, norm_mix[l])
        proj = u @ w_in[l]
        z_pool = proj[..., :POOL_WIDTH]
        v_ssm = proj[..., POOL_WIDTH:POOL_WIDTH + SSM_WIDTH]
        gate_logits = proj[..., POOL_WIDTH + SSM_WIDTH:]
        y_pool = pool_mixer(z_pool, pool_w[l], pool_scale[l])
        y_s = s5_ssm(v_ssm, ssm_a_re[l], ssm_a_im[l], ssm_log_dt[l], ssm_b_re[l], ssm_b_im[l],
                     ssm_c_re[l], ssm_c_im[l], ssm_d[l])
        y_s = jax.nn.gelu(y_s)
        y_s = (y_s * jax.nn.sigmoid(y_s @ glu_w[l].astype(jnp.float32) + glu_b[l].astype(jnp.float32))).astype(h.dtype)
        gates = jax.nn.sigmoid(gate_logits.astype(jnp.float32)).astype(h.dtype)
        merged = gates[..., :D_MODEL] * (y_pool @ br_pool[l]) + gates[..., D_MODEL:] * (y_s @ br_ssm[l])
        h = h + merged @ w_out[l]
        h = h + memory_cross_attention(rms_norm(h, norm_x[l]), rms_norm(mem, norm_mem[l]),
                                       x_wq[l], x_wkv[l], x_wo[l])
        h = h + hierarchical_moe(rms_norm(h, norm_ffn[l]), router_g_w[l], router_g_b[l],
                                 router_e_w[l], router_e_b[l], exp_w1[l], exp_w3[l], exp_w2[l])
    return rms_norm(h, norm_final)
```

```python
import functools
import math

import jax
import jax.numpy as jnp
from jax import lax
from jax.experimental import pallas as pl
from jax.experimental.pallas import tpu as pltpu

F32 = jnp.float32
BF16 = jnp.bfloat16

RMS_EPS = 1e-6
POOL_WINDOWS = (2, 4, 8, 16)
POOL_HALO = 16
SSM_GROUP_DIM = 16
SSM_STATE = 64
LANES = 128
OCT = LANES // SSM_GROUP_DIM
SSM_CHUNK = 4
X_HEADS = 4
N_EXPERT_GROUPS = 4
EXPERTS_PER_GROUP = 4
VMEM_LIMIT = 56 * 1024 * 1024


def _rms(x, g):
    return x * lax.rsqrt(jnp.mean(x * x, axis=-1, keepdims=True) + RMS_EPS) * g


def _const_spec(shape):
    nd = len(shape)
    return pl.BlockSpec(shape, lambda *_: (0,) * nd, pipeline_mode=pl.Buffered(1))


def _ssm_tables(a_re, a_im, log_dt, b_re, b_im, c_re, c_im, d_skip, chunk, n_scan):
    hi = lax.Precision.HIGHEST
    G, N = a_re.shape
    C = SSM_GROUP_DIM
    Q = G // OCT
    dt = jnp.exp(log_dt)[:, None]
    mag = jnp.exp(a_re * dt)
    lb_re, lb_im = mag * jnp.cos(a_im * dt), mag * jnp.sin(a_im * dt)
    den = a_re * a_re + a_im * a_im
    nr, ni = lb_re - 1.0, lb_im
    f_re = ((nr * a_re + ni * a_im) / den)[..., None]
    f_im = ((ni * a_re - nr * a_im) / den)[..., None]
    bb_re = f_re * b_re - f_im * b_im
    bb_im = f_re * b_im + f_im * b_re

    pw_re, pw_im = [jnp.ones_like(lb_re)], [jnp.zeros_like(lb_re)]
    for _ in range(chunk):
        r, i = pw_re[-1], pw_im[-1]
        pw_re.append(r * lb_re - i * lb_im)
        pw_im.append(r * lb_im + i * lb_re)
    pw_re, pw_im = jnp.stack(pw_re), jnp.stack(pw_im)

    lbb_re = pw_re[:chunk, :, :, None] * bb_re - pw_im[:chunk, :, :, None] * bb_im
    lbb_im = pw_re[:chunk, :, :, None] * bb_im + pw_im[:chunk, :, :, None] * bb_re
    kern = (jnp.einsum('gon,dgnc->dgoc', c_re, lbb_re, precision=hi)
            - jnp.einsum('gon,dgnc->dgoc', c_im, lbb_im, precision=hi))
    kern = kern.at[0].add(d_skip.reshape(G, C)[:, :, None] * jnp.eye(C, dtype=F32))
    eye_g = jnp.eye(OCT, dtype=F32)

    kq = kern.reshape(chunk, Q, OCT, C, C)
    blocks = []
    for j in range(chunk):
        row = []
        for jp in range(chunk):
            if jp >= j:
                blk = jnp.einsum('qgoc,gh->qgcho', kq[jp - j], eye_g)
            else:
                blk = jnp.zeros((Q, OCT, C, OCT, C), F32)
            row.append(blk)
        blocks.append(jnp.stack(row, axis=3))
    m_intra = jnp.stack(blocks, axis=1).reshape(Q, chunk * LANES, chunk * LANES)

    def state_half(lbb):
        rev = lbb[::-1].reshape(chunk, Q, OCT, N, C)
        w = jnp.einsum('jqgnc,gh->qjgchn', rev, eye_g)
        return w.reshape(Q, chunk * LANES, OCT * N)
    w_state = jnp.concatenate([state_half(lbb_re), state_half(lbb_im)], axis=-1)

    p_re = c_re[None] * pw_re[1:, :, None, :] - c_im[None] * pw_im[1:, :, None, :]
    p_im = c_re[None] * pw_im[1:, :, None, :] + c_im[None] * pw_re[1:, :, None, :]
    def out_half(p):
        pq = p.reshape(chunk, Q, OCT, C, N)
        w = jnp.einsum('jqgon,gh->qgnjho', pq, eye_g)
        return w.reshape(Q, OCT * N, chunk * LANES)
    w_out = jnp.concatenate([out_half(p_re), out_half(-p_im)], axis=1)

    lr, li = [pw_re[chunk]], [pw_im[chunk]]
    for _ in range(n_scan - 1):
        r, i = lr[-1], li[-1]
        lr.append(r * r - i * i)
        li.append(2.0 * r * i)
    lam_re = jnp.stack(lr).reshape(n_scan, G * N)
    lam_im = jnp.stack(li).reshape(n_scan, G * N)
    return m_intra.astype(BF16), w_state.astype(BF16), w_out.astype(BF16), lam_re, lam_im


def _mixer_kernel(x_ref, nm_ref, win_ref, poolw_ref, pscale_ref, mi_ref, ws_ref, wh_ref,
                  lre_ref, lim_ref, gluw_ref, glub_ref, brp_ref, brs_ref, wout_ref,
                  o_ref, zext_ref, v_ref, y_ref, carry_ref, *, tiles_per_seq, chunk):
    tm, d_model = x_ref.shape
    pool_w = zext_ref.shape[1]
    n_oct = v_ref.shape[0]
    ssm_w = n_oct * LANES
    n_chunks = tm // chunk
    n_scan = lre_ref.shape[0]
    seq_tile = pl.program_id(0) % tiles_per_seq

    @pl.when(seq_tile == 0)
    def _():
        zext_ref[0:POOL_HALO, :] = jnp.zeros((POOL_HALO, pool_w), F32)
        carry_ref[...] = jnp.zeros_like(carry_ref)

    x = x_ref[...]
    u = _rms(x, nm_ref[...]).astype(BF16)
    proj = jnp.dot(u, win_ref[...], preferred_element_type=F32)
    z = proj[:, :pool_w]
    zext_ref[POOL_HALO:, :] = z
    for q in range(n_oct):
        v_ref[q] = proj[:, pool_w + q * LANES:pool_w + (q + 1) * LANES]
    g_pool = jax.nn.sigmoid(proj[:, pool_w + ssm_w:pool_w + ssm_w + d_model])
    g_ssm = jax.nn.sigmoid(proj[:, pool_w + ssm_w + d_model:])

    pos = (seq_tile * tm + lax.broadcasted_iota(jnp.int32, (tm, 1), 0) + 1).astype(F32)
    gdim = pool_w // len(POOL_WINDOWS)
    ypool = []
    for gi, w in enumerate(POOL_WINDOWS):
        cols = slice(gi * gdim, (gi + 1) * gdim)
        acc = z[:, cols]
        for dlt in range(1, w):
            acc = acc + zext_ref[POOL_HALO - dlt:POOL_HALO - dlt + tm, cols]
        diff = acc / jnp.minimum(pos, float(w)) - z[:, cols]
        ypool.append(jnp.dot(diff.astype(BF16), poolw_ref[gi], preferred_element_type=F32))
    ypool = jnp.concatenate(ypool, axis=1) * pscale_ref[...]
    zext_ref[0:POOL_HALO, :] = zext_ref[tm:tm + POOL_HALO, :]
    p_br = jnp.dot(ypool.astype(BF16), brp_ref[...], preferred_element_type=F32)

    row = lax.broadcasted_iota(jnp.int32, (n_chunks, 1), 0)
    half = OCT * SSM_STATE
    for q in range(n_oct):
        st = slice(q * half, (q + 1) * half)
        vq = jnp.concatenate(
            [v_ref[q, pl.ds(j, n_chunks, stride=chunk), :] for j in range(chunk)],
            axis=1).astype(BF16)
        s = jnp.dot(vq, ws_ref[q], preferred_element_type=F32)
        s_re, s_im = s[:, :half], s[:, half:]
        c_re, c_im = carry_ref[0:1, st], carry_ref[1:2, st]
        l_re, l_im = lre_ref[0:1, st], lim_ref[0:1, st]
        first = row == 0
        s_re = s_re + jnp.where(first, l_re * c_re - l_im * c_im, 0.0)
        s_im = s_im + jnp.where(first, l_re * c_im + l_im * c_re, 0.0)
        for k in range(n_scan):
            sh = 1 << k
            keep = row >= sh
            r_re = jnp.where(keep, pltpu.roll(s_re, sh, axis=0), 0.0)
            r_im = jnp.where(keep, pltpu.roll(s_im, sh, axis=0), 0.0)
            l_re, l_im = lre_ref[k:k + 1, st], lim_ref[k:k + 1, st]
            s_re, s_im = (s_re + l_re * r_re - l_im * r_im,
                          s_im + l_re * r_im + l_im * r_re)
        h0_re = jnp.where(first, c_re, pltpu.roll(s_re, 1, axis=0))
        h0_im = jnp.where(first, c_im, pltpu.roll(s_im, 1, axis=0))
        carry_ref[0:1, st] = s_re[n_chunks - 1:n_chunks, :]
        carry_ref[1:2, st] = s_im[n_chunks - 1:n_chunks, :]
        yq = (jnp.dot(vq, mi_ref[q], preferred_element_type=F32)
              + jnp.dot(h0_re.astype(BF16), wh_ref[q, :half, :], preferred_element_type=F32)
              + jnp.dot(h0_im.astype(BF16), wh_ref[q, half:, :], preferred_element_type=F32))
        for j in range(chunk):
            y_ref[q, pl.ds(j, n_chunks, stride=chunk), :] = yq[:, j * LANES:(j + 1) * LANES]

    y = jnp.concatenate([y_ref[q] for q in range(n_oct)], axis=1)
    ys = jax.nn.gelu(y, approximate=True)
    glu = jnp.dot(ys.astype(BF16), gluw_ref[...], preferred_element_type=F32) + glub_ref[...]
    ys = ys * jax.nn.sigmoid(glu)
    s_br = jnp.dot(ys.astype(BF16), brs_ref[...], preferred_element_type=F32)

    merged = g_pool * p_br + g_ssm * s_br
    o_ref[...] = x + jnp.dot(merged.astype(BF16), wout_ref[...], preferred_element_type=F32)


def _mixer(x2, seq, nm, w_in, pool_w, pool_scale, tables, glu_w, glu_b, br_pool, br_ssm, w_out, tm):
    t, d = x2.shape
    mi, ws, wh, lre, lim = tables
    pool_width = pool_w.shape[0] * pool_w.shape[1]
    ssm_width = glu_w.shape[0]
    consts = [nm.reshape(1, d), w_in.astype(BF16), pool_w.astype(BF16),
              pool_scale.reshape(1, pool_width), mi, ws, wh, lre, lim,
              glu_w.astype(BF16), glu_b.reshape(1, ssm_width), br_pool.astype(BF16),
              br_ssm.astype(BF16), w_out.astype(BF16)]
    kern = functools.partial(_mixer_kernel, tiles_per_seq=seq // tm, chunk=SSM_CHUNK)
    return pl.pallas_call(
        kern,
        grid=(t // tm,),
        in_specs=[pl.BlockSpec((tm, d), lambda i: (i, 0))] + [_const_spec(c.shape) for c in consts],
        out_specs=pl.BlockSpec((tm, d), lambda i: (i, 0)),
        out_shape=jax.ShapeDtypeStruct((t, d), F32),
        scratch_shapes=[pltpu.VMEM((POOL_HALO + tm, pool_width), F32),
                        pltpu.VMEM((ssm_width // LANES, tm, LANES), F32),
                        pltpu.VMEM((ssm_width // LANES, tm, LANES), F32),
                        pltpu.VMEM((2, lre.shape[1]), F32)],
        compiler_params=pltpu.CompilerParams(dimension_semantics=("arbitrary",),
                                             vmem_limit_bytes=VMEM_LIMIT),
        name="mixer",
    )(x2, *consts)


def _kv_kernel(mem_ref, g_ref, wkv_ref, o_ref):
    mn = _rms(mem_ref[...], g_ref[...]).astype(BF16)
    o_ref[...] = jnp.dot(mn, wkv_ref[...], preferred_element_type=F32)


def _kv_proj(mem2, g, w_kv):
    m, d = mem2.shape
    return pl.pallas_call(
        _kv_kernel,
        grid=(1,),
        in_specs=[pl.BlockSpec((m, d), lambda i: (0, 0)), _const_spec((1, d)),
                  _const_spec(w_kv.shape)],
        out_specs=pl.BlockSpec((m, w_kv.shape[1]), lambda i: (0, 0)),
        out_shape=jax.ShapeDtypeStruct((m, w_kv.shape[1]), F32),
        compiler_params=pltpu.CompilerParams(vmem_limit_bytes=VMEM_LIMIT),
        name="kv_proj",
    )(mem2, g.reshape(1, d), w_kv.astype(BF16))


def _xattn_kernel(h_ref, g_ref, wq_ref, kt_ref, v_ref, wo_ref, o_ref, *, scale):
    h = h_ref[...]
    hn = _rms(h, g_ref[...]).astype(BF16)
    q = jnp.dot(hn, wq_ref[...], preferred_element_type=F32)
    n_heads, hd, _ = kt_ref.shape[1:]
    outs = []
    for hh in range(n_heads):
        qh = q[:, hh * hd:(hh + 1) * hd].astype(BF16)
        s = jnp.dot(qh, kt_ref[0, hh], preferred_element_type=F32) * scale
        s = s - jnp.max(s, axis=-1, keepdims=True)
        p = jnp.exp(s)
        p = p / jnp.sum(p, axis=-1, keepdims=True)
        outs.append(jnp.dot(p.astype(BF16), v_ref[0, hh], preferred_element_type=F32))
    o = jnp.concatenate(outs, axis=1).astype(BF16)
    o_ref[...] = h + jnp.dot(o, wo_ref[...], preferred_element_type=F32)


def _xattn(h2, seq, g, w_q, kt, v, w_o, tm):
    t, d = h2.shape
    tiles_per_seq = seq // tm
    hd = d // X_HEADS
    kern = functools.partial(_xattn_kernel, scale=hd ** -0.5)
    kv_block = (1,) + kt.shape[1:]
    return pl.pallas_call(
        kern,
        grid=(t // tm,),
        in_specs=[pl.BlockSpec((tm, d), lambda i: (i, 0)), _const_spec((1, d)),
                  _const_spec(w_q.shape),
                  pl.BlockSpec(kv_block, lambda i: (i // tiles_per_seq, 0, 0, 0)),
                  pl.BlockSpec((1,) + v.shape[1:], lambda i: (i // tiles_per_seq, 0, 0, 0)),
                  _const_spec(w_o.shape)],
        out_specs=pl.BlockSpec((tm, d), lambda i: (i, 0)),
        out_shape=jax.ShapeDtypeStruct((t, d), F32),
        compiler_params=pltpu.CompilerParams(dimension_semantics=("arbitrary",),
                                             vmem_limit_bytes=VMEM_LIMIT),
        name="xattn",
    )(h2, g.reshape(1, d), w_q.astype(BF16), kt, v, w_o.astype(BF16))


def _route(logits):
    n_g, n_eg = N_EXPERT_GROUPS, EXPERTS_PER_GROUP
    lane = lax.broadcasted_iota(jnp.int32, logits.shape, 1)
    neg = jnp.float32(-jnp.inf)
    big = jnp.int32(1 << 20)
    gl = jnp.where(lane < n_g, logits, neg)
    gmax = jnp.max(gl, axis=-1, keepdims=True)
    g_idx = jnp.min(jnp.where(gl == gmax, lane, big), axis=-1, keepdims=True)
    g_gate = 1.0 / jnp.sum(jnp.exp(gl - gmax), axis=-1, keepdims=True)
    lo = n_g + g_idx * n_eg
    el = jnp.where((lane >= lo) & (lane < lo + n_eg), logits, neg)
    t1 = jnp.max(el, axis=-1, keepdims=True)
    i1 = jnp.min(jnp.where(el == t1, lane, big), axis=-1, keepdims=True)
    el2 = jnp.where(lane == i1, neg, el)
    t2 = jnp.max(el2, axis=-1, keepdims=True)
    i2 = jnp.min(jnp.where(el2 == t2, lane, big), axis=-1, keepdims=True)
    e2 = jnp.exp(t2 - t1)
    w1 = 1.0 / (1.0 + e2)
    w2 = e2 / (1.0 + e2)
    return jnp.where(lane == i1, w1, jnp.where(lane == i2, w2, 0.0)) * g_gate


def _moe_kernel(h_ref, g_ref, wr_ref, br_ref, w1_ref, w3_ref, w2_ref, gf_ref, o_ref,
                zn_ref, comb_ref, acc_ref):
    e = pl.program_id(1)

    @pl.when(e == 0)
    def _():
        h = h_ref[...]
        zn = _rms(h, g_ref[...])
        zn_ref[...] = zn.astype(BF16)
        logits = jnp.dot(zn, wr_ref[...], preferred_element_type=F32,
                         precision=lax.Precision.HIGHEST) + br_ref[...]
        comb_ref[...] = _route(logits)
        acc_ref[...] = h

    zn = zn_ref[...]
    a = jnp.dot(zn, w1_ref[0], preferred_element_type=F32)
    b = jnp.dot(zn, w3_ref[0], preferred_element_type=F32)
    hid = (a * jax.nn.sigmoid(a) * b).astype(BF16)
    lane = lax.broadcasted_iota(jnp.int32, comb_ref.shape, 1)
    ce = jnp.sum(jnp.where(lane == N_EXPERT_GROUPS + e, comb_ref[...], 0.0), axis=-1, keepdims=True)
    acc_ref[...] += ce * jnp.dot(hid, w2_ref[0], preferred_element_type=F32)

    @pl.when(e == pl.num_programs(1) - 1)
    def _():
        o_ref[...] = _rms(acc_ref[...], gf_ref[...])


def _moe(h2, g, w_r, b_r, w1, w3, w2, g_final, tm):
    t, d = h2.shape
    n_e, _, de = w1.shape
    return pl.pallas_call(
        _moe_kernel,
        grid=(t // tm, n_e),
        in_specs=[pl.BlockSpec((tm, d), lambda i, e: (i, 0)), _const_spec((1, d)),
                  _const_spec(w_r.shape), _const_spec(b_r.shape),
                  pl.BlockSpec((1, d, de), lambda i, e: (e, 0, 0)),
                  pl.BlockSpec((1, d, de), lambda i, e: (e, 0, 0)),
                  pl.BlockSpec((1, de, d), lambda i, e: (e, 0, 0)),
                  _const_spec((1, d))],
        out_specs=pl.BlockSpec((tm, d), lambda i, e: (i, 0)),
        out_shape=jax.ShapeDtypeStruct((t, d), F32),
        scratch_shapes=[pltpu.VMEM((tm, d), BF16), pltpu.VMEM((tm, LANES), F32),
                        pltpu.VMEM((tm, d), F32)],
        compiler_params=pltpu.CompilerParams(dimension_semantics=("arbitrary", "arbitrary"),
                                             vmem_limit_bytes=VMEM_LIMIT),
        name="moe",
    )(h2, g.reshape(1, d), w_r, b_r, w1.astype(BF16), w3.astype(BF16), w2.astype(BF16),
      g_final.reshape(1, d))


def _layer(h2, mem2, batch, seq, p, l, g_final, tm):
    d = h2.shape[1]
    n_scan = int(math.log2(tm // SSM_CHUNK))
    tables = _ssm_tables(p["ssm_a_re"][l], p["ssm_a_im"][l], p["ssm_log_dt"][l], p["ssm_b_re"][l],
                         p["ssm_b_im"][l], p["ssm_c_re"][l], p["ssm_c_im"][l], p["ssm_d"][l],
                         SSM_CHUNK, n_scan)
    h2 = _mixer(h2, seq, p["norm_mix"][l], p["w_in"][l], p["pool_w"][l], p["pool_scale"][l], tables,
                p["glu_w"][l], p["glu_b"][l], p["br_pool"][l], p["br_ssm"][l], p["w_out"][l], tm)

    m_len = mem2.shape[0] // batch
    hd = d // X_HEADS
    kv = _kv_proj(mem2, p["norm_mem"][l], p["x_wkv"][l])
    kt = kv[:, :d].reshape(batch, m_len, X_HEADS, hd).transpose(0, 2, 3, 1).astype(BF16)
    v = kv[:, d:].reshape(batch, m_len, X_HEADS, hd).transpose(0, 2, 1, 3).astype(BF16)
    h2 = _xattn(h2, seq, p["norm_x"][l], p["x_wq"][l], kt, v, p["x_wo"][l], tm)

    n_g, n_eg = N_EXPERT_GROUPS, EXPERTS_PER_GROUP
    w_r = jnp.concatenate([p["router_g_w"][l], p["router_e_w"][l].reshape(d, n_g * n_eg)], axis=1)
    w_r = jnp.pad(w_r, ((0, 0), (0, LANES - w_r.shape[1])))
    b_r = jnp.concatenate([p["router_g_b"][l], p["router_e_b"][l].reshape(n_g * n_eg)])
    b_r = jnp.pad(b_r, (0, LANES - b_r.shape[0])).reshape(1, LANES)
    return _moe(h2, p["norm_ffn"][l], w_r, b_r, p["exp_w1"][l], p["exp_w3"][l], p["exp_w2"][l],
                g_final, tm)


def kernel(x, mem, norm_mix, w_in, pool_w, pool_scale, ssm_a_re, ssm_a_im, ssm_log_dt, ssm_b_re,
           ssm_b_im, ssm_c_re, ssm_c_im, ssm_d, glu_w, glu_b, br_pool, br_ssm, w_out, norm_x,
           norm_mem, x_wq, x_wkv, x_wo, norm_ffn, router_g_w, router_g_b, router_e_w, router_e_b,
           exp_w1, exp_w3, exp_w2, norm_final):
    p = dict(norm_mix=norm_mix, w_in=w_in, pool_w=pool_w, pool_scale=pool_scale, ssm_a_re=ssm_a_re,
             ssm_a_im=ssm_a_im, ssm_log_dt=ssm_log_dt, ssm_b_re=ssm_b_re, ssm_b_im=ssm_b_im,
             ssm_c_re=ssm_c_re, ssm_c_im=ssm_c_im, ssm_d=ssm_d, glu_w=glu_w, glu_b=glu_b,
             br_pool=br_pool, br_ssm=br_ssm, w_out=w_out, norm_x=norm_x, norm_mem=norm_mem,
             x_wq=x_wq, x_wkv=x_wkv, x_wo=x_wo, norm_ffn=norm_ffn, router_g_w=router_g_w,
             router_g_b=router_g_b, router_e_w=router_e_w, router_e_b=router_e_b, exp_w1=exp_w1,
             exp_w3=exp_w3, exp_w2=exp_w2)
    batch, seq, d = x.shape
    assert norm_mix.shape[0] == 1, "final norm is fused into the (single) layer's last stage"
    tm = min(512, seq)
    h2 = x.reshape(batch * seq, d)
    mem2 = mem.reshape(-1, d)
    out = _layer(h2, mem2, batch, seq, p, 0, norm_final, tm)
    return out.reshape(batch, seq, d)
```

```python
import functools
import math

import jax
import jax.numpy as jnp
from jax import lax
from jax.experimental import pallas as pl
from jax.experimental.pallas import tpu as pltpu

F32 = jnp.float32
BF16 = jnp.bfloat16

RMS_EPS = 1e-6
POOL_WINDOWS = (2, 4, 8, 16)
POOL_HALO = 16
SSM_GROUP_DIM = 16
SSM_STATE = 64
LANES = 128
SUBLANES = 8
OCT = LANES // SSM_GROUP_DIM
SSM_CHUNK = 4
X_HEADS = 4
N_EXPERT_GROUPS = 4
EXPERTS_PER_GROUP = 4
VMEM_LIMIT = 56 * 1024 * 1024


def _rms(x, g):
    return x * lax.rsqrt(jnp.mean(x * x, axis=-1, keepdims=True) + RMS_EPS) * g


def _const_spec(shape):
    nd = len(shape)
    return pl.BlockSpec(shape, lambda *_: (0,) * nd, pipeline_mode=pl.Buffered(1))


def _ssm_tables(a_re, a_im, log_dt, b_re, b_im, c_re, c_im, d_skip, chunk, n_scan):
    hi = lax.Precision.HIGHEST
    G, N = a_re.shape
    C = SSM_GROUP_DIM
    Q = G // OCT
    dt = jnp.exp(log_dt)[:, None]
    mag = jnp.exp(a_re * dt)
    lb_re, lb_im = mag * jnp.cos(a_im * dt), mag * jnp.sin(a_im * dt)
    den = a_re * a_re + a_im * a_im
    nr, ni = lb_re - 1.0, lb_im
    f_re = ((nr * a_re + ni * a_im) / den)[..., None]
    f_im = ((ni * a_re - nr * a_im) / den)[..., None]
    bb_re = f_re * b_re - f_im * b_im
    bb_im = f_re * b_im + f_im * b_re

    pw_re, pw_im = [jnp.ones_like(lb_re)], [jnp.zeros_like(lb_re)]
    for _ in range(chunk):
        r, i = pw_re[-1], pw_im[-1]
        pw_re.append(r * lb_re - i * lb_im)
        pw_im.append(r * lb_im + i * lb_re)
    pw_re, pw_im = jnp.stack(pw_re), jnp.stack(pw_im)

    lbb_re = pw_re[:chunk, :, :, None] * bb_re - pw_im[:chunk, :, :, None] * bb_im
    lbb_im = pw_re[:chunk, :, :, None] * bb_im + pw_im[:chunk, :, :, None] * bb_re
    kern = (jnp.einsum('gon,dgnc->dgoc', c_re, lbb_re, precision=hi)
            - jnp.einsum('gon,dgnc->dgoc', c_im, lbb_im, precision=hi))
    kern = kern.at[0].add(d_skip.reshape(G, C)[:, :, None] * jnp.eye(C, dtype=F32))
    eye_g = jnp.eye(OCT, dtype=F32)

    kq = kern.reshape(chunk, Q, OCT, C, C)
    blocks = []
    for j in range(chunk):
        row = []
        for jp in range(chunk):
            if jp >= j:
                blk = jnp.einsum('qgoc,gh->qgcho', kq[jp - j], eye_g)
            else:
                blk = jnp.zeros((Q, OCT, C, OCT, C), F32)
            row.append(blk)
        blocks.append(jnp.stack(row, axis=3))
    m_intra = jnp.stack(blocks, axis=1).reshape(Q, chunk * LANES, chunk * LANES)

    def state_half(lbb):
        rev = lbb[::-1].reshape(chunk, Q, OCT, N, C)
        w = jnp.einsum('jqgnc,gh->qjgchn', rev, eye_g)
        return w.reshape(Q, chunk * LANES, OCT * N)
    w_state = jnp.concatenate([state_half(lbb_re), state_half(lbb_im)], axis=-1)

    p_re = c_re[None] * pw_re[1:, :, None, :] - c_im[None] * pw_im[1:, :, None, :]
    p_im = c_re[None] * pw_im[1:, :, None, :] + c_im[None] * pw_re[1:, :, None, :]
    def out_half(p):
        pq = p.reshape(chunk, Q, OCT, C, N)
        w = jnp.einsum('jqgon,gh->qgnjho', pq, eye_g)
        return w.reshape(Q, OCT * N, chunk * LANES)
    w_out = jnp.concatenate([out_half(p_re), out_half(-p_im)], axis=1)

    lr, li = [pw_re[chunk]], [pw_im[chunk]]
    for _ in range(n_scan - 1):
        r, i = lr[-1], li[-1]
        lr.append(r * r - i * i)
        li.append(2.0 * r * i)
    lam_re = jnp.stack(lr).reshape(n_scan, G * N)
    lam_im = jnp.stack(li).reshape(n_scan, G * N)
    return m_intra.astype(BF16), w_state.astype(BF16), w_out.astype(BF16), lam_re, lam_im


def _mixer_kernel(x_ref, nm_ref, win_ref, poolw_ref, pscale_ref, mi_ref, ws_ref, wh_ref,
                  lre_ref, lim_ref, gluw_ref, glub_ref, brp_ref, brs_ref, wout_ref,
                  o_ref, zext_ref, v_ref, y_ref, carry_ref, *, tiles_per_seq, chunk):
    tm, d_model = x_ref.shape
    pool_w = zext_ref.shape[1]
    n_oct = v_ref.shape[0]
    ssm_w = n_oct * LANES
    n_chunks = tm // chunk
    n_scan = lre_ref.shape[0]
    seq_tile = pl.program_id(0) % tiles_per_seq

    @pl.when(seq_tile == 0)
    def _():
        zext_ref[0:POOL_HALO, :] = jnp.zeros((POOL_HALO, pool_w), F32)
        carry_ref[...] = jnp.zeros_like(carry_ref)

    x = x_ref[...]
    u = _rms(x, nm_ref[...]).astype(BF16)
    proj = jnp.dot(u, win_ref[...], preferred_element_type=F32)
    z = proj[:, :pool_w]
    zext_ref[POOL_HALO:, :] = z
    for q in range(n_oct):
        v_ref[q] = proj[:, pool_w + q * LANES:pool_w + (q + 1) * LANES]
    g_pool = jax.nn.sigmoid(proj[:, pool_w + ssm_w:pool_w + ssm_w + d_model])
    g_ssm = jax.nn.sigmoid(proj[:, pool_w + ssm_w + d_model:])

    pos = (seq_tile * tm + lax.broadcasted_iota(jnp.int32, (tm, 1), 0) + 1).astype(F32)
    gdim = pool_w // len(POOL_WINDOWS)
    ypool = []
    for gi, w in enumerate(POOL_WINDOWS):
        cols = slice(gi * gdim, (gi + 1) * gdim)
        acc = z[:, cols]
        for dlt in range(1, w):
            acc = acc + zext_ref[POOL_HALO - dlt:POOL_HALO - dlt + tm, cols]
        diff = acc / jnp.minimum(pos, float(w)) - z[:, cols]
        ypool.append(jnp.dot(diff.astype(BF16), poolw_ref[gi], preferred_element_type=F32))
    ypool = jnp.concatenate(ypool, axis=1) * pscale_ref[...]
    zext_ref[0:POOL_HALO, :] = zext_ref[tm:tm + POOL_HALO, :]
    p_br = jnp.dot(ypool.astype(BF16), brp_ref[...], preferred_element_type=F32)

    row = lax.broadcasted_iota(jnp.int32, (n_chunks, 1), 0)
    half = OCT * SSM_STATE
    for q in range(n_oct):
        st = slice(q * half, (q + 1) * half)
        vq = jnp.concatenate(
            [v_ref[q, pl.ds(j, n_chunks, stride=chunk), :] for j in range(chunk)],
            axis=1).astype(BF16)
        s = jnp.dot(vq, ws_ref[q], preferred_element_type=F32)
        s_re, s_im = s[:, :half], s[:, half:]
        c_re, c_im = carry_ref[0:1, st], carry_ref[1:2, st]
        l_re, l_im = lre_ref[0:1, st], lim_ref[0:1, st]
        first = row == 0
        s_re = s_re + jnp.where(first, l_re * c_re - l_im * c_im, 0.0)
        s_im = s_im + jnp.where(first, l_re * c_im + l_im * c_re, 0.0)
        for k in range(n_scan):
            sh = 1 << k
            keep = row >= sh
            r_re = jnp.where(keep, pltpu.roll(s_re, sh, axis=0), 0.0)
            r_im = jnp.where(keep, pltpu.roll(s_im, sh, axis=0), 0.0)
            l_re, l_im = lre_ref[k:k + 1, st], lim_ref[k:k + 1, st]
            s_re, s_im = (s_re + l_re * r_re - l_im * r_im,
                          s_im + l_re * r_im + l_im * r_re)
        h0_re = jnp.where(first, c_re, pltpu.roll(s_re, 1, axis=0))
        h0_im = jnp.where(first, c_im, pltpu.roll(s_im, 1, axis=0))
        carry_ref[0:1, st] = s_re[n_chunks - 1:n_chunks, :]
        carry_ref[1:2, st] = s_im[n_chunks - 1:n_chunks, :]
        yq = (jnp.dot(vq, mi_ref[q], preferred_element_type=F32)
              + jnp.dot(h0_re.astype(BF16), wh_ref[q, :half, :], preferred_element_type=F32)
              + jnp.dot(h0_im.astype(BF16), wh_ref[q, half:, :], preferred_element_type=F32))
        for j in range(chunk):
            y_ref[q, pl.ds(j, n_chunks, stride=chunk), :] = yq[:, j * LANES:(j + 1) * LANES]

    y = jnp.concatenate([y_ref[q] for q in range(n_oct)], axis=1)
    ys = jax.nn.gelu(y, approximate=True)
    glu = jnp.dot(ys.astype(BF16), gluw_ref[...], preferred_element_type=F32) + glub_ref[...]
    ys = ys * jax.nn.sigmoid(glu)
    s_br = jnp.dot(ys.astype(BF16), brs_ref[...], preferred_element_type=F32)

    merged = g_pool * p_br + g_ssm * s_br
    o_ref[...] = x + jnp.dot(merged.astype(BF16), wout_ref[...], preferred_element_type=F32)


def _mixer(x2, seq, nm, w_in, pool_w, pool_scale, tables, glu_w, glu_b, br_pool, br_ssm, w_out, tm):
    t, d = x2.shape
    mi, ws, wh, lre, lim = tables
    pool_width = pool_w.shape[0] * pool_w.shape[1]
    ssm_width = glu_w.shape[0]
    consts = [nm.reshape(1, d), w_in.astype(BF16), pool_w.astype(BF16),
              pool_scale.reshape(1, pool_width), mi, ws, wh, lre, lim,
              glu_w.astype(BF16), glu_b.reshape(1, ssm_width), br_pool.astype(BF16),
              br_ssm.astype(BF16), w_out.astype(BF16)]
    kern = functools.partial(_mixer_kernel, tiles_per_seq=seq // tm, chunk=SSM_CHUNK)
    return pl.pallas_call(
        kern,
        grid=(t // tm,),
        in_specs=[pl.BlockSpec((tm, d), lambda i: (i, 0))] + [_const_spec(c.shape) for c in consts],
        out_specs=pl.BlockSpec((tm, d), lambda i: (i, 0)),
        out_shape=jax.ShapeDtypeStruct((t, d), F32),
        scratch_shapes=[pltpu.VMEM((POOL_HALO + tm, pool_width), F32),
                        pltpu.VMEM((ssm_width // LANES, tm, LANES), F32),
                        pltpu.VMEM((ssm_width // LANES, tm, LANES), F32),
                        pltpu.VMEM((2, lre.shape[1]), F32)],
        compiler_params=pltpu.CompilerParams(dimension_semantics=("arbitrary",),
                                             vmem_limit_bytes=VMEM_LIMIT),
        name="mixer",
    )(x2, *consts)


def _kv_kernel(mem_ref, g_ref, wkv_ref, o_ref):
    mn = _rms(mem_ref[...], g_ref[...]).astype(BF16)
    o_ref[...] = jnp.dot(mn, wkv_ref[...], preferred_element_type=F32)


def _kv_proj(mem2, g, w_kv):
    m, d = mem2.shape
    return pl.pallas_call(
        _kv_kernel,
        grid=(1,),
        in_specs=[pl.BlockSpec((m, d), lambda i: (0, 0)), _const_spec((1, d)),
                  _const_spec(w_kv.shape)],
        out_specs=pl.BlockSpec((m, w_kv.shape[1]), lambda i: (0, 0)),
        out_shape=jax.ShapeDtypeStruct((m, w_kv.shape[1]), F32),
        compiler_params=pltpu.CompilerParams(vmem_limit_bytes=VMEM_LIMIT),
        name="kv_proj",
    )(mem2, g.reshape(1, d), w_kv.astype(BF16))


def _route(logits):
    n_g, n_eg = N_EXPERT_GROUPS, EXPERTS_PER_GROUP
    lane = lax.broadcasted_iota(jnp.int32, logits.shape, 1)
    neg = jnp.float32(-jnp.inf)
    big = jnp.int32(1 << 20)
    gl = jnp.where(lane < n_g, logits, neg)
    gmax = jnp.max(gl, axis=-1, keepdims=True)
    g_idx = jnp.min(jnp.where(gl == gmax, lane, big), axis=-1, keepdims=True)
    g_gate = 1.0 / jnp.sum(jnp.exp(gl - gmax), axis=-1, keepdims=True)
    lo = n_g + g_idx * n_eg
    el = jnp.where((lane >= lo) & (lane < lo + n_eg), logits, neg)
    t1 = jnp.max(el, axis=-1, keepdims=True)
    i1 = jnp.min(jnp.where(el == t1, lane, big), axis=-1, keepdims=True)
    el2 = jnp.where(lane == i1, neg, el)
    t2 = jnp.max(el2, axis=-1, keepdims=True)
    i2 = jnp.min(jnp.where(el2 == t2, lane, big), axis=-1, keepdims=True)
    e2 = jnp.exp(t2 - t1)
    w1 = 1.0 / (1.0 + e2)
    w2 = e2 / (1.0 + e2)
    comb = jnp.where(lane == i1, w1, jnp.where(lane == i2, w2, 0.0)) * g_gate
    moved = comb * 0.0
    for g in range(n_g):
        moved = moved + pltpu.roll(comb, LANES - (n_g + g * n_eg), axis=1)
    return jnp.where(lane < n_eg, moved, jnp.where(lane == n_eg, g_idx.astype(F32), 0.0))


def _xattn_kernel(h_ref, g_ref, wq_ref, kt_ref, v_ref, wo_ref, gffn_ref, wr_ref, br_ref, o_ref,
                  *, scale):
    h = h_ref[...]
    d = h.shape[1]
    hn = _rms(h, g_ref[...]).astype(BF16)
    q = jnp.dot(hn, wq_ref[...], preferred_element_type=F32)
    n_heads, hd, _ = kt_ref.shape[1:]
    outs = []
    for hh in range(n_heads):
        qh = q[:, hh * hd:(hh + 1) * hd].astype(BF16)
        s = jnp.dot(qh, kt_ref[0, hh], preferred_element_type=F32) * scale
        s = s - jnp.max(s, axis=-1, keepdims=True)
        p = jnp.exp(s)
        p = p / jnp.sum(p, axis=-1, keepdims=True)
        outs.append(jnp.dot(p.astype(BF16), v_ref[0, hh], preferred_element_type=F32))
    o = jnp.concatenate(outs, axis=1).astype(BF16)
    h_out = h + jnp.dot(o, wo_ref[...], preferred_element_type=F32)
    zn = _rms(h_out, gffn_ref[...])
    logits = jnp.dot(zn, wr_ref[...], preferred_element_type=F32,
                     precision=lax.Precision.HIGHEST) + br_ref[...]
    o_ref[:, :d] = h_out
    o_ref[:, d:] = _route(logits)


def _xattn(h2, seq, g, w_q, kt, v, w_o, g_ffn, w_r, b_r, tm):
    t, d = h2.shape
    tiles_per_seq = seq // tm
    hd = d // X_HEADS
    kern = functools.partial(_xattn_kernel, scale=hd ** -0.5)
    return pl.pallas_call(
        kern,
        grid=(t // tm,),
        in_specs=[pl.BlockSpec((tm, d), lambda i: (i, 0)), _const_spec((1, d)),
                  _const_spec(w_q.shape),
                  pl.BlockSpec((1,) + kt.shape[1:], lambda i: (i // tiles_per_seq, 0, 0, 0)),
                  pl.BlockSpec((1,) + v.shape[1:], lambda i: (i // tiles_per_seq, 0, 0, 0)),
                  _const_spec(w_o.shape), _const_spec((1, d)), _const_spec(w_r.shape),
                  _const_spec(b_r.shape)],
        out_specs=pl.BlockSpec((tm, d + LANES), lambda i: (i, 0)),
        out_shape=jax.ShapeDtypeStruct((t, d + LANES), F32),
        compiler_params=pltpu.CompilerParams(dimension_semantics=("arbitrary",),
                                             vmem_limit_bytes=VMEM_LIMIT),
        name="xattn",
    )(h2, g.reshape(1, d), w_q.astype(BF16), kt, v, w_o.astype(BF16), g_ffn.reshape(1, d), w_r, b_r)


def _dispatch_tables(g_idx, tm, n_groups):
    t = g_idx.shape[0]
    n_tiles = t // tm + n_groups
    i32 = jnp.int32
    onehot = (g_idx[:, None] == jnp.arange(n_groups, dtype=i32)[None, :]).astype(i32)
    csum = jnp.cumsum(onehot, axis=0)
    counts = csum[-1]
    rank = jnp.take_along_axis(csum, g_idx[:, None], axis=1)[:, 0] - 1
    tiles_g = (counts + tm - 1) // tm
    tile_end = jnp.cumsum(tiles_g)
    tile_start = tile_end - tiles_g
    slot = tile_start[g_idx] * tm + rank
    tok_of_slot = jnp.zeros((n_tiles * tm,), i32).at[slot].set(jnp.arange(t, dtype=i32))
    tile_ids = jnp.arange(n_tiles, dtype=i32)
    tile_group = jnp.minimum(jnp.sum((tile_ids[:, None] >= tile_end[None, :]).astype(i32), axis=1),
                             n_groups - 1)
    tile_valid = jnp.clip(counts[tile_group] - (tile_ids - tile_start[tile_group]) * tm, 0, tm)
    tile_valid = jnp.where(tile_ids < tile_end[-1], tile_valid, 0)
    return tok_of_slot, tile_group.astype(i32), tile_valid.astype(i32)


def _moe_kernel(tok_ref, tg_ref, tv_ref, hx_hbm, g_ref, w1_ref, w3_ref, w2_ref, gf_ref, out_hbm,
                xbuf, obuf, gsem, ssem):
    i = pl.program_id(0)
    n = pl.num_programs(0)
    tm = xbuf.shape[1]
    d = obuf.shape[2]
    cur = i % 2

    def row_in(tok, buf, r):
        return pltpu.make_async_copy(hx_hbm.at[pl.ds(tok, 1), :], xbuf.at[buf, pl.ds(r, 1), :],
                                     gsem.at[buf])

    def row_out(tok, buf, r):
        return pltpu.make_async_copy(obuf.at[buf, pl.ds(r, 1), :], out_hbm.at[pl.ds(tok, 1), :],
                                     ssem.at[buf])

    def gather_start(tile, buf):
        def body(r, c):
            row_in(tok_ref[tile * tm + r], buf, r).start()
            return c
        lax.fori_loop(0, tm, body, 0)

    def gather_wait(buf):
        pltpu.make_async_copy(hx_hbm.at[pl.ds(0, tm), :], xbuf.at[buf], gsem.at[buf]).wait()

    def scatter_start(tile, buf):
        def body(r, c):
            row_out(tok_ref[tile * tm + r], buf, r).start()
            return c
        lax.fori_loop(0, tv_ref[tile], body, 0)

    def scatter_wait(tile, buf):
        nv = tv_ref[tile]
        n_blk = pl.multiple_of((nv // SUBLANES) * SUBLANES, SUBLANES)

        @pl.when(n_blk > 0)
        def _():
            pltpu.make_async_copy(obuf.at[buf, pl.ds(0, n_blk), :], out_hbm.at[pl.ds(0, n_blk), :],
                                  ssem.at[buf]).wait()

        def body(r, c):
            row_out(0, buf, 0).wait()
            return c
        lax.fori_loop(0, nv - n_blk, body, 0)

    @pl.when(i == 0)
    def _():
        gather_start(0, 0)

    gather_wait(cur)

    @pl.when(i + 1 < n)
    def _():
        gather_start(i + 1, 1 - cur)

    @pl.when(i >= 2)
    def _():
        scatter_wait(i - 2, cur)

    @pl.when(tv_ref[i] > 0)
    def _():
        xt = xbuf[cur]
        h = xt[:, :d]
        zn = _rms(h, g_ref[...]).astype(BF16)
        acc = h
        for e in range(w1_ref.shape[0]):
            a = jnp.dot(zn, w1_ref[e], preferred_element_type=F32)
            b = jnp.dot(zn, w3_ref[e], preferred_element_type=F32)
            hid = (a * jax.nn.sigmoid(a) * b).astype(BF16)
            acc = acc + xt[:, d + e:d + e + 1] * jnp.dot(hid, w2_ref[e], preferred_element_type=F32)
        obuf[cur] = _rms(acc, gf_ref[...])

    scatter_start(i, cur)

    @pl.when(i == n - 1)
    def _():
        @pl.when(i >= 1)
        def _():
            scatter_wait(i - 1, 1 - cur)
        scatter_wait(i, cur)


def _moe(hx, g, w1, w3, w2, g_final, tm):
    t, dx = hx.shape
    d = dx - LANES
    n_eg = EXPERTS_PER_GROUP
    de = w1.shape[2]
    g_idx = hx[:, d + n_eg].astype(jnp.int32)
    tok_of_slot, tile_group, tile_valid = _dispatch_tables(g_idx, tm, N_EXPERT_GROUPS)
    n_tiles = tile_group.shape[0]
    grid_spec = pltpu.PrefetchScalarGridSpec(
        num_scalar_prefetch=3,
        grid=(n_tiles,),
        in_specs=[pl.BlockSpec(memory_space=pl.ANY),
                  pl.BlockSpec((1, d), lambda i, tok, tg, tv: (0, 0)),
                  pl.BlockSpec((n_eg, d, de), lambda i, tok, tg, tv: (tg[i], 0, 0)),
                  pl.BlockSpec((n_eg, d, de), lambda i, tok, tg, tv: (tg[i], 0, 0)),
                  pl.BlockSpec((n_eg, de, d), lambda i, tok, tg, tv: (tg[i], 0, 0)),
                  pl.BlockSpec((1, d), lambda i, tok, tg, tv: (0, 0))],
        out_specs=pl.BlockSpec(memory_space=pl.ANY),
        scratch_shapes=[pltpu.VMEM((2, tm, dx), F32), pltpu.VMEM((2, tm, d), F32),
                        pltpu.SemaphoreType.DMA((2,)), pltpu.SemaphoreType.DMA((2,))])
    return pl.pallas_call(
        _moe_kernel,
        grid_spec=grid_spec,
        out_shape=jax.ShapeDtypeStruct((t, d), F32),
        compiler_params=pltpu.CompilerParams(dimension_semantics=("arbitrary",),
                                             vmem_limit_bytes=VMEM_LIMIT),
        name="moe",
    )(tok_of_slot, tile_group, tile_valid, hx, g.reshape(1, d), w1.astype(BF16), w3.astype(BF16),
      w2.astype(BF16), g_final.reshape(1, d))


def _layer(h2, mem2, batch, seq, p, l, g_final, tm):
    d = h2.shape[1]
    n_scan = int(math.log2(tm // SSM_CHUNK))
    tables = _ssm_tables(p["ssm_a_re"][l], p["ssm_a_im"][l], p["ssm_log_dt"][l], p["ssm_b_re"][l],
                         p["ssm_b_im"][l], p["ssm_c_re"][l], p["ssm_c_im"][l], p["ssm_d"][l],
                         SSM_CHUNK, n_scan)
    h2 = _mixer(h2, seq, p["norm_mix"][l], p["w_in"][l], p["pool_w"][l], p["pool_scale"][l], tables,
                p["glu_w"][l], p["glu_b"][l], p["br_pool"][l], p["br_ssm"][l], p["w_out"][l], tm)

    m_len = mem2.shape[0] // batch
    hd = d // X_HEADS
    kv = _kv_proj(mem2, p["norm_mem"][l], p["x_wkv"][l])
    kt = kv[:, :d].reshape(batch, m_len, X_HEADS, hd).transpose(0, 2, 3, 1).astype(BF16)
    v = kv[:, d:].reshape(batch, m_len, X_HEADS, hd).transpose(0, 2, 1, 3).astype(BF16)

    n_g, n_eg = N_EXPERT_GROUPS, EXPERTS_PER_GROUP
    w_r = jnp.concatenate([p["router_g_w"][l], p["router_e_w"][l].reshape(d, n_g * n_eg)], axis=1)
    w_r = jnp.pad(w_r, ((0, 0), (0, LANES - w_r.shape[1])))
    b_r = jnp.concatenate([p["router_g_b"][l], p["router_e_b"][l].reshape(n_g * n_eg)])
    b_r = jnp.pad(b_r, (0, LANES - b_r.shape[0])).reshape(1, LANES)
    hx = _xattn(h2, seq, p["norm_x"][l], p["x_wq"][l], kt, v, p["x_wo"][l], p["norm_ffn"][l],
                w_r, b_r, tm)
    return _moe(hx, p["norm_ffn"][l], p["exp_w1"][l], p["exp_w3"][l], p["exp_w2"][l], g_final, tm)


def kernel(x, mem, norm_mix, w_in, pool_w, pool_scale, ssm_a_re, ssm_a_im, ssm_log_dt, ssm_b_re,
           ssm_b_im, ssm_c_re, ssm_c_im, ssm_d, glu_w, glu_b, br_pool, br_ssm, w_out, norm_x,
           norm_mem, x_wq, x_wkv, x_wo, norm_ffn, router_g_w, router_g_b, router_e_w, router_e_b,
           exp_w1, exp_w3, exp_w2, norm_final):
    p = dict(norm_mix=norm_mix, w_in=w_in, pool_w=pool_w, pool_scale=pool_scale, ssm_a_re=ssm_a_re,
             ssm_a_im=ssm_a_im, ssm_log_dt=ssm_log_dt, ssm_b_re=ssm_b_re, ssm_b_im=ssm_b_im,
             ssm_c_re=ssm_c_re, ssm_c_im=ssm_c_im, ssm_d=ssm_d, glu_w=glu_w, glu_b=glu_b,
             br_pool=br_pool, br_ssm=br_ssm, w_out=w_out, norm_x=norm_x, norm_mem=norm_mem,
             x_wq=x_wq, x_wkv=x_wkv, x_wo=x_wo, norm_ffn=norm_ffn, router_g_w=router_g_w,
             router_g_b=router_g_b, router_e_w=router_e_w, router_e_b=router_e_b, exp_w1=exp_w1,
             exp_w3=exp_w3, exp_w2=exp_w2)
    batch, seq, d = x.shape
    assert norm_mix.shape[0] == 1, "final norm is fused into the (single) layer's last stage"
    tm = min(512, seq)
    h2 = x.reshape(batch * seq, d)
    mem2 = mem.reshape(-1, d)
    out = _layer(h2, mem2, batch, seq, p, 0, norm_final, tm)
    return out.reshape(batch, seq, d)
```

```python
import functools
import math

import jax
import jax.numpy as jnp
from jax import lax
from jax.experimental import pallas as pl
from jax.experimental.pallas import tpu as pltpu

F32 = jnp.float32
BF16 = jnp.bfloat16

RMS_EPS = 1e-6
POOL_WINDOWS = (2, 4, 8, 16)
POOL_HALO = 16
SSM_GROUP_DIM = 16
SSM_STATE = 64
LANES = 128
SUBLANES = 8
OCT = LANES // SSM_GROUP_DIM
SSM_CHUNK = 4
X_HEADS = 4
N_EXPERT_GROUPS = 4
EXPERTS_PER_GROUP = 4
VMEM_LIMIT = 56 * 1024 * 1024


def _rms(x, g):
    return x * lax.rsqrt(jnp.mean(x * x, axis=-1, keepdims=True) + RMS_EPS) * g


def _const_spec(shape):
    nd = len(shape)
    return pl.BlockSpec(shape, lambda *_: (0,) * nd, pipeline_mode=pl.Buffered(1))


def _ssm_tables(a_re, a_im, log_dt, b_re, b_im, c_re, c_im, d_skip, chunk, n_scan):
    hi = lax.Precision.HIGHEST
    G, N = a_re.shape
    C = SSM_GROUP_DIM
    Q = G // OCT
    dt = jnp.exp(log_dt)[:, None]
    mag = jnp.exp(a_re * dt)
    lb_re, lb_im = mag * jnp.cos(a_im * dt), mag * jnp.sin(a_im * dt)
    den = a_re * a_re + a_im * a_im
    nr, ni = lb_re - 1.0, lb_im
    f_re = ((nr * a_re + ni * a_im) / den)[..., None]
    f_im = ((ni * a_re - nr * a_im) / den)[..., None]
    bb_re = f_re * b_re - f_im * b_im
    bb_im = f_re * b_im + f_im * b_re

    pw_re, pw_im = [jnp.ones_like(lb_re)], [jnp.zeros_like(lb_re)]
    for _ in range(chunk):
        r, i = pw_re[-1], pw_im[-1]
        pw_re.append(r * lb_re - i * lb_im)
        pw_im.append(r * lb_im + i * lb_re)
    pw_re, pw_im = jnp.stack(pw_re), jnp.stack(pw_im)

    lbb_re = pw_re[:chunk, :, :, None] * bb_re - pw_im[:chunk, :, :, None] * bb_im
    lbb_im = pw_re[:chunk, :, :, None] * bb_im + pw_im[:chunk, :, :, None] * bb_re
    kern = (jnp.einsum('gon,dgnc->dgoc', c_re, lbb_re, precision=hi)
            - jnp.einsum('gon,dgnc->dgoc', c_im, lbb_im, precision=hi))
    kern = kern.at[0].add(d_skip.reshape(G, C)[:, :, None] * jnp.eye(C, dtype=F32))
    eye_g = jnp.eye(OCT, dtype=F32)

    kq = kern.reshape(chunk, Q, OCT, C, C)
    blocks = []
    for j in range(chunk):
        row = []
        for jp in range(chunk):
            if jp >= j:
                blk = jnp.einsum('qgoc,gh->qgcho', kq[jp - j], eye_g)
            else:
                blk = jnp.zeros((Q, OCT, C, OCT, C), F32)
            row.append(blk)
        blocks.append(jnp.stack(row, axis=3))
    m_intra = jnp.stack(blocks, axis=1).reshape(Q, chunk * LANES, chunk * LANES)

    def state_half(lbb):
        rev = lbb[::-1].reshape(chunk, Q, OCT, N, C)
        w = jnp.einsum('jqgnc,gh->qjgchn', rev, eye_g)
        return w.reshape(Q, chunk * LANES, OCT * N)
    w_state = jnp.concatenate([state_half(lbb_re), state_half(lbb_im)], axis=-1)

    p_re = c_re[None] * pw_re[1:, :, None, :] - c_im[None] * pw_im[1:, :, None, :]
    p_im = c_re[None] * pw_im[1:, :, None, :] + c_im[None] * pw_re[1:, :, None, :]
    def out_half(p):
        pq = p.reshape(chunk, Q, OCT, C, N)
        w = jnp.einsum('jqgon,gh->qgnjho', pq, eye_g)
        return w.reshape(Q, OCT * N, chunk * LANES)
    w_out = jnp.concatenate([out_half(p_re), out_half(-p_im)], axis=1)

    lr, li = [pw_re[chunk]], [pw_im[chunk]]
    for _ in range(n_scan - 1):
        r, i = lr[-1], li[-1]
        lr.append(r * r - i * i)
        li.append(2.0 * r * i)
    lam_re = jnp.stack(lr).reshape(n_scan, G * N)
    lam_im = jnp.stack(li).reshape(n_scan, G * N)
    return m_intra.astype(BF16), w_state.astype(BF16), w_out.astype(BF16), lam_re, lam_im


def _mixer_kernel(x_ref, nm_ref, win_ref, poolw_ref, pscale_ref, mi_ref, ws_ref, wh_ref,
                  lre_ref, lim_ref, gluw_ref, glub_ref, brp_ref, brs_ref, wout_ref,
                  o_ref, zext_ref, v_ref, y_ref, carry_ref, *, tiles_per_seq, chunk):
    tm, d_model = x_ref.shape
    pool_w = zext_ref.shape[1]
    n_oct = v_ref.shape[0]
    ssm_w = n_oct * LANES
    n_chunks = tm // chunk
    n_scan = lre_ref.shape[0]
    seq_tile = pl.program_id(0) % tiles_per_seq

    @pl.when(seq_tile == 0)
    def _():
        zext_ref[0:POOL_HALO, :] = jnp.zeros((POOL_HALO, pool_w), F32)
        carry_ref[...] = jnp.zeros_like(carry_ref)

    x = x_ref[...]
    u = _rms(x, nm_ref[...]).astype(BF16)
    proj = jnp.dot(u, win_ref[...], preferred_element_type=F32)
    z = proj[:, :pool_w]
    zext_ref[POOL_HALO:, :] = z
    for q in range(n_oct):
        v_ref[q] = proj[:, pool_w + q * LANES:pool_w + (q + 1) * LANES]
    g_pool = jax.nn.sigmoid(proj[:, pool_w + ssm_w:pool_w + ssm_w + d_model])
    g_ssm = jax.nn.sigmoid(proj[:, pool_w + ssm_w + d_model:])

    pos = (seq_tile * tm + lax.broadcasted_iota(jnp.int32, (tm, 1), 0) + 1).astype(F32)
    gdim = pool_w // len(POOL_WINDOWS)
    ypool = []
    for gi, w in enumerate(POOL_WINDOWS):
        cols = slice(gi * gdim, (gi + 1) * gdim)
        acc = z[:, cols]
        for dlt in range(1, w):
            acc = acc + zext_ref[POOL_HALO - dlt:POOL_HALO - dlt + tm, cols]
        diff = acc / jnp.minimum(pos, float(w)) - z[:, cols]
        ypool.append(jnp.dot(diff.astype(BF16), poolw_ref[gi], preferred_element_type=F32))
    ypool = jnp.concatenate(ypool, axis=1) * pscale_ref[...]
    zext_ref[0:POOL_HALO, :] = zext_ref[tm:tm + POOL_HALO, :]
    p_br = jnp.dot(ypool.astype(BF16), brp_ref[...], preferred_element_type=F32)

    row = lax.broadcasted_iota(jnp.int32, (n_chunks, 1), 0)
    half = OCT * SSM_STATE
    for q in range(n_oct):
        st = slice(q * half, (q + 1) * half)
        vq = jnp.concatenate(
            [v_ref[q, pl.ds(j, n_chunks, stride=chunk), :] for j in range(chunk)],
            axis=1).astype(BF16)
        s = jnp.dot(vq, ws_ref[q], preferred_element_type=F32)
        s_re, s_im = s[:, :half], s[:, half:]
        c_re, c_im = carry_ref[0:1, st], carry_ref[1:2, st]
        l_re, l_im = lre_ref[0:1, st], lim_ref[0:1, st]
        first = row == 0
        s_re = s_re + jnp.where(first, l_re * c_re - l_im * c_im, 0.0)
        s_im = s_im + jnp.where(first, l_re * c_im + l_im * c_re, 0.0)
        for k in range(n_scan):
            sh = 1 << k
            keep = row >= sh
            r_re = jnp.where(keep, pltpu.roll(s_re, sh, axis=0), 0.0)
            r_im = jnp.where(keep, pltpu.roll(s_im, sh, axis=0), 0.0)
            l_re, l_im = lre_ref[k:k + 1, st], lim_ref[k:k + 1, st]
            s_re, s_im = (s_re + l_re * r_re - l_im * r_im,
                          s_im + l_re * r_im + l_im * r_re)
        h0_re = jnp.where(first, c_re, pltpu.roll(s_re, 1, axis=0))
        h0_im = jnp.where(first, c_im, pltpu.roll(s_im, 1, axis=0))
        carry_ref[0:1, st] = s_re[n_chunks - 1:n_chunks, :]
        carry_ref[1:2, st] = s_im[n_chunks - 1:n_chunks, :]
        yq = (jnp.dot(vq, mi_ref[q], preferred_element_type=F32)
              + jnp.dot(h0_re.astype(BF16), wh_ref[q, :half, :], preferred_element_type=F32)
              + jnp.dot(h0_im.astype(BF16), wh_ref[q, half:, :], preferred_element_type=F32))
        for j in range(chunk):
            y_ref[q, pl.ds(j, n_chunks, stride=chunk), :] = yq[:, j * LANES:(j + 1) * LANES]

    y = jnp.concatenate([y_ref[q] for q in range(n_oct)], axis=1)
    ys = jax.nn.gelu(y, approximate=True)
    glu = jnp.dot(ys.astype(BF16), gluw_ref[...], preferred_element_type=F32) + glub_ref[...]
    ys = ys * jax.nn.sigmoid(glu)
    s_br = jnp.dot(ys.astype(BF16), brs_ref[...], preferred_element_type=F32)

    merged = g_pool * p_br + g_ssm * s_br
    o_ref[...] = x + jnp.dot(merged.astype(BF16), wout_ref[...], preferred_element_type=F32)


def _mixer(x2, seq, nm, w_in, pool_w, pool_scale, tables, glu_w, glu_b, br_pool, br_ssm, w_out, tm):
    t, d = x2.shape
    mi, ws, wh, lre, lim = tables
    pool_width = pool_w.shape[0] * pool_w.shape[1]
    ssm_width = glu_w.shape[0]
    consts = [nm.reshape(1, d), w_in.astype(BF16), pool_w.astype(BF16),
              pool_scale.reshape(1, pool_width), mi, ws, wh, lre, lim,
              glu_w.astype(BF16), glu_b.reshape(1, ssm_width), br_pool.astype(BF16),
              br_ssm.astype(BF16), w_out.astype(BF16)]
    kern = functools.partial(_mixer_kernel, tiles_per_seq=seq // tm, chunk=SSM_CHUNK)
    return pl.pallas_call(
        kern,
        grid=(t // tm,),
        in_specs=[pl.BlockSpec((tm, d), lambda i: (i, 0))] + [_const_spec(c.shape) for c in consts],
        out_specs=pl.BlockSpec((tm, d), lambda i: (i, 0)),
        out_shape=jax.ShapeDtypeStruct((t, d), F32),
        scratch_shapes=[pltpu.VMEM((POOL_HALO + tm, pool_width), F32),
                        pltpu.VMEM((ssm_width // LANES, tm, LANES), F32),
                        pltpu.VMEM((ssm_width // LANES, tm, LANES), F32),
                        pltpu.VMEM((2, lre.shape[1]), F32)],
        compiler_params=pltpu.CompilerParams(dimension_semantics=("arbitrary",),
                                             vmem_limit_bytes=VMEM_LIMIT),
        name="mixer",
    )(x2, *consts)


def _kv_kernel(mem_ref, g_ref, wkv_ref, o_ref):
    mn = _rms(mem_ref[...], g_ref[...]).astype(BF16)
    o_ref[...] = jnp.dot(mn, wkv_ref[...], preferred_element_type=F32)


def _kv_proj(mem2, g, w_kv):
    m, d = mem2.shape
    return pl.pallas_call(
        _kv_kernel,
        grid=(1,),
        in_specs=[pl.BlockSpec((m, d), lambda i: (0, 0)), _const_spec((1, d)),
                  _const_spec(w_kv.shape)],
        out_specs=pl.BlockSpec((m, w_kv.shape[1]), lambda i: (0, 0)),
        out_shape=jax.ShapeDtypeStruct((m, w_kv.shape[1]), F32),
        compiler_params=pltpu.CompilerParams(vmem_limit_bytes=VMEM_LIMIT),
        name="kv_proj",
    )(mem2, g.reshape(1, d), w_kv.astype(BF16))


def _route(logits):
    n_g, n_eg = N_EXPERT_GROUPS, EXPERTS_PER_GROUP
    tm = logits.shape[0]
    lt = logits.T
    neg = jnp.float32(-jnp.inf)
    big = jnp.int32(1 << 20)

    def first_max(v):
        row = lax.broadcasted_iota(jnp.int32, v.shape, 0)
        m = jnp.max(v, axis=0, keepdims=True)
        return m, jnp.min(jnp.where(v == m, row, big), axis=0, keepdims=True)

    gl = lt[0:n_g]
    gmax, g_idx = first_max(gl)
    g_gate = 1.0 / jnp.sum(jnp.exp(gl - gmax), axis=0, keepdims=True)
    el = lt[n_g:n_g + n_eg]
    for g in range(1, n_g):
        el = jnp.where(g_idx == g, lt[n_g + g * n_eg:n_g + (g + 1) * n_eg], el)
    row = lax.broadcasted_iota(jnp.int32, el.shape, 0)
    t1, i1 = first_max(el)
    t2, i2 = first_max(jnp.where(row == i1, neg, el))
    e2 = jnp.exp(t2 - t1)
    w1 = 1.0 / (1.0 + e2)
    w2 = e2 / (1.0 + e2)
    comb = jnp.where(row == i1, w1, jnp.where(row == i2, w2, 0.0)) * g_gate
    info = jnp.concatenate([comb, g_idx.astype(F32), jnp.zeros((LANES - n_eg - 1, tm), F32)], axis=0)
    return info.T


def _dot_bf16x3(x, w_hi, w_lo):
    x_hi = x.astype(BF16)
    x_lo = (x - x_hi.astype(F32)).astype(BF16)
    return (jnp.dot(x_hi, w_hi, preferred_element_type=F32)
            + jnp.dot(x_hi, w_lo, preferred_element_type=F32)
            + jnp.dot(x_lo, w_hi, preferred_element_type=F32))


def _xattn_kernel(h_ref, g_ref, wq_ref, kt_ref, v_ref, wo_ref, gffn_ref, wrh_ref, wrl_ref, br_ref,
                  o_ref, *, scale):
    h = h_ref[...]
    d = h.shape[1]
    hn = _rms(h, g_ref[...]).astype(BF16)
    q = jnp.dot(hn, wq_ref[...], preferred_element_type=F32)
    n_heads, hd, _ = kt_ref.shape[1:]
    outs = []
    for hh in range(n_heads):
        qh = q[:, hh * hd:(hh + 1) * hd].astype(BF16)
        s = jnp.dot(qh, kt_ref[0, hh], preferred_element_type=F32) * scale
        s = s - jnp.max(s, axis=-1, keepdims=True)
        p = jnp.exp(s)
        p = p / jnp.sum(p, axis=-1, keepdims=True)
        outs.append(jnp.dot(p.astype(BF16), v_ref[0, hh], preferred_element_type=F32))
    o = jnp.concatenate(outs, axis=1).astype(BF16)
    h_out = h + jnp.dot(o, wo_ref[...], preferred_element_type=F32)
    zn = _rms(h_out, gffn_ref[...])
    logits = _dot_bf16x3(zn, wrh_ref[...], wrl_ref[...]) + br_ref[...]
    o_ref[:, :d] = h_out
    o_ref[:, d:] = _route(logits)


def _xattn(h2, seq, g, w_q, kt, v, w_o, g_ffn, w_r, b_r, tm):
    t, d = h2.shape
    tiles_per_seq = seq // tm
    hd = d // X_HEADS
    kern = functools.partial(_xattn_kernel, scale=hd ** -0.5)
    w_r_hi = w_r.astype(BF16)
    w_r_lo = (w_r - w_r_hi.astype(F32)).astype(BF16)
    return pl.pallas_call(
        kern,
        grid=(t // tm,),
        in_specs=[pl.BlockSpec((tm, d), lambda i: (i, 0)), _const_spec((1, d)),
                  _const_spec(w_q.shape),
                  pl.BlockSpec((1,) + kt.shape[1:], lambda i: (i // tiles_per_seq, 0, 0, 0)),
                  pl.BlockSpec((1,) + v.shape[1:], lambda i: (i // tiles_per_seq, 0, 0, 0)),
                  _const_spec(w_o.shape), _const_spec((1, d)), _const_spec(w_r.shape),
                  _const_spec(w_r.shape), _const_spec(b_r.shape)],
        out_specs=pl.BlockSpec((tm, d + LANES), lambda i: (i, 0)),
        out_shape=jax.ShapeDtypeStruct((t, d + LANES), F32),
        compiler_params=pltpu.CompilerParams(dimension_semantics=("arbitrary",),
                                             vmem_limit_bytes=VMEM_LIMIT),
        name="xattn",
    )(h2, g.reshape(1, d), w_q.astype(BF16), kt, v, w_o.astype(BF16), g_ffn.reshape(1, d), w_r_hi,
      w_r_lo, b_r)


def _dispatch_tables(g_idx, tm, n_groups):
    t = g_idx.shape[0]
    n_tiles = t // tm + n_groups
    i32 = jnp.int32
    onehot = (g_idx[:, None] == jnp.arange(n_groups, dtype=i32)[None, :]).astype(i32)
    csum = jnp.cumsum(onehot, axis=0)
    counts = csum[-1]
    rank = jnp.sum(csum * onehot, axis=1) - 1
    tiles_g = (counts + tm - 1) // tm
    tile_end = jnp.cumsum(tiles_g)
    tile_start = tile_end - tiles_g
    slot = jnp.sum(onehot * tile_start[None, :], axis=1) * tm + rank
    tok_of_slot = jnp.zeros((n_tiles * tm,), i32).at[slot].set(
        jnp.arange(t, dtype=i32), unique_indices=True, mode="promise_in_bounds")
    tile_ids = jnp.arange(n_tiles, dtype=i32)
    tile_group = jnp.minimum(jnp.sum((tile_ids[:, None] >= tile_end[None, :]).astype(i32), axis=1),
                             n_groups - 1)
    tile_valid = jnp.clip(counts[tile_group] - (tile_ids - tile_start[tile_group]) * tm, 0, tm)
    tile_valid = jnp.where(tile_ids < tile_end[-1], tile_valid, 0)
    return tok_of_slot, tile_group.astype(i32), tile_valid.astype(i32)


def _moe_kernel(tok_ref, tg_ref, tv_ref, hx_hbm, g_ref, w1_ref, w3_ref, w2_ref, gf_ref, out_hbm,
                xbuf, obuf, gsem, ssem):
    i = pl.program_id(0)
    n = pl.num_programs(0)
    n_blk, rows = xbuf.shape[1:3]
    tm = n_blk * rows
    d = obuf.shape[3]
    cur = i % 2

    def row_in(tok, buf, blk, k):
        return pltpu.make_async_copy(hx_hbm.at[pl.ds(tok, 1), :], xbuf.at[buf, blk, pl.ds(k, 1), :],
                                     gsem.at[buf])

    def row_out(tok, buf, blk, k):
        return pltpu.make_async_copy(obuf.at[buf, blk, pl.ds(k, 1), :], out_hbm.at[pl.ds(tok, 1), :],
                                     ssem.at[buf])

    def gather_start(tile, buf):
        def body(blk, c):
            for k in range(rows):
                row_in(tok_ref[tile * tm + blk * rows + k], buf, blk, k).start()
            return c
        lax.fori_loop(0, n_blk, body, 0)

    def gather_wait(buf):
        pltpu.make_async_copy(xbuf.at[buf], xbuf.at[buf], gsem.at[buf]).wait()

    def scatter_start(tile, buf):
        nv = tv_ref[tile]
        full = nv // rows

        def body(blk, c):
            for k in range(rows):
                row_out(tok_ref[tile * tm + blk * rows + k], buf, blk, k).start()
            return c
        lax.fori_loop(0, full, body, 0)
        for k in range(rows - 1):

            @pl.when(full * rows + k < nv)
            def _():
                row_out(tok_ref[tile * tm + full * rows + k], buf, full, k).start()

    def scatter_wait(tile, buf):
        nv = tv_ref[tile]
        full = nv // rows

        @pl.when(full > 0)
        def _():
            pltpu.make_async_copy(obuf.at[buf, pl.ds(0, full)], obuf.at[buf, pl.ds(0, full)],
                                  ssem.at[buf]).wait()
        for k in range(rows - 1):

            @pl.when(full * rows + k < nv)
            def _():
                row_out(0, buf, 0, 0).wait()

    @pl.when(i == 0)
    def _():
        gather_start(0, 0)

    gather_wait(cur)

    @pl.when(i + 1 < n)
    def _():
        gather_start(i + 1, 1 - cur)

    @pl.when(i >= 2)
    def _():
        scatter_wait(i - 2, cur)

    @pl.when(tv_ref[i] > 0)
    def _():
        xt = xbuf[cur].reshape(tm, xbuf.shape[3])
        h = xt[:, :d]
        zn = _rms(h, g_ref[...]).astype(BF16)
        acc = h
        for e in range(w1_ref.shape[0]):
            a = jnp.dot(zn, w1_ref[e], preferred_element_type=F32)
            b = jnp.dot(zn, w3_ref[e], preferred_element_type=F32)
            hid = (a * jax.nn.sigmoid(a) * b).astype(BF16)
            acc = acc + xt[:, d + e:d + e + 1] * jnp.dot(hid, w2_ref[e], preferred_element_type=F32)
        obuf[cur] = _rms(acc, gf_ref[...]).reshape(n_blk, rows, d)

    scatter_start(i, cur)

    @pl.when(i == n - 1)
    def _():
        @pl.when(i >= 1)
        def _():
            scatter_wait(i - 1, 1 - cur)
        scatter_wait(i, cur)


def _moe(hx, g, w1, w3, w2, g_final, tm):
    t, dx = hx.shape
    d = dx - LANES
    n_eg = EXPERTS_PER_GROUP
    de = w1.shape[2]
    g_idx = hx[:, d + n_eg].astype(jnp.int32)
    tok_of_slot, tile_group, tile_valid = _dispatch_tables(g_idx, tm, N_EXPERT_GROUPS)
    n_tiles = tile_group.shape[0]
    grid_spec = pltpu.PrefetchScalarGridSpec(
        num_scalar_prefetch=3,
        grid=(n_tiles,),
        in_specs=[pl.BlockSpec(memory_space=pl.ANY),
                  pl.BlockSpec((1, d), lambda i, tok, tg, tv: (0, 0)),
                  pl.BlockSpec((n_eg, d, de), lambda i, tok, tg, tv: (tg[i], 0, 0)),
                  pl.BlockSpec((n_eg, d, de), lambda i, tok, tg, tv: (tg[i], 0, 0)),
                  pl.BlockSpec((n_eg, de, d), lambda i, tok, tg, tv: (tg[i], 0, 0)),
                  pl.BlockSpec((1, d), lambda i, tok, tg, tv: (0, 0))],
        out_specs=pl.BlockSpec(memory_space=pl.ANY),
        scratch_shapes=[pltpu.VMEM((2, tm // SUBLANES, SUBLANES, dx), F32),
                        pltpu.VMEM((2, tm // SUBLANES, SUBLANES, d), F32),
                        pltpu.SemaphoreType.DMA((2,)), pltpu.SemaphoreType.DMA((2,))])
    return pl.pallas_call(
        _moe_kernel,
        grid_spec=grid_spec,
        out_shape=jax.ShapeDtypeStruct((t, d), F32),
        compiler_params=pltpu.CompilerParams(dimension_semantics=("arbitrary",),
                                             vmem_limit_bytes=VMEM_LIMIT),
        name="moe",
    )(tok_of_slot, tile_group, tile_valid, hx, g.reshape(1, d), w1.astype(BF16), w3.astype(BF16),
      w2.astype(BF16), g_final.reshape(1, d))


def _layer(h2, mem2, batch, seq, p, l, g_final, tm):
    d = h2.shape[1]
    n_scan = int(math.log2(tm // SSM_CHUNK))
    tables = _ssm_tables(p["ssm_a_re"][l], p["ssm_a_im"][l], p["ssm_log_dt"][l], p["ssm_b_re"][l],
                         p["ssm_b_im"][l], p["ssm_c_re"][l], p["ssm_c_im"][l], p["ssm_d"][l],
                         SSM_CHUNK, n_scan)
    h2 = _mixer(h2, seq, p["norm_mix"][l], p["w_in"][l], p["pool_w"][l], p["pool_scale"][l], tables,
                p["glu_w"][l], p["glu_b"][l], p["br_pool"][l], p["br_ssm"][l], p["w_out"][l], tm)

    m_len = mem2.shape[0] // batch
    hd = d // X_HEADS
    kv = _kv_proj(mem2, p["norm_mem"][l], p["x_wkv"][l])
    kt = kv[:, :d].reshape(batch, m_len, X_HEADS, hd).transpose(0, 2, 3, 1).astype(BF16)
    v = kv[:, d:].reshape(batch, m_len, X_HEADS, hd).transpose(0, 2, 1, 3).astype(BF16)

    n_g, n_eg = N_EXPERT_GROUPS, EXPERTS_PER_GROUP
    w_r = jnp.concatenate([p["router_g_w"][l], p["router_e_w"][l].reshape(d, n_g * n_eg)], axis=1)
    w_r = jnp.pad(w_r, ((0, 0), (0, LANES - w_r.shape[1])))
    b_r = jnp.concatenate([p["router_g_b"][l], p["router_e_b"][l].reshape(n_g * n_eg)])
    b_r = jnp.pad(b_r, (0, LANES - b_r.shape[0])).reshape(1, LANES)
    hx = _xattn(h2, seq, p["norm_x"][l], p["x_wq"][l], kt, v, p["x_wo"][l], p["norm_ffn"][l],
                w_r, b_r, tm)
    return _moe(hx, p["norm_ffn"][l], p["exp_w1"][l], p["exp_w3"][l], p["exp_w2"][l], g_final, tm)


def kernel(x, mem, norm_mix, w_in, pool_w, pool_scale, ssm_a_re, ssm_a_im, ssm_log_dt, ssm_b_re,
           ssm_b_im, ssm_c_re, ssm_c_im, ssm_d, glu_w, glu_b, br_pool, br_ssm, w_out, norm_x,
           norm_mem, x_wq, x_wkv, x_wo, norm_ffn, router_g_w, router_g_b, router_e_w, router_e_b,
           exp_w1, exp_w3, exp_w2, norm_final):
    p = dict(norm_mix=norm_mix, w_in=w_in, pool_w=pool_w, pool_scale=pool_scale, ssm_a_re=ssm_a_re,
             ssm_a_im=ssm_a_im, ssm_log_dt=ssm_log_dt, ssm_b_re=ssm_b_re, ssm_b_im=ssm_b_im,
             ssm_c_re=ssm_c_re, ssm_c_im=ssm_c_im, ssm_d=ssm_d, glu_w=glu_w, glu_b=glu_b,
             br_pool=br_pool, br_ssm=br_ssm, w_out=w_out, norm_x=norm_x, norm_mem=norm_mem,
             x_wq=x_wq, x_wkv=x_wkv, x_wo=x_wo, norm_ffn=norm_ffn, router_g_w=router_g_w,
             router_g_b=router_g_b, router_e_w=router_e_w, router_e_b=router_e_b, exp_w1=exp_w1,
             exp_w3=exp_w3, exp_w2=exp_w2)
    batch, seq, d = x.shape
    assert norm_mix.shape[0] == 1, "final norm is fused into the (single) layer's last stage"
    tm = min(512, seq)
    h2 = x.reshape(batch * seq, d)
    mem2 = mem.reshape(-1, d)
    out = _layer(h2, mem2, batch, seq, p, 0, norm_final, tm)
    return out.reshape(batch, seq, d)
```

```python
import functools
import math

import jax
import jax.numpy as jnp
from jax import lax
from jax.experimental import pallas as pl
from jax.experimental.pallas import tpu as pltpu

F32 = jnp.float32
BF16 = jnp.bfloat16

RMS_EPS = 1e-6
POOL_WINDOWS = (2, 4, 8, 16)
POOL_HALO = 16
SSM_GROUP_DIM = 16
SSM_STATE = 64
LANES = 128
SUBLANES = 8
OCT = LANES // SSM_GROUP_DIM
SSM_CHUNK = 4
X_HEADS = 4
N_EXPERT_GROUPS = 4
EXPERTS_PER_GROUP = 4
VMEM_LIMIT = 56 * 1024 * 1024


def _rms(x, g):
    return x * lax.rsqrt(jnp.mean(x * x, axis=-1, keepdims=True) + RMS_EPS) * g


def _const_spec(shape):
    nd = len(shape)
    return pl.BlockSpec(shape, lambda *_: (0,) * nd, pipeline_mode=pl.Buffered(1))


def _ssm_tables(a_re, a_im, log_dt, b_re, b_im, c_re, c_im, d_skip, chunk, n_scan, n_pow):
    hi = lax.Precision.HIGHEST
    G, N = a_re.shape
    C = SSM_GROUP_DIM
    Q = G // OCT
    dt = jnp.exp(log_dt)[:, None]
    mag = jnp.exp(a_re * dt)
    lb_re, lb_im = mag * jnp.cos(a_im * dt), mag * jnp.sin(a_im * dt)
    den = a_re * a_re + a_im * a_im
    nr, ni = lb_re - 1.0, lb_im
    f_re = ((nr * a_re + ni * a_im) / den)[..., None]
    f_im = ((ni * a_re - nr * a_im) / den)[..., None]
    bb_re = f_re * b_re - f_im * b_im
    bb_im = f_re * b_im + f_im * b_re

    pw_re, pw_im = [jnp.ones_like(lb_re)], [jnp.zeros_like(lb_re)]
    for _ in range(chunk):
        r, i = pw_re[-1], pw_im[-1]
        pw_re.append(r * lb_re - i * lb_im)
        pw_im.append(r * lb_im + i * lb_re)
    pw_re, pw_im = jnp.stack(pw_re), jnp.stack(pw_im)

    lbb_re = pw_re[:chunk, :, :, None] * bb_re - pw_im[:chunk, :, :, None] * bb_im
    lbb_im = pw_re[:chunk, :, :, None] * bb_im + pw_im[:chunk, :, :, None] * bb_re
    kern = (jnp.einsum('gon,dgnc->dgoc', c_re, lbb_re, precision=hi)
            - jnp.einsum('gon,dgnc->dgoc', c_im, lbb_im, precision=hi))
    kern = kern.at[0].add(d_skip.reshape(G, C)[:, :, None] * jnp.eye(C, dtype=F32))
    eye_g = jnp.eye(OCT, dtype=F32)

    kq = kern.reshape(chunk, Q, OCT, C, C)
    blocks = []
    for j in range(chunk):
        row = []
        for jp in range(chunk):
            if jp >= j:
                blk = jnp.einsum('qgoc,gh->qgcho', kq[jp - j], eye_g)
            else:
                blk = jnp.zeros((Q, OCT, C, OCT, C), F32)
            row.append(blk)
        blocks.append(jnp.stack(row, axis=3))
    m_intra = jnp.stack(blocks, axis=1).reshape(Q, chunk * LANES, chunk * LANES)

    def state_half(lbb):
        rev = lbb[::-1].reshape(chunk, Q, OCT, N, C)
        w = jnp.einsum('jqgnc,gh->qjgchn', rev, eye_g)
        return w.reshape(Q, chunk * LANES, OCT * N)
    w_state = jnp.concatenate([state_half(lbb_re), state_half(lbb_im)], axis=-1)

    p_re = c_re[None] * pw_re[1:, :, None, :] - c_im[None] * pw_im[1:, :, None, :]
    p_im = c_re[None] * pw_im[1:, :, None, :] + c_im[None] * pw_re[1:, :, None, :]
    def out_half(p):
        pq = p.reshape(chunk, Q, OCT, C, N)
        w = jnp.einsum('jqgon,gh->qgnjho', pq, eye_g)
        return w.reshape(Q, OCT * N, chunk * LANES)
    w_out = jnp.concatenate([out_half(p_re), out_half(-p_im)], axis=1)

    lr, li = [pw_re[chunk]], [pw_im[chunk]]
    for _ in range(n_scan - 1):
        r, i = lr[-1], li[-1]
        lr.append(r * r - i * i)
        li.append(2.0 * r * i)
    qr, qi = [jnp.ones_like(lb_re)], [jnp.zeros_like(lb_re)]
    for _ in range(n_pow - 1):
        r, i = qr[-1], qi[-1]
        qr.append(r * pw_re[chunk] - i * pw_im[chunk])
        qi.append(r * pw_im[chunk] + i * pw_re[chunk])
    lam_re = jnp.stack(lr + qr).reshape(n_scan + n_pow, G * N)
    lam_im = jnp.stack(li + qi).reshape(n_scan + n_pow, G * N)
    return m_intra.astype(BF16), w_state.astype(BF16), w_out.astype(BF16), lam_re, lam_im


def _mixer_kernel(x_ref, nm_ref, win_ref, poolw_ref, pscale_ref, mi_ref, ws_ref, wh_ref,
                  lre_ref, lim_ref, gluw_ref, glub_ref, brp_ref, brs_ref, wout_ref,
                  o_ref, zext_ref, v_ref, y_ref, carry_ref, *, tiles_per_seq, chunk):
    tm, d_model = x_ref.shape
    pool_w = zext_ref.shape[1]
    n_oct = v_ref.shape[0]
    ssm_w = n_oct * LANES
    n_chunks = tm // chunk
    n_str = SUBLANES
    per = n_chunks // n_str
    span = per * chunk
    n_scan = lre_ref.shape[0] - per
    seq_tile = pl.program_id(0) % tiles_per_seq

    @pl.when(seq_tile == 0)
    def _():
        zext_ref[0:POOL_HALO, :] = jnp.zeros((POOL_HALO, pool_w), F32)
        carry_ref[...] = jnp.zeros_like(carry_ref)

    x = x_ref[...]
    u = _rms(x, nm_ref[...]).astype(BF16)
    proj = jnp.dot(u, win_ref[...], preferred_element_type=F32)
    z = proj[:, :pool_w]
    zext_ref[POOL_HALO:, :] = z
    for q in range(n_oct):
        v_ref[q] = proj[:, pool_w + q * LANES:pool_w + (q + 1) * LANES]
    g_pool = jax.nn.sigmoid(proj[:, pool_w + ssm_w:pool_w + ssm_w + d_model])
    g_ssm = jax.nn.sigmoid(proj[:, pool_w + ssm_w + d_model:])

    pos = (seq_tile * tm + lax.broadcasted_iota(jnp.int32, (tm, 1), 0) + 1).astype(F32)
    gdim = pool_w // len(POOL_WINDOWS)
    ypool = []
    for gi, w in enumerate(POOL_WINDOWS):
        cols = slice(gi * gdim, (gi + 1) * gdim)
        acc = z[:, cols]
        for dlt in range(1, w):
            acc = acc + zext_ref[POOL_HALO - dlt:POOL_HALO - dlt + tm, cols]
        diff = acc / jnp.minimum(pos, float(w)) - z[:, cols]
        ypool.append(jnp.dot(diff.astype(BF16), poolw_ref[gi], preferred_element_type=F32))
    ypool = jnp.concatenate(ypool, axis=1) * pscale_ref[...]
    zext_ref[0:POOL_HALO, :] = zext_ref[tm:tm + POOL_HALO, :]
    p_br = jnp.dot(ypool.astype(BF16), brp_ref[...], preferred_element_type=F32)

    srow = lax.broadcasted_iota(jnp.int32, (n_str, 1), 0)
    half = OCT * SSM_STATE
    log_per = per.bit_length() - 1
    for q in range(n_oct):
        st = slice(q * half, (q + 1) * half)

        def lam(k):
            return lre_ref[k:k + 1, st], lim_ref[k:k + 1, st]

        vq = jnp.concatenate(
            [jnp.concatenate([v_ref[q, pl.ds(b * chunk + j, n_str, stride=span), :]
                              for j in range(chunk)], axis=1) for b in range(per)],
            axis=0).astype(BF16)
        s = jnp.dot(vq, ws_ref[q], preferred_element_type=F32)
        a_re, a_im = lam(0)
        loc_re, loc_im = [s[0:n_str, :half]], [s[0:n_str, half:]]
        for b in range(1, per):
            p_re, p_im = loc_re[-1], loc_im[-1]
            loc_re.append(s[b * n_str:(b + 1) * n_str, :half] + a_re * p_re - a_im * p_im)
            loc_im.append(s[b * n_str:(b + 1) * n_str, half:] + a_re * p_im + a_im * p_re)
        c_re, c_im = carry_ref[0:1, st], carry_ref[1:2, st]
        e_re, e_im = loc_re[-1], loc_im[-1]
        l_re, l_im = lam(log_per)
        e_re = e_re + jnp.where(srow == 0, l_re * c_re - l_im * c_im, 0.0)
        e_im = e_im + jnp.where(srow == 0, l_re * c_im + l_im * c_re, 0.0)
        for k in range(n_str.bit_length() - 1):
            sh = 1 << k
            r_re = jnp.where(srow >= sh, pltpu.roll(e_re, sh, axis=0), 0.0)
            r_im = jnp.where(srow >= sh, pltpu.roll(e_im, sh, axis=0), 0.0)
            l_re, l_im = lam(log_per + k)
            e_re, e_im = e_re + l_re * r_re - l_im * r_im, e_im + l_re * r_im + l_im * r_re
        carry_ref[0:1, st] = e_re[n_str - 1:n_str, :]
        carry_ref[1:2, st] = e_im[n_str - 1:n_str, :]
        i_re = jnp.where(srow == 0, c_re, pltpu.roll(e_re, 1, axis=0))
        i_im = jnp.where(srow == 0, c_im, pltpu.roll(e_im, 1, axis=0))
        h0_re, h0_im = [i_re], [i_im]
        for b in range(1, per):
            l_re, l_im = lam(n_scan + b)
            h0_re.append(loc_re[b - 1] + l_re * i_re - l_im * i_im)
            h0_im.append(loc_im[b - 1] + l_re * i_im + l_im * i_re)
        h0_re = jnp.concatenate(h0_re, axis=0).astype(BF16)
        h0_im = jnp.concatenate(h0_im, axis=0).astype(BF16)
        yq = (jnp.dot(vq, mi_ref[q], preferred_element_type=F32)
              + jnp.dot(h0_re, wh_ref[q, :half, :], preferred_element_type=F32)
              + jnp.dot(h0_im, wh_ref[q, half:, :], preferred_element_type=F32))
        for b in range(per):
            for j in range(chunk):
                y_ref[q, pl.ds(b * chunk + j, n_str, stride=span), :] = (
                    yq[b * n_str:(b + 1) * n_str, j * LANES:(j + 1) * LANES])

    y = jnp.concatenate([y_ref[q] for q in range(n_oct)], axis=1)
    ys = jax.nn.gelu(y, approximate=True)
    glu = jnp.dot(ys.astype(BF16), gluw_ref[...], preferred_element_type=F32) + glub_ref[...]
    ys = ys * jax.nn.sigmoid(glu)
    s_br = jnp.dot(ys.astype(BF16), brs_ref[...], preferred_element_type=F32)

    merged = g_pool * p_br + g_ssm * s_br
    o_ref[...] = x + jnp.dot(merged.astype(BF16), wout_ref[...], preferred_element_type=F32)


def _mixer(x2, seq, nm, w_in, pool_w, pool_scale, tables, glu_w, glu_b, br_pool, br_ssm, w_out, tm):
    t, d = x2.shape
    mi, ws, wh, lre, lim = tables
    pool_width = pool_w.shape[0] * pool_w.shape[1]
    ssm_width = glu_w.shape[0]
    consts = [nm.reshape(1, d), w_in.astype(BF16), pool_w.astype(BF16),
              pool_scale.reshape(1, pool_width), mi, ws, wh, lre, lim,
              glu_w.astype(BF16), glu_b.reshape(1, ssm_width), br_pool.astype(BF16),
              br_ssm.astype(BF16), w_out.astype(BF16)]
    kern = functools.partial(_mixer_kernel, tiles_per_seq=seq // tm, chunk=SSM_CHUNK)
    return pl.pallas_call(
        kern,
        grid=(t // tm,),
        in_specs=[pl.BlockSpec((tm, d), lambda i: (i, 0))] + [_const_spec(c.shape) for c in consts],
        out_specs=pl.BlockSpec((tm, d), lambda i: (i, 0)),
        out_shape=jax.ShapeDtypeStruct((t, d), F32),
        scratch_shapes=[pltpu.VMEM((POOL_HALO + tm, pool_width), F32),
                        pltpu.VMEM((ssm_width // LANES, tm, LANES), F32),
                        pltpu.VMEM((ssm_width // LANES, tm, LANES), F32),
                        pltpu.VMEM((2, lre.shape[1]), F32)],
        compiler_params=pltpu.CompilerParams(dimension_semantics=("arbitrary",),
                                             vmem_limit_bytes=VMEM_LIMIT),
        name="mixer",
    )(x2, *consts)


def _kv_kernel(mem_ref, g_ref, wkv_ref, o_ref):
    mn = _rms(mem_ref[...], g_ref[...]).astype(BF16)
    o_ref[...] = jnp.dot(mn, wkv_ref[...], preferred_element_type=F32)


def _kv_proj(mem2, g, w_kv):
    m, d = mem2.shape
    return pl.pallas_call(
        _kv_kernel,
        grid=(1,),
        in_specs=[pl.BlockSpec((m, d), lambda i: (0, 0)), _const_spec((1, d)),
                  _const_spec(w_kv.shape)],
        out_specs=pl.BlockSpec((m, w_kv.shape[1]), lambda i: (0, 0)),
        out_shape=jax.ShapeDtypeStruct((m, w_kv.shape[1]), F32),
        compiler_params=pltpu.CompilerParams(vmem_limit_bytes=VMEM_LIMIT),
        name="kv_proj",
    )(mem2, g.reshape(1, d), w_kv.astype(BF16))


def _route(logits):
    n_g, n_eg = N_EXPERT_GROUPS, EXPERTS_PER_GROUP
    tm = logits.shape[0]
    lt = logits.T
    neg = jnp.float32(-jnp.inf)
    big = jnp.int32(1 << 20)

    def first_max(v):
        row = lax.broadcasted_iota(jnp.int32, v.shape, 0)
        m = jnp.max(v, axis=0, keepdims=True)
        return m, jnp.min(jnp.where(v == m, row, big), axis=0, keepdims=True)

    gl = lt[0:n_g]
    gmax, g_idx = first_max(gl)
    g_gate = 1.0 / jnp.sum(jnp.exp(gl - gmax), axis=0, keepdims=True)
    el = lt[n_g:n_g + n_eg]
    for g in range(1, n_g):
        el = jnp.where(g_idx == g, lt[n_g + g * n_eg:n_g + (g + 1) * n_eg], el)
    row = lax.broadcasted_iota(jnp.int32, el.shape, 0)
    t1, i1 = first_max(el)
    t2, i2 = first_max(jnp.where(row == i1, neg, el))
    e2 = jnp.exp(t2 - t1)
    w1 = 1.0 / (1.0 + e2)
    w2 = e2 / (1.0 + e2)
    comb = jnp.where(row == i1, w1, jnp.where(row == i2, w2, 0.0)) * g_gate
    info = jnp.concatenate([comb, g_idx.astype(F32), jnp.zeros((LANES - n_eg - 1, tm), F32)], axis=0)
    return info.T


def _dot_bf16x3(x, w_hi, w_lo):
    x_hi = x.astype(BF16)
    x_lo = (x - x_hi.astype(F32)).astype(BF16)
    return (jnp.dot(x_hi, w_hi, preferred_element_type=F32)
            + jnp.dot(x_hi, w_lo, preferred_element_type=F32)
            + jnp.dot(x_lo, w_hi, preferred_element_type=F32))


def _xattn_kernel(h_ref, g_ref, wq_ref, kt_ref, v_ref, wo_ref, gffn_ref, wrh_ref, wrl_ref, br_ref,
                  o_ref, *, scale):
    h = h_ref[...]
    d = h.shape[1]
    hn = _rms(h, g_ref[...]).astype(BF16)
    q = jnp.dot(hn, wq_ref[...], preferred_element_type=F32)
    n_heads, hd, _ = kt_ref.shape[1:]
    outs = []
    for hh in range(n_heads):
        qh = q[:, hh * hd:(hh + 1) * hd].astype(BF16)
        s = jnp.dot(qh, kt_ref[0, hh], preferred_element_type=F32) * scale
        s = s - jnp.max(s, axis=-1, keepdims=True)
        p = jnp.exp(s)
        p = p / jnp.sum(p, axis=-1, keepdims=True)
        outs.append(jnp.dot(p.astype(BF16), v_ref[0, hh], preferred_element_type=F32))
    o = jnp.concatenate(outs, axis=1).astype(BF16)
    h_out = h + jnp.dot(o, wo_ref[...], preferred_element_type=F32)
    zn = _rms(h_out, gffn_ref[...])
    logits = _dot_bf16x3(zn, wrh_ref[...], wrl_ref[...]) + br_ref[...]
    o_ref[:, :d] = h_out
    o_ref[:, d:] = _route(logits)


def _xattn(h2, seq, g, w_q, kt, v, w_o, g_ffn, w_r, b_r, tm):
    t, d = h2.shape
    tiles_per_seq = seq // tm
    hd = d // X_HEADS
    kern = functools.partial(_xattn_kernel, scale=hd ** -0.5)
    w_r_hi = w_r.astype(BF16)
    w_r_lo = (w_r - w_r_hi.astype(F32)).astype(BF16)
    return pl.pallas_call(
        kern,
        grid=(t // tm,),
        in_specs=[pl.BlockSpec((tm, d), lambda i: (i, 0)), _const_spec((1, d)),
                  _const_spec(w_q.shape),
                  pl.BlockSpec((1,) + kt.shape[1:], lambda i: (i // tiles_per_seq, 0, 0, 0)),
                  pl.BlockSpec((1,) + v.shape[1:], lambda i: (i // tiles_per_seq, 0, 0, 0)),
                  _const_spec(w_o.shape), _const_spec((1, d)), _const_spec(w_r.shape),
                  _const_spec(w_r.shape), _const_spec(b_r.shape)],
        out_specs=pl.BlockSpec((tm, d + LANES), lambda i: (i, 0)),
        out_shape=jax.ShapeDtypeStruct((t, d + LANES), F32),
        compiler_params=pltpu.CompilerParams(dimension_semantics=("arbitrary",),
                                             vmem_limit_bytes=VMEM_LIMIT),
        name="xattn",
    )(h2, g.reshape(1, d), w_q.astype(BF16), kt, v, w_o.astype(BF16), g_ffn.reshape(1, d), w_r_hi,
      w_r_lo, b_r)


def _dispatch_tables(g_idx, tm, n_groups):
    t = g_idx.shape[0]
    n_tiles = t // tm + n_groups
    i32 = jnp.int32
    onehot = (g_idx[:, None] == jnp.arange(n_groups, dtype=i32)[None, :]).astype(i32)
    csum = jnp.cumsum(onehot, axis=0)
    counts = csum[-1]
    rank = jnp.sum(csum * onehot, axis=1) - 1
    tiles_g = (counts + tm - 1) // tm
    tile_end = jnp.cumsum(tiles_g)
    tile_start = tile_end - tiles_g
    slot = jnp.sum(onehot * tile_start[None, :], axis=1) * tm + rank
    tok_of_slot = jnp.zeros((n_tiles * tm,), i32).at[slot].set(
        jnp.arange(t, dtype=i32), unique_indices=True, mode="promise_in_bounds")
    tile_ids = jnp.arange(n_tiles, dtype=i32)
    tile_group = jnp.minimum(jnp.sum((tile_ids[:, None] >= tile_end[None, :]).astype(i32), axis=1),
                             n_groups - 1)
    tile_valid = jnp.clip(counts[tile_group] - (tile_ids - tile_start[tile_group]) * tm, 0, tm)
    tile_valid = jnp.where(tile_ids < tile_end[-1], tile_valid, 0)
    return tok_of_slot, tile_group.astype(i32), tile_valid.astype(i32)


def _moe_kernel(tok_ref, tg_ref, tv_ref, hx_hbm, g_ref, w1_ref, w3_ref, w2_ref, gf_ref, out_hbm,
                xbuf, obuf, xcur, gsem, ssem):
    i = pl.program_id(0)
    n = pl.num_programs(0)
    n_blk, rows = xbuf.shape[1] - 1, xbuf.shape[2]
    tm = n_blk * rows
    d = obuf.shape[3]
    cur = i % 2

    def row_in(tok, buf, blk, k):
        return pltpu.make_async_copy(hx_hbm.at[pl.ds(tok, 1), :], xbuf.at[buf, blk, pl.ds(k, 1), :],
                                     gsem.at[buf])

    def row_out(tok, buf, blk, k):
        return pltpu.make_async_copy(obuf.at[buf, blk, pl.ds(k, 1), :], out_hbm.at[pl.ds(tok, 1), :],
                                     ssem.at[buf])

    def gather_start(tile, buf):
        def body(blk, c):
            for k in range(rows):
                row_in(tok_ref[tile * tm + blk * rows + k], buf, blk, k).start()
            return c
        lax.fori_loop(0, n_blk, body, 0)

    def gather_wait(buf):
        pltpu.make_async_copy(xbuf.at[buf, pl.ds(0, n_blk)], xbuf.at[buf, pl.ds(0, n_blk)],
                              gsem.at[buf]).wait()

    def scatter_start(tile, buf):
        nv = tv_ref[tile]
        full = nv // rows

        def body(blk, c):
            for k in range(rows):
                row_out(tok_ref[tile * tm + blk * rows + k], buf, blk, k).start()
            return c
        lax.fori_loop(0, full, body, 0)
        for k in range(rows - 1):

            @pl.when(full * rows + k < nv)
            def _():
                row_out(tok_ref[tile * tm + full * rows + k], buf, full, k).start()

    def scatter_wait(tile, buf):
        nv = tv_ref[tile]
        full = nv // rows

        @pl.when(full > 0)
        def _():
            pltpu.make_async_copy(obuf.at[buf, pl.ds(0, full)], obuf.at[buf, pl.ds(0, full)],
                                  ssem.at[buf]).wait()
        for k in range(rows - 1):

            @pl.when(full * rows + k < nv)
            def _():
                row_out(0, buf, 0, 0).wait()

    @pl.when(i == 0)
    def _():
        xbuf[:, n_blk] = jnp.zeros((2, rows, xbuf.shape[3]), F32)
        gather_start(0, 0)

    gather_wait(cur)

    busy = tv_ref[i] > 0

    @pl.when(jnp.logical_and(i + 1 < n, jnp.logical_not(busy)))
    def _():
        gather_start(i + 1, 1 - cur)

    @pl.when(i >= 2)
    def _():
        scatter_wait(i - 2, cur)

    @pl.when(busy)
    def _():
        xcur[...] = xbuf[cur, 0:n_blk].reshape(tm, xbuf.shape[3])

    @pl.when(busy)
    def _():
        h = xcur[:, :d]
        zn = _rms(h, g_ref[...]).astype(BF16)
        acc = h
        n_exp = w1_ref.shape[0]
        for e in range(n_exp):
            for r in range(e * tm // n_exp, (e + 1) * tm // n_exp):
                row_in(tok_ref[(i + 1) * tm + r], 1 - cur, r // rows, r % rows).start()
            a = jnp.dot(zn, w1_ref[e], preferred_element_type=F32)
            b = jnp.dot(zn, w3_ref[e], preferred_element_type=F32)
            hid = (a * jax.nn.sigmoid(a) * b).astype(BF16)
            acc = acc + xcur[:, d + e:d + e + 1] * jnp.dot(hid, w2_ref[e], preferred_element_type=F32)
            acc = acc + xbuf[1 - cur, n_blk, 0:1, 0:d]
        obuf[cur] = _rms(acc, gf_ref[...]).reshape(n_blk, rows, d)

    scatter_start(i, cur)

    @pl.when(i == n - 1)
    def _():
        @pl.when(i >= 1)
        def _():
            scatter_wait(i - 1, 1 - cur)
        scatter_wait(i, cur)


def _moe(hx, g, w1, w3, w2, g_final, tm):
    t, dx = hx.shape
    d = dx - LANES
    n_eg = EXPERTS_PER_GROUP
    de = w1.shape[2]
    g_idx = hx[:, d + n_eg].astype(jnp.int32)
    tok_of_slot, tile_group, tile_valid = _dispatch_tables(g_idx, tm, N_EXPERT_GROUPS)
    n_tiles = tile_group.shape[0]
    grid_spec = pltpu.PrefetchScalarGridSpec(
        num_scalar_prefetch=3,
        grid=(n_tiles,),
        in_specs=[pl.BlockSpec(memory_space=pl.ANY),
                  pl.BlockSpec((1, d), lambda i, tok, tg, tv: (0, 0)),
                  pl.BlockSpec((n_eg, d, de), lambda i, tok, tg, tv: (tg[i], 0, 0)),
                  pl.BlockSpec((n_eg, d, de), lambda i, tok, tg, tv: (tg[i], 0, 0)),
                  pl.BlockSpec((n_eg, de, d), lambda i, tok, tg, tv: (tg[i], 0, 0)),
                  pl.BlockSpec((1, d), lambda i, tok, tg, tv: (0, 0))],
        out_specs=pl.BlockSpec(memory_space=pl.ANY),
        scratch_shapes=[pltpu.VMEM((2, tm // SUBLANES + 1, SUBLANES, dx), F32),
                        pltpu.VMEM((2, tm // SUBLANES, SUBLANES, d), F32),
                        pltpu.VMEM((tm, dx), F32),
                        pltpu.SemaphoreType.DMA((2,)), pltpu.SemaphoreType.DMA((2,))])
    return pl.pallas_call(
        _moe_kernel,
        grid_spec=grid_spec,
        out_shape=jax.ShapeDtypeStruct((t, d), F32),
        compiler_params=pltpu.CompilerParams(dimension_semantics=("arbitrary",),
                                             vmem_limit_bytes=VMEM_LIMIT),
        name="moe",
    )(tok_of_slot, tile_group, tile_valid, hx, g.reshape(1, d), w1.astype(BF16), w3.astype(BF16),
      w2.astype(BF16), g_final.reshape(1, d))


def _layer(h2, mem2, batch, seq, p, l, g_final, tm):
    d = h2.shape[1]
    n_chunks = tm // SSM_CHUNK
    n_scan = int(math.log2(n_chunks))
    tables = _ssm_tables(p["ssm_a_re"][l], p["ssm_a_im"][l], p["ssm_log_dt"][l], p["ssm_b_re"][l],
                         p["ssm_b_im"][l], p["ssm_c_re"][l], p["ssm_c_im"][l], p["ssm_d"][l],
                         SSM_CHUNK, n_scan, n_chunks // SUBLANES)
    h2 = _mixer(h2, seq, p["norm_mix"][l], p["w_in"][l], p["pool_w"][l], p["pool_scale"][l], tables,
                p["glu_w"][l], p["glu_b"][l], p["br_pool"][l], p["br_ssm"][l], p["w_out"][l], tm)

    m_len = mem2.shape[0] // batch
    hd = d // X_HEADS
    kv = _kv_proj(mem2, p["norm_mem"][l], p["x_wkv"][l])
    kt = kv[:, :d].reshape(batch, m_len, X_HEADS, hd).transpose(0, 2, 3, 1).astype(BF16)
    v = kv[:, d:].reshape(batch, m_len, X_HEADS, hd).transpose(0, 2, 1, 3).astype(BF16)

    n_g, n_eg = N_EXPERT_GROUPS, EXPERTS_PER_GROUP
    w_r = jnp.concatenate([p["router_g_w"][l], p["router_e_w"][l].reshape(d, n_g * n_eg)], axis=1)
    w_r = jnp.pad(w_r, ((0, 0), (0, LANES - w_r.shape[1])))
    b_r = jnp.concatenate([p["router_g_b"][l], p["router_e_b"][l].reshape(n_g * n_eg)])
    b_r = jnp.pad(b_r, (0, LANES - b_r.shape[0])).reshape(1, LANES)
    hx = _xattn(h2, seq, p["norm_x"][l], p["x_wq"][l], kt, v, p["x_wo"][l], p["norm_ffn"][l],
                w_r, b_r, tm)
    return _moe(hx, p["norm_ffn"][l], p["exp_w1"][l], p["exp_w3"][l], p["exp_w2"][l], g_final, tm)


def kernel(x, mem, norm_mix, w_in, pool_w, pool_scale, ssm_a_re, ssm_a_im, ssm_log_dt, ssm_b_re,
           ssm_b_im, ssm_c_re, ssm_c_im, ssm_d, glu_w, glu_b, br_pool, br_ssm, w_out, norm_x,
           norm_mem, x_wq, x_wkv, x_wo, norm_ffn, router_g_w, router_g_b, router_e_w, router_e_b,
           exp_w1, exp_w3, exp_w2, norm_final):
    p = dict(norm_mix=norm_mix, w_in=w_in, pool_w=pool_w, pool_scale=pool_scale, ssm_a_re=ssm_a_re,
             ssm_a_im=ssm_a_im, ssm_log_dt=ssm_log_dt, ssm_b_re=ssm_b_re, ssm_b_im=ssm_b_im,
             ssm_c_re=ssm_c_re, ssm_c_im=ssm_c_im, ssm_d=ssm_d, glu_w=glu_w, glu_b=glu_b,
             br_pool=br_pool, br_ssm=br_ssm, w_out=w_out, norm_x=norm_x, norm_mem=norm_mem,
             x_wq=x_wq, x_wkv=x_wkv, x_wo=x_wo, norm_ffn=norm_ffn, router_g_w=router_g_w,
             router_g_b=router_g_b, router_e_w=router_e_w, router_e_b=router_e_b, exp_w1=exp_w1,
             exp_w3=exp_w3, exp_w2=exp_w2)
    batch, seq, d = x.shape
    assert norm_mix.shape[0] == 1, "final norm is fused into the (single) layer's last stage"
    tm = min(512, seq)
    h2 = x.reshape(batch * seq, d)
    mem2 = mem.reshape(-1, d)
    out = _layer(h2, mem2, batch, seq, p, 0, norm_final, tm)
    return out.reshape(batch, seq, d)
```

```python
import functools
import math

import jax
import jax.numpy as jnp
from jax import lax
from jax.experimental import pallas as pl
from jax.experimental.pallas import tpu as pltpu

F32 = jnp.float32
BF16 = jnp.bfloat16

RMS_EPS = 1e-6
POOL_WINDOWS = (2, 4, 8, 16)
POOL_HALO = 16
SSM_GROUP_DIM = 16
SSM_STATE = 64
LANES = 128
SUBLANES = 8
OCT = LANES // SSM_GROUP_DIM
SSM_CHUNK = 4
X_HEADS = 4
N_EXPERT_GROUPS = 4
EXPERTS_PER_GROUP = 4
EXPERT_PAIRS = ((0, 1), (0, 2), (0, 3), (1, 2), (1, 3), (2, 3))
VMEM_LIMIT = 56 * 1024 * 1024


def _rms(x, g):
    return x * lax.rsqrt(jnp.mean(x * x, axis=-1, keepdims=True) + RMS_EPS) * g


def _const_spec(shape):
    nd = len(shape)
    return pl.BlockSpec(shape, lambda *_: (0,) * nd, pipeline_mode=pl.Buffered(1))


def _ssm_tables(a_re, a_im, log_dt, b_re, b_im, c_re, c_im, d_skip, chunk, n_scan, n_pow):
    hi = lax.Precision.HIGHEST
    G, N = a_re.shape
    C = SSM_GROUP_DIM
    Q = G // OCT
    dt = jnp.exp(log_dt)[:, None]
    mag = jnp.exp(a_re * dt)
    lb_re, lb_im = mag * jnp.cos(a_im * dt), mag * jnp.sin(a_im * dt)
    den = a_re * a_re + a_im * a_im
    nr, ni = lb_re - 1.0, lb_im
    f_re = ((nr * a_re + ni * a_im) / den)[..., None]
    f_im = ((ni * a_re - nr * a_im) / den)[..., None]
    bb_re = f_re * b_re - f_im * b_im
    bb_im = f_re * b_im + f_im * b_re

    pw_re, pw_im = [jnp.ones_like(lb_re)], [jnp.zeros_like(lb_re)]
    for _ in range(chunk):
        r, i = pw_re[-1], pw_im[-1]
        pw_re.append(r * lb_re - i * lb_im)
        pw_im.append(r * lb_im + i * lb_re)
    pw_re, pw_im = jnp.stack(pw_re), jnp.stack(pw_im)

    lbb_re = pw_re[:chunk, :, :, None] * bb_re - pw_im[:chunk, :, :, None] * bb_im
    lbb_im = pw_re[:chunk, :, :, None] * bb_im + pw_im[:chunk, :, :, None] * bb_re
    kern = (jnp.einsum('gon,dgnc->dgoc', c_re, lbb_re, precision=hi)
            - jnp.einsum('gon,dgnc->dgoc', c_im, lbb_im, precision=hi))
    kern = kern.at[0].add(d_skip.reshape(G, C)[:, :, None] * jnp.eye(C, dtype=F32))
    eye_g = jnp.eye(OCT, dtype=F32)

    kq = kern.reshape(chunk, Q, OCT, C, C)
    blocks = []
    for j in range(chunk):
        row = []
        for jp in range(chunk):
            if jp >= j:
                blk = jnp.einsum('qgoc,gh->qgcho', kq[jp - j], eye_g)
            else:
                blk = jnp.zeros((Q, OCT, C, OCT, C), F32)
            row.append(blk)
        blocks.append(jnp.stack(row, axis=3))
    m_intra = jnp.stack(blocks, axis=1).reshape(Q, chunk * LANES, chunk * LANES)

    def state_half(lbb):
        rev = lbb[::-1].reshape(chunk, Q, OCT, N, C)
        w = jnp.einsum('jqgnc,gh->qjgchn', rev, eye_g)
        return w.reshape(Q, chunk * LANES, OCT * N)
    w_state = jnp.concatenate([state_half(lbb_re), state_half(lbb_im)], axis=-1)

    p_re = c_re[None] * pw_re[1:, :, None, :] - c_im[None] * pw_im[1:, :, None, :]
    p_im = c_re[None] * pw_im[1:, :, None, :] + c_im[None] * pw_re[1:, :, None, :]
    def out_half(p):
        pq = p.reshape(chunk, Q, OCT, C, N)
        w = jnp.einsum('jqgon,gh->qgnjho', pq, eye_g)
        return w.reshape(Q, OCT * N, chunk * LANES)
    w_out = jnp.concatenate([out_half(p_re), out_half(-p_im)], axis=1)

    lr, li = [pw_re[chunk]], [pw_im[chunk]]
    for _ in range(n_scan - 1):
        r, i = lr[-1], li[-1]
        lr.append(r * r - i * i)
        li.append(2.0 * r * i)
    qr, qi = [jnp.ones_like(lb_re)], [jnp.zeros_like(lb_re)]
    for _ in range(n_pow - 1):
        r, i = qr[-1], qi[-1]
        qr.append(r * pw_re[chunk] - i * pw_im[chunk])
        qi.append(r * pw_im[chunk] + i * pw_re[chunk])
    lam_re = jnp.stack(lr + qr).reshape(n_scan + n_pow, G * N)
    lam_im = jnp.stack(li + qi).reshape(n_scan + n_pow, G * N)
    return m_intra.astype(BF16), w_state.astype(BF16), w_out.astype(BF16), lam_re, lam_im


def _mixer_kernel(x_ref, nm_ref, win_ref, poolw_ref, pscale_ref, mi_ref, ws_ref, wh_ref,
                  lre_ref, lim_ref, gluw_ref, glub_ref, brp_ref, brs_ref, wout_ref,
                  o_ref, zext_ref, v_ref, y_ref, carry_ref, *, tiles_per_seq, chunk):
    tm, d_model = x_ref.shape
    pool_w = zext_ref.shape[1]
    n_oct = v_ref.shape[0]
    ssm_w = n_oct * LANES
    n_chunks = tm // chunk
    n_str = SUBLANES
    per = n_chunks // n_str
    span = per * chunk
    n_scan = lre_ref.shape[0] - per
    seq_tile = pl.program_id(0) % tiles_per_seq

    @pl.when(seq_tile == 0)
    def _():
        zext_ref[0:POOL_HALO, :] = jnp.zeros((POOL_HALO, pool_w), F32)
        carry_ref[...] = jnp.zeros_like(carry_ref)

    x = x_ref[...]
    u = _rms(x, nm_ref[...]).astype(BF16)
    proj = jnp.dot(u, win_ref[...], preferred_element_type=F32)
    z = proj[:, :pool_w]
    zext_ref[POOL_HALO:, :] = z
    for q in range(n_oct):
        v_ref[q] = proj[:, pool_w + q * LANES:pool_w + (q + 1) * LANES]
    g_pool = jax.nn.sigmoid(proj[:, pool_w + ssm_w:pool_w + ssm_w + d_model])
    g_ssm = jax.nn.sigmoid(proj[:, pool_w + ssm_w + d_model:])

    pos = (seq_tile * tm + lax.broadcasted_iota(jnp.int32, (tm, 1), 0) + 1).astype(F32)
    gdim = pool_w // len(POOL_WINDOWS)
    ypool = []
    for gi, w in enumerate(POOL_WINDOWS):
        cols = slice(gi * gdim, (gi + 1) * gdim)
        acc = z[:, cols]
        for dlt in range(1, w):
            acc = acc + zext_ref[POOL_HALO - dlt:POOL_HALO - dlt + tm, cols]
        diff = acc / jnp.minimum(pos, float(w)) - z[:, cols]
        ypool.append(jnp.dot(diff.astype(BF16), poolw_ref[gi], preferred_element_type=F32))
    ypool = jnp.concatenate(ypool, axis=1) * pscale_ref[...]
    zext_ref[0:POOL_HALO, :] = zext_ref[tm:tm + POOL_HALO, :]
    p_br = jnp.dot(ypool.astype(BF16), brp_ref[...], preferred_element_type=F32)

    srow = lax.broadcasted_iota(jnp.int32, (n_str, 1), 0)
    half = OCT * SSM_STATE
    log_per = per.bit_length() - 1
    for q in range(n_oct):
        st = slice(q * half, (q + 1) * half)

        def lam(k):
            return lre_ref[k:k + 1, st], lim_ref[k:k + 1, st]

        vq = jnp.concatenate(
            [jnp.concatenate([v_ref[q, pl.ds(b * chunk + j, n_str, stride=span), :]
                              for j in range(chunk)], axis=1) for b in range(per)],
            axis=0).astype(BF16)
        s = jnp.dot(vq, ws_ref[q], preferred_element_type=F32)
        a_re, a_im = lam(0)
        loc_re, loc_im = [s[0:n_str, :half]], [s[0:n_str, half:]]
        for b in range(1, per):
            p_re, p_im = loc_re[-1], loc_im[-1]
            loc_re.append(s[b * n_str:(b + 1) * n_str, :half] + a_re * p_re - a_im * p_im)
            loc_im.append(s[b * n_str:(b + 1) * n_str, half:] + a_re * p_im + a_im * p_re)
        c_re, c_im = carry_ref[0:1, st], carry_ref[1:2, st]
        e_re, e_im = loc_re[-1], loc_im[-1]
        l_re, l_im = lam(log_per)
        e_re = e_re + jnp.where(srow == 0, l_re * c_re - l_im * c_im, 0.0)
        e_im = e_im + jnp.where(srow == 0, l_re * c_im + l_im * c_re, 0.0)
        for k in range(n_str.bit_length() - 1):
            sh = 1 << k
            r_re = jnp.where(srow >= sh, pltpu.roll(e_re, sh, axis=0), 0.0)
            r_im = jnp.where(srow >= sh, pltpu.roll(e_im, sh, axis=0), 0.0)
            l_re, l_im = lam(log_per + k)
            e_re, e_im = e_re + l_re * r_re - l_im * r_im, e_im + l_re * r_im + l_im * r_re
        carry_ref[0:1, st] = e_re[n_str - 1:n_str, :]
        carry_ref[1:2, st] = e_im[n_str - 1:n_str, :]
        i_re = jnp.where(srow == 0, c_re, pltpu.roll(e_re, 1, axis=0))
        i_im = jnp.where(srow == 0, c_im, pltpu.roll(e_im, 1, axis=0))
        h0_re, h0_im = [i_re], [i_im]
        for b in range(1, per):
            l_re, l_im = lam(n_scan + b)
            h0_re.append(loc_re[b - 1] + l_re * i_re - l_im * i_im)
            h0_im.append(loc_im[b - 1] + l_re * i_im + l_im * i_re)
        h0_re = jnp.concatenate(h0_re, axis=0).astype(BF16)
        h0_im = jnp.concatenate(h0_im, axis=0).astype(BF16)
        yq = (jnp.dot(vq, mi_ref[q], preferred_element_type=F32)
              + jnp.dot(h0_re, wh_ref[q, :half, :], preferred_element_type=F32)
              + jnp.dot(h0_im, wh_ref[q, half:, :], preferred_element_type=F32))
        for b in range(per):
            for j in range(chunk):
                y_ref[q, pl.ds(b * chunk + j, n_str, stride=span), :] = (
                    yq[b * n_str:(b + 1) * n_str, j * LANES:(j + 1) * LANES])

    y = jnp.concatenate([y_ref[q] for q in range(n_oct)], axis=1)
    ys = jax.nn.gelu(y, approximate=True)
    glu = jnp.dot(ys.astype(BF16), gluw_ref[...], preferred_element_type=F32) + glub_ref[...]
    ys = ys * jax.nn.sigmoid(glu)
    s_br = jnp.dot(ys.astype(BF16), brs_ref[...], preferred_element_type=F32)

    merged = g_pool * p_br + g_ssm * s_br
    o_ref[...] = x + jnp.dot(merged.astype(BF16), wout_ref[...], preferred_element_type=F32)


def _mixer(x2, seq, nm, w_in, pool_w, pool_scale, tables, glu_w, glu_b, br_pool, br_ssm, w_out, tm):
    t, d = x2.shape
    mi, ws, wh, lre, lim = tables
    pool_width = pool_w.shape[0] * pool_w.shape[1]
    ssm_width = glu_w.shape[0]
    consts = [nm.reshape(1, d), w_in.astype(BF16), pool_w.astype(BF16),
              pool_scale.reshape(1, pool_width), mi, ws, wh, lre, lim,
              glu_w.astype(BF16), glu_b.reshape(1, ssm_width), br_pool.astype(BF16),
              br_ssm.astype(BF16), w_out.astype(BF16)]
    kern = functools.partial(_mixer_kernel, tiles_per_seq=seq // tm, chunk=SSM_CHUNK)
    return pl.pallas_call(
        kern,
        grid=(t // tm,),
        in_specs=[pl.BlockSpec((tm, d), lambda i: (i, 0))] + [_const_spec(c.shape) for c in consts],
        out_specs=pl.BlockSpec((tm, d), lambda i: (i, 0)),
        out_shape=jax.ShapeDtypeStruct((t, d), F32),
        scratch_shapes=[pltpu.VMEM((POOL_HALO + tm, pool_width), F32),
                        pltpu.VMEM((ssm_width // LANES, tm, LANES), F32),
                        pltpu.VMEM((ssm_width // LANES, tm, LANES), F32),
                        pltpu.VMEM((2, lre.shape[1]), F32)],
        compiler_params=pltpu.CompilerParams(dimension_semantics=("arbitrary",),
                                             vmem_limit_bytes=VMEM_LIMIT),
        name="mixer",
    )(x2, *consts)


def _kv_kernel(mem_ref, g_ref, wkv_ref, o_ref):
    mn = _rms(mem_ref[...], g_ref[...]).astype(BF16)
    o_ref[...] = jnp.dot(mn, wkv_ref[...], preferred_element_type=F32)


def _kv_proj(mem2, g, w_kv):
    m, d = mem2.shape
    return pl.pallas_call(
        _kv_kernel,
        grid=(1,),
        in_specs=[pl.BlockSpec((m, d), lambda i: (0, 0)), _const_spec((1, d)),
                  _const_spec(w_kv.shape)],
        out_specs=pl.BlockSpec((m, w_kv.shape[1]), lambda i: (0, 0)),
        out_shape=jax.ShapeDtypeStruct((m, w_kv.shape[1]), F32),
        compiler_params=pltpu.CompilerParams(vmem_limit_bytes=VMEM_LIMIT),
        name="kv_proj",
    )(mem2, g.reshape(1, d), w_kv.astype(BF16))


def _route(logits):
    n_g, n_eg = N_EXPERT_GROUPS, EXPERTS_PER_GROUP
    tm = logits.shape[0]
    lt = logits.T
    neg = jnp.float32(-jnp.inf)
    big = jnp.int32(1 << 20)

    def first_max(v):
        row = lax.broadcasted_iota(jnp.int32, v.shape, 0)
        m = jnp.max(v, axis=0, keepdims=True)
        return m, jnp.min(jnp.where(v == m, row, big), axis=0, keepdims=True)

    gl = lt[0:n_g]
    gmax, g_idx = first_max(gl)
    g_gate = 1.0 / jnp.sum(jnp.exp(gl - gmax), axis=0, keepdims=True)
    el = lt[n_g:n_g + n_eg]
    for g in range(1, n_g):
        el = jnp.where(g_idx == g, lt[n_g + g * n_eg:n_g + (g + 1) * n_eg], el)
    row = lax.broadcasted_iota(jnp.int32, el.shape, 0)
    t1, i1 = first_max(el)
    t2, i2 = first_max(jnp.where(row == i1, neg, el))
    e2 = jnp.exp(t2 - t1)
    w1 = 1.0 / (1.0 + e2)
    w2 = e2 / (1.0 + e2)
    comb = jnp.where(row == i1, w1, jnp.where(row == i2, w2, 0.0)) * g_gate
    lo, hi = jnp.minimum(i1, i2), jnp.maximum(i1, i2)
    pair = jnp.where(lo == 0, hi - 1, jnp.where(lo == 1, hi + 1, len(EXPERT_PAIRS) - 1))
    cls = (g_idx * len(EXPERT_PAIRS) + pair).astype(F32)
    info = jnp.concatenate([comb, cls, jnp.zeros((LANES - n_eg - 1, tm), F32)], axis=0)
    return info.T


def _dot_bf16x3(x, w_hi, w_lo):
    x_hi = x.astype(BF16)
    x_lo = (x - x_hi.astype(F32)).astype(BF16)
    return (jnp.dot(x_hi, w_hi, preferred_element_type=F32)
            + jnp.dot(x_hi, w_lo, preferred_element_type=F32)
            + jnp.dot(x_lo, w_hi, preferred_element_type=F32))


def _xattn_kernel(h_ref, g_ref, wq_ref, kt_ref, v_ref, wo_ref, gffn_ref, wrh_ref, wrl_ref, br_ref,
                  o_ref, *, scale):
    h = h_ref[...]
    d = h.shape[1]
    hn = _rms(h, g_ref[...]).astype(BF16)
    q = jnp.dot(hn, wq_ref[...], preferred_element_type=F32)
    n_heads, hd, _ = kt_ref.shape[1:]
    outs = []
    for hh in range(n_heads):
        qh = q[:, hh * hd:(hh + 1) * hd].astype(BF16)
        s = jnp.dot(qh, kt_ref[0, hh], preferred_element_type=F32) * scale
        s = s - jnp.max(s, axis=-1, keepdims=True)
        p = jnp.exp(s)
        p = p / jnp.sum(p, axis=-1, keepdims=True)
        outs.append(jnp.dot(p.astype(BF16), v_ref[0, hh], preferred_element_type=F32))
    o = jnp.concatenate(outs, axis=1).astype(BF16)
    h_out = h + jnp.dot(o, wo_ref[...], preferred_element_type=F32)
    zn = _rms(h_out, gffn_ref[...])
    logits = _dot_bf16x3(zn, wrh_ref[...], wrl_ref[...]) + br_ref[...]
    o_ref[:, :d] = h_out
    o_ref[:, d:] = _route(logits)


def _xattn(h2, seq, g, w_q, kt, v, w_o, g_ffn, w_r, b_r, tm):
    t, d = h2.shape
    tiles_per_seq = seq // tm
    hd = d // X_HEADS
    kern = functools.partial(_xattn_kernel, scale=hd ** -0.5)
    w_r_hi = w_r.astype(BF16)
    w_r_lo = (w_r - w_r_hi.astype(F32)).astype(BF16)
    return pl.pallas_call(
        kern,
        grid=(t // tm,),
        in_specs=[pl.BlockSpec((tm, d), lambda i: (i, 0)), _const_spec((1, d)),
                  _const_spec(w_q.shape),
                  pl.BlockSpec((1,) + kt.shape[1:], lambda i: (i // tiles_per_seq, 0, 0, 0)),
                  pl.BlockSpec((1,) + v.shape[1:], lambda i: (i // tiles_per_seq, 0, 0, 0)),
                  _const_spec(w_o.shape), _const_spec((1, d)), _const_spec(w_r.shape),
                  _const_spec(w_r.shape), _const_spec(b_r.shape)],
        out_specs=pl.BlockSpec((tm, d + LANES), lambda i: (i, 0)),
        out_shape=jax.ShapeDtypeStruct((t, d + LANES), F32),
        compiler_params=pltpu.CompilerParams(dimension_semantics=("arbitrary",),
                                             vmem_limit_bytes=VMEM_LIMIT),
        name="xattn",
    )(h2, g.reshape(1, d), w_q.astype(BF16), kt, v, w_o.astype(BF16), g_ffn.reshape(1, d), w_r_hi,
      w_r_lo, b_r)


def _dispatch_tables(cls, tm):
    t = cls.shape[0]
    n_g, n_eg, n_pair = N_EXPERT_GROUPS, EXPERTS_PER_GROUP, len(EXPERT_PAIRS)
    n_tiles = t // tm + n_g
    i32 = jnp.int32
    _, tok_sorted = lax.sort((cls, jnp.arange(t, dtype=i32)), num_keys=1, is_stable=True)
    tok_sorted = jnp.concatenate([tok_sorted, jnp.zeros((tm,), i32)])
    counts = jnp.sum((cls[:, None] == jnp.arange(n_g * n_pair, dtype=i32)[None, :]).astype(i32), axis=0)
    c_start = jnp.cumsum(counts) - counts
    g_count = counts.reshape(n_g, n_pair).sum(axis=1)
    g_start = jnp.cumsum(g_count) - g_count
    tiles_g = (g_count + tm - 1) // tm
    tile_end = jnp.cumsum(tiles_g)
    tile_first = tile_end - tiles_g
    tile_ids = jnp.arange(n_tiles, dtype=i32)
    tile_group = jnp.minimum(jnp.sum((tile_ids[:, None] >= tile_end[None, :]).astype(i32), axis=1),
                             n_g - 1)
    k_in_group = tile_ids - tile_first[tile_group]
    tile_valid = jnp.clip(g_count[tile_group] - k_in_group * tm, 0, tm)
    tile_valid = jnp.where(tile_ids < tile_end[-1], tile_valid, 0)
    tile_start = jnp.where(tile_valid > 0, g_start[tile_group] + k_in_group * tm, 0)
    lo, hi = tile_start[:, None], (tile_start + tile_valid)[:, None]
    inter = (counts[None, :] > 0) & (c_start[None, :] < hi) & ((c_start + counts)[None, :] > lo)
    inter = inter.reshape(n_tiles, n_g, n_pair).any(axis=1)
    pair_has = jnp.array([[e in p for e in range(n_eg)] for p in EXPERT_PAIRS])
    need = (inter[:, :, None] & pair_has[None]).any(axis=1)
    tile_nexp = need.sum(axis=1)
    tile_exp = jnp.argsort(jnp.logical_not(need), axis=1, stable=True)
    return (tok_sorted, tile_start.astype(i32), tile_group.astype(i32), tile_valid.astype(i32),
            tile_nexp.astype(i32), tile_exp.reshape(-1).astype(i32))


def _moe_kernel(tok_ref, ts_ref, tg_ref, tv_ref, tn_ref, te_ref, hx_hbm, g_ref, w1_ref, w3_ref, w2_ref,
                gf_ref, out_hbm, xbuf, obuf, xcur, zn_ref, acc_ref, gsem, ssem):
    i = pl.program_id(0)
    n = pl.num_programs(0)
    n_blk, rows = xbuf.shape[1] - 1, xbuf.shape[2]
    tm = n_blk * rows
    d = obuf.shape[3]
    n_eg = w1_ref.shape[0]
    cur = i % 2

    def row_in(tok, buf, blk, k):
        return pltpu.make_async_copy(hx_hbm.at[pl.ds(tok, 1), :], xbuf.at[buf, blk, pl.ds(k, 1), :],
                                     gsem.at[buf])

    def row_out(tok, buf, blk, k):
        return pltpu.make_async_copy(obuf.at[buf, blk, pl.ds(k, 1), :], out_hbm.at[pl.ds(tok, 1), :],
                                     ssem.at[buf])

    def gather_start(tile, buf):
        base = ts_ref[tile]

        def body(blk, c):
            for k in range(rows):
                row_in(tok_ref[base + blk * rows + k], buf, blk, k).start()
            return c
        lax.fori_loop(0, n_blk, body, 0)

    def gather_wait(buf):
        pltpu.make_async_copy(xbuf.at[buf, pl.ds(0, n_blk)], xbuf.at[buf, pl.ds(0, n_blk)],
                              gsem.at[buf]).wait()

    def scatter_start(tile, buf):
        base = ts_ref[tile]
        nv = tv_ref[tile]
        full = nv // rows

        def body(blk, c):
            for k in range(rows):
                row_out(tok_ref[base + blk * rows + k], buf, blk, k).start()
            return c
        lax.fori_loop(0, full, body, 0)
        for k in range(rows - 1):

            @pl.when(full * rows + k < nv)
            def _():
                row_out(tok_ref[base + full * rows + k], buf, full, k).start()

    def scatter_wait(tile, buf):
        nv = tv_ref[tile]
        full = nv // rows

        @pl.when(full > 0)
        def _():
            pltpu.make_async_copy(obuf.at[buf, pl.ds(0, full)], obuf.at[buf, pl.ds(0, full)],
                                  ssem.at[buf]).wait()
        for k in range(rows - 1):

            @pl.when(full * rows + k < nv)
            def _():
                row_out(0, buf, 0, 0).wait()

    def expert(zn, k):
        e = te_ref[i * n_eg + k]
        a = jnp.dot(zn, w1_ref[e], preferred_element_type=F32)
        b = jnp.dot(zn, w3_ref[e], preferred_element_type=F32)
        hid = (a * jax.nn.sigmoid(a) * b).astype(BF16)
        route = xcur[:, d:]
        lane = lax.broadcasted_iota(jnp.int32, route.shape, 1)
        comb = jnp.sum(jnp.where(lane == e, route, 0.0), axis=1, keepdims=True)
        return comb * jnp.dot(hid, w2_ref[e], preferred_element_type=F32)

    @pl.when(i == 0)
    def _():
        xbuf[:, n_blk] = jnp.zeros((2, rows, xbuf.shape[3]), F32)
        gather_start(0, 0)

    gather_wait(cur)

    busy = tv_ref[i] > 0

    @pl.when(jnp.logical_and(i + 1 < n, jnp.logical_not(busy)))
    def _():
        gather_start(i + 1, 1 - cur)

    @pl.when(i >= 2)
    def _():
        scatter_wait(i - 2, cur)

    @pl.when(busy)
    def _():
        xcur[...] = xbuf[cur, 0:n_blk].reshape(tm, xbuf.shape[3])

    n_always = 2

    @pl.when(busy)
    def _():
        h = xcur[:, :d]
        zn = _rms(h, g_ref[...]).astype(BF16)
        zn_ref[...] = zn
        acc = h
        nxt = ts_ref[i + 1]
        for k in range(n_always):
            for r in range(k * tm // n_always, (k + 1) * tm // n_always):
                row_in(tok_ref[nxt + r], 1 - cur, r // rows, r % rows).start()
            acc = acc + expert(zn, k)
            acc = acc + xbuf[1 - cur, n_blk, 0:1, 0:d]
        acc_ref[...] = acc

    for k in range(n_always, n_eg):
        @pl.when(tn_ref[i] > k)
        def _():
            acc_ref[...] += expert(zn_ref[...], k)

    @pl.when(busy)
    def _():
        obuf[cur] = _rms(acc_ref[...], gf_ref[...]).reshape(n_blk, rows, d)

    scatter_start(i, cur)

    @pl.when(i == n - 1)
    def _():
        @pl.when(i >= 1)
        def _():
            scatter_wait(i - 1, 1 - cur)
        scatter_wait(i, cur)


def _moe(hx, g, w1, w3, w2, g_final, tm):
    t, dx = hx.shape
    d = dx - LANES
    n_eg = EXPERTS_PER_GROUP
    de = w1.shape[2]
    cls = hx[:, d + n_eg].astype(jnp.int32)
    tables = _dispatch_tables(cls, tm)
    n_tiles = tables[2].shape[0]

    def wmap(i, tok, ts, tg, tv, tn, te):
        return (tg[i], 0, 0)

    def cmap(i, tok, ts, tg, tv, tn, te):
        return (0, 0)

    grid_spec = pltpu.PrefetchScalarGridSpec(
        num_scalar_prefetch=len(tables),
        grid=(n_tiles,),
        in_specs=[pl.BlockSpec(memory_space=pl.ANY),
                  pl.BlockSpec((1, d), cmap),
                  pl.BlockSpec((n_eg, d, de), wmap),
                  pl.BlockSpec((n_eg, d, de), wmap),
                  pl.BlockSpec((n_eg, de, d), wmap),
                  pl.BlockSpec((1, d), cmap)],
        out_specs=pl.BlockSpec(memory_space=pl.ANY),
        scratch_shapes=[pltpu.VMEM((2, tm // SUBLANES + 1, SUBLANES, dx), F32),
                        pltpu.VMEM((2, tm // SUBLANES, SUBLANES, d), F32),
                        pltpu.VMEM((tm, dx), F32),
                        pltpu.VMEM((tm, d), BF16),
                        pltpu.VMEM((tm, d), F32),
                        pltpu.SemaphoreType.DMA((2,)), pltpu.SemaphoreType.DMA((2,))])
    return pl.pallas_call(
        _moe_kernel,
        grid_spec=grid_spec,
        out_shape=jax.ShapeDtypeStruct((t, d), F32),
        compiler_params=pltpu.CompilerParams(dimension_semantics=("arbitrary",),
                                             vmem_limit_bytes=VMEM_LIMIT),
        name="moe",
    )(*tables, hx, g.reshape(1, d), w1.astype(BF16), w3.astype(BF16), w2.astype(BF16),
      g_final.reshape(1, d))


def _layer(h2, mem2, batch, seq, p, l, g_final, tm):
    d = h2.shape[1]
    n_chunks = tm // SSM_CHUNK
    n_scan = int(math.log2(n_chunks))
    tables = _ssm_tables(p["ssm_a_re"][l], p["ssm_a_im"][l], p["ssm_log_dt"][l], p["ssm_b_re"][l],
                         p["ssm_b_im"][l], p["ssm_c_re"][l], p["ssm_c_im"][l], p["ssm_d"][l],
                         SSM_CHUNK, n_scan, n_chunks // SUBLANES)
    h2 = _mixer(h2, seq, p["norm_mix"][l], p["w_in"][l], p["pool_w"][l], p["pool_scale"][l], tables,
                p["glu_w"][l], p["glu_b"][l], p["br_pool"][l], p["br_ssm"][l], p["w_out"][l], tm)

    m_len = mem2.shape[0] // batch
    hd = d // X_HEADS
    kv = _kv_proj(mem2, p["norm_mem"][l], p["x_wkv"][l])
    kt = kv[:, :d].reshape(batch, m_len, X_HEADS, hd).transpose(0, 2, 3, 1).astype(BF16)
    v = kv[:, d:].reshape(batch, m_len, X_HEADS, hd).transpose(0, 2, 1, 3).astype(BF16)

    n_g, n_eg = N_EXPERT_GROUPS, EXPERTS_PER_GROUP
    w_r = jnp.concatenate([p["router_g_w"][l], p["router_e_w"][l].reshape(d, n_g * n_eg)], axis=1)
    w_r = jnp.pad(w_r, ((0, 0), (0, LANES - w_r.shape[1])))
    b_r = jnp.concatenate([p["router_g_b"][l], p["router_e_b"][l].reshape(n_g * n_eg)])
    b_r = jnp.pad(b_r, (0, LANES - b_r.shape[0])).reshape(1, LANES)
    hx = _xattn(h2, seq, p["norm_x"][l], p["x_wq"][l], kt, v, p["x_wo"][l], p["norm_ffn"][l],
                w_r, b_r, tm)
    return _moe(hx, p["norm_ffn"][l], p["exp_w1"][l], p["exp_w3"][l], p["exp_w2"][l], g_final, tm)


def kernel(x, mem, norm_mix, w_in, pool_w, pool_scale, ssm_a_re, ssm_a_im, ssm_log_dt, ssm_b_re,
           ssm_b_im, ssm_c_re, ssm_c_im, ssm_d, glu_w, glu_b, br_pool, br_ssm, w_out, norm_x,
           norm_mem, x_wq, x_wkv, x_wo, norm_ffn, router_g_w, router_g_b, router_e_w, router_e_b,
           exp_w1, exp_w3, exp_w2, norm_final):
    p = dict(norm_mix=norm_mix, w_in=w_in, pool_w=pool_w, pool_scale=pool_scale, ssm_a_re=ssm_a_re,
             ssm_a_im=ssm_a_im, ssm_log_dt=ssm_log_dt, ssm_b_re=ssm_b_re, ssm_b_im=ssm_b_im,
             ssm_c_re=ssm_c_re, ssm_c_im=ssm_c_im, ssm_d=ssm_d, glu_w=glu_w, glu_b=glu_b,
             br_pool=br_pool, br_ssm=br_ssm, w_out=w_out, norm_x=norm_x, norm_mem=norm_mem,
             x_wq=x_wq, x_wkv=x_wkv, x_wo=x_wo, norm_ffn=norm_ffn, router_g_w=router_g_w,
             router_g_b=router_g_b, router_e_w=router_e_w, router_e_b=router_e_b, exp_w1=exp_w1,
             exp_w3=exp_w3, exp_w2=exp_w2)
    batch, seq, d = x.shape
    assert norm_mix.shape[0] == 1, "final norm is fused into the (single) layer's last stage"
    tm = min(512, seq)
    h2 = x.reshape(batch * seq, d)
    mem2 = mem.reshape(-1, d)
    out = _layer(h2, mem2, batch, seq, p, 0, norm_final, tm)
    return out.reshape(batch, seq, d)
```

```python
import functools
import math

import jax
import jax.numpy as jnp
from jax import lax
from jax.experimental import pallas as pl
from jax.experimental.pallas import tpu as pltpu

F32 = jnp.float32
BF16 = jnp.bfloat16

RMS_EPS = 1e-6
POOL_WINDOWS = (2, 4, 8, 16)
POOL_HALO = 16
SSM_GROUP_DIM = 16
SSM_STATE = 64
LANES = 128
SUBLANES = 8
OCT = LANES // SSM_GROUP_DIM
SSM_CHUNK = 4
X_HEADS = 4
N_EXPERT_GROUPS = 4
EXPERTS_PER_GROUP = 4
EXPERT_PAIRS = ((0, 1), (0, 2), (0, 3), (1, 2), (1, 3), (2, 3))
VMEM_LIMIT = 56 * 1024 * 1024


def _rms(x, g):
    return x * lax.rsqrt(jnp.mean(x * x, axis=-1, keepdims=True) + RMS_EPS) * g


def _const_spec(shape):
    nd = len(shape)
    return pl.BlockSpec(shape, lambda *_: (0,) * nd, pipeline_mode=pl.Buffered(1))


def _ssm_tables(a_re, a_im, log_dt, b_re, b_im, c_re, c_im, d_skip, chunk, n_scan, n_pow):
    hi = lax.Precision.HIGHEST
    G, N = a_re.shape
    C = SSM_GROUP_DIM
    Q = G // OCT
    dt = jnp.exp(log_dt)[:, None]
    mag = jnp.exp(a_re * dt)
    lb_re, lb_im = mag * jnp.cos(a_im * dt), mag * jnp.sin(a_im * dt)
    den = a_re * a_re + a_im * a_im
    nr, ni = lb_re - 1.0, lb_im
    f_re = ((nr * a_re + ni * a_im) / den)[..., None]
    f_im = ((ni * a_re - nr * a_im) / den)[..., None]
    bb_re = f_re * b_re - f_im * b_im
    bb_im = f_re * b_im + f_im * b_re

    pw_re, pw_im = [jnp.ones_like(lb_re)], [jnp.zeros_like(lb_re)]
    for _ in range(chunk):
        r, i = pw_re[-1], pw_im[-1]
        pw_re.append(r * lb_re - i * lb_im)
        pw_im.append(r * lb_im + i * lb_re)
    pw_re, pw_im = jnp.stack(pw_re), jnp.stack(pw_im)

    lbb_re = pw_re[:chunk, :, :, None] * bb_re - pw_im[:chunk, :, :, None] * bb_im
    lbb_im = pw_re[:chunk, :, :, None] * bb_im + pw_im[:chunk, :, :, None] * bb_re
    kern = (jnp.einsum('gon,dgnc->dgoc', c_re, lbb_re, precision=hi)
            - jnp.einsum('gon,dgnc->dgoc', c_im, lbb_im, precision=hi))
    kern = kern.at[0].add(d_skip.reshape(G, C)[:, :, None] * jnp.eye(C, dtype=F32))

    lane_g = jnp.arange(LANES) // C
    state_g = jnp.arange(OCT * N) // N
    rep_c = (jnp.arange(C)[:, None] == (jnp.arange(LANES) % C)[None, :]).astype(F32)
    rep_n = (jnp.arange(N)[:, None] == (jnp.arange(OCT * N) % N)[None, :]).astype(F32)
    same_cc = (lane_g[:, None] == lane_g[None, :]).astype(F32)
    same_cn = (lane_g[:, None] == state_g[None, :]).astype(F32)

    kt = kern.transpose(0, 1, 3, 2).reshape(chunk, Q, LANES, C)
    blk = jnp.einsum('dqrc,cl->dqrl', kt, rep_c, precision=hi) * same_cc
    zero = jnp.zeros((Q, LANES, LANES), F32)
    m_intra = jnp.concatenate(
        [jnp.concatenate([blk[jp - j] if jp >= j else zero for jp in range(chunk)], axis=2)
         for j in range(chunk)], axis=1)

    def state_half(lbb):
        z = lbb[::-1].transpose(0, 1, 3, 2).reshape(chunk, Q, LANES, N)
        w = jnp.einsum('jqrn,nl->qjrl', z, rep_n, precision=hi) * same_cn
        return w.reshape(Q, chunk * LANES, OCT * N)
    w_state = jnp.concatenate([state_half(lbb_re), state_half(lbb_im)], axis=-1)

    p_re = c_re[None] * pw_re[1:, :, None, :] - c_im[None] * pw_im[1:, :, None, :]
    p_im = c_re[None] * pw_im[1:, :, None, :] + c_im[None] * pw_re[1:, :, None, :]
    def out_half(p):
        z = p.transpose(0, 1, 3, 2).reshape(chunk, Q, OCT * N, C)
        w = jnp.einsum('jqrc,cl->qrjl', z, rep_c, precision=hi)
        w = w * same_cn.T[None, :, None, :]
        return w.reshape(Q, OCT * N, chunk * LANES)
    w_out = jnp.concatenate([out_half(p_re), out_half(-p_im)], axis=1)

    lr, li = [pw_re[chunk]], [pw_im[chunk]]
    for _ in range(n_scan - 1):
        r, i = lr[-1], li[-1]
        lr.append(r * r - i * i)
        li.append(2.0 * r * i)
    qr, qi = [jnp.ones_like(lb_re)], [jnp.zeros_like(lb_re)]
    for _ in range(n_pow - 1):
        r, i = qr[-1], qi[-1]
        qr.append(r * pw_re[chunk] - i * pw_im[chunk])
        qi.append(r * pw_im[chunk] + i * pw_re[chunk])
    lam_re = jnp.stack(lr + qr).reshape(n_scan + n_pow, G * N)
    lam_im = jnp.stack(li + qi).reshape(n_scan + n_pow, G * N)
    return m_intra.astype(BF16), w_state.astype(BF16), w_out.astype(BF16), lam_re, lam_im


def _mixer_kernel(x_ref, nm_ref, win_ref, poolw_ref, pscale_ref, mi_ref, ws_ref, wh_ref,
                  lre_ref, lim_ref, gluw_ref, glub_ref, brp_ref, brs_ref, wout_ref,
                  o_ref, zext_ref, v_ref, y_ref, carry_ref, *, tiles_per_seq, chunk):
    tm, d_model = x_ref.shape
    pool_w = zext_ref.shape[1]
    n_oct = v_ref.shape[0]
    ssm_w = n_oct * LANES
    n_chunks = tm // chunk
    n_str = SUBLANES
    per = n_chunks // n_str
    span = per * chunk
    n_scan = lre_ref.shape[0] - per
    seq_tile = pl.program_id(0) % tiles_per_seq

    @pl.when(seq_tile == 0)
    def _():
        zext_ref[0:POOL_HALO, :] = jnp.zeros((POOL_HALO, pool_w), F32)
        carry_ref[...] = jnp.zeros_like(carry_ref)

    x = x_ref[...]
    u = _rms(x, nm_ref[...]).astype(BF16)
    proj = jnp.dot(u, win_ref[...], preferred_element_type=F32)
    z = proj[:, :pool_w]
    zext_ref[POOL_HALO:, :] = z
    for q in range(n_oct):
        v_ref[q] = proj[:, pool_w + q * LANES:pool_w + (q + 1) * LANES]
    g_pool = jax.nn.sigmoid(proj[:, pool_w + ssm_w:pool_w + ssm_w + d_model])
    g_ssm = jax.nn.sigmoid(proj[:, pool_w + ssm_w + d_model:])

    pos = (seq_tile * tm + lax.broadcasted_iota(jnp.int32, (tm, 1), 0) + 1).astype(F32)
    gdim = pool_w // len(POOL_WINDOWS)
    ypool = []
    for gi, w in enumerate(POOL_WINDOWS):
        cols = slice(gi * gdim, (gi + 1) * gdim)
        acc = z[:, cols]
        for dlt in range(1, w):
            acc = acc + zext_ref[POOL_HALO - dlt:POOL_HALO - dlt + tm, cols]
        diff = acc / jnp.minimum(pos, float(w)) - z[:, cols]
        ypool.append(jnp.dot(diff.astype(BF16), poolw_ref[gi], preferred_element_type=F32))
    ypool = jnp.concatenate(ypool, axis=1) * pscale_ref[...]
    zext_ref[0:POOL_HALO, :] = zext_ref[tm:tm + POOL_HALO, :]
    p_br = jnp.dot(ypool.astype(BF16), brp_ref[...], preferred_element_type=F32)

    srow = lax.broadcasted_iota(jnp.int32, (n_str, 1), 0)
    half = OCT * SSM_STATE
    log_per = per.bit_length() - 1
    for q in range(n_oct):
        st = slice(q * half, (q + 1) * half)

        def lam(k):
            return lre_ref[k:k + 1, st], lim_ref[k:k + 1, st]

        vq = jnp.concatenate(
            [jnp.concatenate([v_ref[q, pl.ds(b * chunk + j, n_str, stride=span), :]
                              for j in range(chunk)], axis=1) for b in range(per)],
            axis=0).astype(BF16)
        s = jnp.dot(vq, ws_ref[q], preferred_element_type=F32)
        a_re, a_im = lam(0)
        loc_re, loc_im = [s[0:n_str, :half]], [s[0:n_str, half:]]
        for b in range(1, per):
            p_re, p_im = loc_re[-1], loc_im[-1]
            loc_re.append(s[b * n_str:(b + 1) * n_str, :half] + a_re * p_re - a_im * p_im)
            loc_im.append(s[b * n_str:(b + 1) * n_str, half:] + a_re * p_im + a_im * p_re)
        c_re, c_im = carry_ref[0:1, st], carry_ref[1:2, st]
        e_re, e_im = loc_re[-1], loc_im[-1]
        l_re, l_im = lam(log_per)
        e_re = e_re + jnp.where(srow == 0, l_re * c_re - l_im * c_im, 0.0)
        e_im = e_im + jnp.where(srow == 0, l_re * c_im + l_im * c_re, 0.0)
        for k in range(n_str.bit_length() - 1):
            sh = 1 << k
            r_re = jnp.where(srow >= sh, pltpu.roll(e_re, sh, axis=0), 0.0)
            r_im = jnp.where(srow >= sh, pltpu.roll(e_im, sh, axis=0), 0.0)
            l_re, l_im = lam(log_per + k)
            e_re, e_im = e_re + l_re * r_re - l_im * r_im, e_im + l_re * r_im + l_im * r_re
        carry_ref[0:1, st] = e_re[n_str - 1:n_str, :]
        carry_ref[1:2, st] = e_im[n_str - 1:n_str, :]
        i_re = jnp.where(srow == 0, c_re, pltpu.roll(e_re, 1, axis=0))
        i_im = jnp.where(srow == 0, c_im, pltpu.roll(e_im, 1, axis=0))
        h0_re, h0_im = [i_re], [i_im]
        for b in range(1, per):
            l_re, l_im = lam(n_scan + b)
            h0_re.append(loc_re[b - 1] + l_re * i_re - l_im * i_im)
            h0_im.append(loc_im[b - 1] + l_re * i_im + l_im * i_re)
        h0_re = jnp.concatenate(h0_re, axis=0).astype(BF16)
        h0_im = jnp.concatenate(h0_im, axis=0).astype(BF16)
        yq = (jnp.dot(vq, mi_ref[q], preferred_element_type=F32)
              + jnp.dot(h0_re, wh_ref[q, :half, :], preferred_element_type=F32)
              + jnp.dot(h0_im, wh_ref[q, half:, :], preferred_element_type=F32))
        for b in range(per):
            for j in range(chunk):
                y_ref[q, pl.ds(b * chunk + j, n_str, stride=span), :] = (
                    yq[b * n_str:(b + 1) * n_str, j * LANES:(j + 1) * LANES])

    y = jnp.concatenate([y_ref[q] for q in range(n_oct)], axis=1)
    ys = jax.nn.gelu(y, approximate=True)
    glu = jnp.dot(ys.astype(BF16), gluw_ref[...], preferred_element_type=F32) + glub_ref[...]
    ys = ys * jax.nn.sigmoid(glu)
    s_br = jnp.dot(ys.astype(BF16), brs_ref[...], preferred_element_type=F32)

    merged = g_pool * p_br + g_ssm * s_br
    o_ref[...] = x + jnp.dot(merged.astype(BF16), wout_ref[...], preferred_element_type=F32)


def _mixer(x2, seq, nm, w_in, pool_w, pool_scale, tables, glu_w, glu_b, br_pool, br_ssm, w_out, tm):
    t, d = x2.shape
    mi, ws, wh, lre, lim = tables
    pool_width = pool_w.shape[0] * pool_w.shape[1]
    ssm_width = glu_w.shape[0]
    consts = [nm.reshape(1, d), w_in.astype(BF16), pool_w.astype(BF16),
              pool_scale.reshape(1, pool_width), mi, ws, wh, lre, lim,
              glu_w.astype(BF16), glu_b.reshape(1, ssm_width), br_pool.astype(BF16),
              br_ssm.astype(BF16), w_out.astype(BF16)]
    kern = functools.partial(_mixer_kernel, tiles_per_seq=seq // tm, chunk=SSM_CHUNK)
    return pl.pallas_call(
        kern,
        grid=(t // tm,),
        in_specs=[pl.BlockSpec((tm, d), lambda i: (i, 0))] + [_const_spec(c.shape) for c in consts],
        out_specs=pl.BlockSpec((tm, d), lambda i: (i, 0)),
        out_shape=jax.ShapeDtypeStruct((t, d), F32),
        scratch_shapes=[pltpu.VMEM((POOL_HALO + tm, pool_width), F32),
                        pltpu.VMEM((ssm_width // LANES, tm, LANES), F32),
                        pltpu.VMEM((ssm_width // LANES, tm, LANES), F32),
                        pltpu.VMEM((2, lre.shape[1]), F32)],
        compiler_params=pltpu.CompilerParams(dimension_semantics=("arbitrary",),
                                             vmem_limit_bytes=VMEM_LIMIT),
        name="mixer",
    )(x2, *consts)


def _kv_kernel(mem_ref, g_ref, wkv_ref, o_ref):
    mn = _rms(mem_ref[...], g_ref[...]).astype(BF16)
    o_ref[...] = jnp.dot(mn, wkv_ref[...], preferred_element_type=F32)


def _kv_proj(mem2, g, w_kv):
    m, d = mem2.shape
    return pl.pallas_call(
        _kv_kernel,
        grid=(1,),
        in_specs=[pl.BlockSpec((m, d), lambda i: (0, 0)), _const_spec((1, d)),
                  _const_spec(w_kv.shape)],
        out_specs=pl.BlockSpec((m, w_kv.shape[1]), lambda i: (0, 0)),
        out_shape=jax.ShapeDtypeStruct((m, w_kv.shape[1]), F32),
        compiler_params=pltpu.CompilerParams(vmem_limit_bytes=VMEM_LIMIT),
        name="kv_proj",
    )(mem2, g.reshape(1, d), w_kv.astype(BF16))


def _route(logits):
    n_g, n_eg = N_EXPERT_GROUPS, EXPERTS_PER_GROUP
    tm = logits.shape[0]
    lt = logits.T
    neg = jnp.float32(-jnp.inf)
    big = jnp.int32(1 << 20)

    def first_max(v):
        row = lax.broadcasted_iota(jnp.int32, v.shape, 0)
        m = jnp.max(v, axis=0, keepdims=True)
        return m, jnp.min(jnp.where(v == m, row, big), axis=0, keepdims=True)

    gl = lt[0:n_g]
    gmax, g_idx = first_max(gl)
    g_gate = 1.0 / jnp.sum(jnp.exp(gl - gmax), axis=0, keepdims=True)
    el = lt[n_g:n_g + n_eg]
    for g in range(1, n_g):
        el = jnp.where(g_idx == g, lt[n_g + g * n_eg:n_g + (g + 1) * n_eg], el)
    row = lax.broadcasted_iota(jnp.int32, el.shape, 0)
    t1, i1 = first_max(el)
    t2, i2 = first_max(jnp.where(row == i1, neg, el))
    e2 = jnp.exp(t2 - t1)
    w1 = 1.0 / (1.0 + e2)
    w2 = e2 / (1.0 + e2)
    comb = jnp.where(row == i1, w1, jnp.where(row == i2, w2, 0.0)) * g_gate
    lo, hi = jnp.minimum(i1, i2), jnp.maximum(i1, i2)
    pair = jnp.where(lo == 0, hi - 1, jnp.where(lo == 1, hi + 1, len(EXPERT_PAIRS) - 1))
    cls = (g_idx * len(EXPERT_PAIRS) + pair).astype(F32)
    info = jnp.concatenate([comb, cls, jnp.zeros((LANES - n_eg - 1, tm), F32)], axis=0)
    return info.T


def _dot_bf16x3(x, w_hilo):
    n = w_hilo.shape[1] // 2
    x_hi = x.astype(BF16)
    x_lo = (x - x_hi.astype(F32)).astype(BF16)
    both = jnp.dot(x_hi, w_hilo, preferred_element_type=F32)
    return both[:, :n] + both[:, n:] + jnp.dot(x_lo, w_hilo[:, :n], preferred_element_type=F32)


def _xattn_kernel(h_ref, g_ref, wq_ref, kt_ref, v_ref, wo_ref, gffn_ref, wr_ref, br_ref, o_ref,
                  *, scale):
    h = h_ref[...]
    d = h.shape[1]
    hn = _rms(h, g_ref[...]).astype(BF16)
    q = jnp.dot(hn, wq_ref[...], preferred_element_type=F32)
    n_heads, hd, _ = kt_ref.shape[1:]
    outs = []
    for hh in range(n_heads):
        qh = q[:, hh * hd:(hh + 1) * hd].astype(BF16)
        s = jnp.dot(qh, kt_ref[0, hh], preferred_element_type=F32) * scale
        s = s - jnp.max(s, axis=-1, keepdims=True)
        p = jnp.exp(s)
        p = p / jnp.sum(p, axis=-1, keepdims=True)
        outs.append(jnp.dot(p.astype(BF16), v_ref[0, hh], preferred_element_type=F32))
    o = jnp.concatenate(outs, axis=1).astype(BF16)
    h_out = h + jnp.dot(o, wo_ref[...], preferred_element_type=F32)
    zn = _rms(h_out, gffn_ref[...])
    logits = _dot_bf16x3(zn, wr_ref[...]) + br_ref[...]
    o_ref[:, :d] = h_out
    o_ref[:, d:] = _route(logits)


def _xattn(h2, seq, g, w_q, kt, v, w_o, g_ffn, w_r, b_r, tm):
    t, d = h2.shape
    tiles_per_seq = seq // tm
    hd = d // X_HEADS
    kern = functools.partial(_xattn_kernel, scale=hd ** -0.5)
    w_r_hi = w_r.astype(BF16)
    w_r_lo = (w_r - w_r_hi.astype(F32)).astype(BF16)
    w_r_hilo = jnp.concatenate([w_r_hi, w_r_lo], axis=1)
    return pl.pallas_call(
        kern,
        grid=(t // tm,),
        in_specs=[pl.BlockSpec((tm, d), lambda i: (i, 0)), _const_spec((1, d)),
                  _const_spec(w_q.shape),
                  pl.BlockSpec((1,) + kt.shape[1:], lambda i: (i // tiles_per_seq, 0, 0, 0)),
                  pl.BlockSpec((1,) + v.shape[1:], lambda i: (i // tiles_per_seq, 0, 0, 0)),
                  _const_spec(w_o.shape), _const_spec((1, d)), _const_spec(w_r_hilo.shape),
                  _const_spec(b_r.shape)],
        out_specs=pl.BlockSpec((tm, d + LANES), lambda i: (i, 0)),
        out_shape=jax.ShapeDtypeStruct((t, d + LANES), F32),
        compiler_params=pltpu.CompilerParams(dimension_semantics=("arbitrary",),
                                             vmem_limit_bytes=VMEM_LIMIT),
        name="xattn",
    )(h2, g.reshape(1, d), w_q.astype(BF16), kt, v, w_o.astype(BF16), g_ffn.reshape(1, d),
      w_r_hilo, b_r)


def _dispatch_tables(cls, tm):
    t = cls.shape[0]
    n_g, n_eg, n_pair = N_EXPERT_GROUPS, EXPERTS_PER_GROUP, len(EXPERT_PAIRS)
    n_tiles = t // tm + n_g
    i32 = jnp.int32
    _, tok_sorted = lax.sort((cls, jnp.arange(t, dtype=i32)), num_keys=1, is_stable=True)
    tok_sorted = jnp.concatenate([tok_sorted, jnp.zeros((tm,), i32)])
    counts = jnp.sum((cls[:, None] == jnp.arange(n_g * n_pair, dtype=i32)[None, :]).astype(i32), axis=0)
    c_start = jnp.cumsum(counts) - counts
    g_count = counts.reshape(n_g, n_pair).sum(axis=1)
    g_start = jnp.cumsum(g_count) - g_count
    tiles_g = (g_count + tm - 1) // tm
    tile_end = jnp.cumsum(tiles_g)
    tile_first = tile_end - tiles_g
    tile_ids = jnp.arange(n_tiles, dtype=i32)
    tile_group = jnp.minimum(jnp.sum((tile_ids[:, None] >= tile_end[None, :]).astype(i32), axis=1),
                             n_g - 1)
    k_in_group = tile_ids - tile_first[tile_group]
    tile_valid = jnp.clip(g_count[tile_group] - k_in_group * tm, 0, tm)
    tile_valid = jnp.where(tile_ids < tile_end[-1], tile_valid, 0)
    tile_start = jnp.where(tile_valid > 0, g_start[tile_group] + k_in_group * tm, 0)
    lo, hi = tile_start[:, None], (tile_start + tile_valid)[:, None]
    inter = (counts[None, :] > 0) & (c_start[None, :] < hi) & ((c_start + counts)[None, :] > lo)
    inter = inter.reshape(n_tiles, n_g, n_pair).any(axis=1)
    pair_has = jnp.array([[e in p for e in range(n_eg)] for p in EXPERT_PAIRS])
    need = (inter[:, :, None] & pair_has[None]).any(axis=1)
    tile_nexp = need.sum(axis=1)
    tile_exp = jnp.argsort(jnp.logical_not(need), axis=1, stable=True)
    return (tok_sorted, tile_start.astype(i32), tile_group.astype(i32), tile_valid.astype(i32),
            tile_nexp.astype(i32), tile_exp.reshape(-1).astype(i32))


def _moe_kernel(tok_ref, ts_ref, tg_ref, tv_ref, tn_ref, te_ref, hx_hbm, g_ref, w1_ref, w3_ref, w2_ref,
                gf_ref, out_hbm, xnext, osend, xcur, zn_ref, acc_ref, gsem, ssem):
    i = pl.program_id(0)
    n_blk, rows = xnext.shape[0] - 1, xnext.shape[1]
    tm = n_blk * rows
    d = osend.shape[2]
    n_eg = w1_ref.shape[0]

    def row_in(tok, blk, k):
        return pltpu.make_async_copy(hx_hbm.at[pl.ds(tok, 1), :], xnext.at[blk, pl.ds(k, 1), :], gsem)

    def row_out(tok, blk, k):
        return pltpu.make_async_copy(osend.at[blk, pl.ds(k, 1), :], out_hbm.at[pl.ds(tok, 1), :], ssem)

    def tile_in_wait():
        pltpu.make_async_copy(xnext.at[pl.ds(0, n_blk)], xnext.at[pl.ds(0, n_blk)], gsem).wait()

    def tile_out_wait():
        pltpu.make_async_copy(osend.at[pl.ds(0, n_blk)], osend.at[pl.ds(0, n_blk)], ssem).wait()

    def anchor(width):
        zero = xnext[n_blk, 0:1, 0:width]
        osend[n_blk, 0:1, 0:width] = zero
        return zero + osend[n_blk, 0:1, 0:width]

    def expert(zn, k, move_rows=None):
        e = te_ref[i * n_eg + k]
        if move_rows is not None:
            move_rows(0)
        a = jnp.dot(zn, w1_ref[e], preferred_element_type=F32)
        if move_rows is not None:
            a = a + anchor(a.shape[1])
            move_rows(1)
        b = jnp.dot(zn, w3_ref[e], preferred_element_type=F32)
        hid = (a * jax.nn.sigmoid(a) * b).astype(BF16)
        route = xcur[:, d:]
        lane = lax.broadcasted_iota(jnp.int32, route.shape, 1)
        comb = jnp.sum(jnp.where(lane == e, route, 0.0), axis=1, keepdims=True)
        out = comb * jnp.dot(hid, w2_ref[e], preferred_element_type=F32)
        if move_rows is not None:
            out = out + anchor(d)
        return out

    busy = tv_ref[i] > 0
    after_busy = tv_ref[jnp.maximum(i - 1, 0)] > 0
    drain = jnp.logical_and(jnp.logical_not(busy), jnp.logical_and(i >= 1, after_busy))

    @pl.when(i == 0)
    def _():
        xnext[n_blk] = jnp.zeros(xnext.shape[1:], F32)
        osend[...] = jnp.zeros(osend.shape, F32)
        base = ts_ref[0]

        def body(blk, c):
            for k in range(rows):
                row_in(tok_ref[base + blk * rows + k], blk, k).start()
            return c
        lax.fori_loop(0, n_blk, body, 0)

    @pl.when(jnp.logical_or(busy, drain))
    def _():
        tile_in_wait()

        @pl.when(i >= 1)
        def _():
            tile_out_wait()
            osend[0:n_blk] = acc_ref[...].reshape(n_blk, rows, d)
        xcur[...] = xnext[0:n_blk].reshape(tm, xnext.shape[2])

    n_always = 2

    @pl.when(busy)
    def _():
        h = xcur[:, :d]
        zn = _rms(h, g_ref[...]).astype(BF16)
        zn_ref[...] = zn
        acc = h
        nxt = ts_ref[i + 1]
        prv = ts_ref[jnp.maximum(i - 1, 0)]
        n_batch = 2 * n_always
        for k in range(n_always):
            def move_rows(part, k=k):
                q = 2 * k + part
                for r in range(q * tm // n_batch, (q + 1) * tm // n_batch):
                    row_in(tok_ref[nxt + r], r // rows, r % rows).start()
                    row_out(tok_ref[prv + r], r // rows, r % rows).start()
            acc = acc + expert(zn, k, move_rows)
        acc_ref[...] = acc

    for k in range(n_always, n_eg):
        @pl.when(tn_ref[i] > k)
        def _():
            acc_ref[...] += expert(zn_ref[...], k)

    @pl.when(busy)
    def _():
        acc_ref[...] = _rms(acc_ref[...], gf_ref[...])

    @pl.when(drain)
    def _():
        base = ts_ref[i - 1]
        nv = tv_ref[i - 1]
        full = nv // rows

        def body(blk, c):
            for k in range(rows):
                row_out(tok_ref[base + blk * rows + k], blk, k).start()
            return c
        lax.fori_loop(0, full, body, 0)
        for k in range(rows - 1):

            @pl.when(full * rows + k < nv)
            def _():
                row_out(tok_ref[base + full * rows + k], full, k).start()

        @pl.when(full > 0)
        def _():
            pltpu.make_async_copy(osend.at[pl.ds(0, full)], osend.at[pl.ds(0, full)], ssem).wait()
        for k in range(rows - 1):

            @pl.when(full * rows + k < nv)
            def _():
                row_out(0, 0, 0).wait()


def _moe(hx, g, w1, w3, w2, g_final, tm):
    t, dx = hx.shape
    d = dx - LANES
    n_eg = EXPERTS_PER_GROUP
    de = w1.shape[2]
    cls = hx[:, d + n_eg].astype(jnp.int32)
    tables = _dispatch_tables(cls, tm)
    n_tiles = tables[2].shape[0]

    def wmap(i, tok, ts, tg, tv, tn, te):
        return (tg[i], 0, 0)

    def cmap(i, tok, ts, tg, tv, tn, te):
        return (0, 0)

    grid_spec = pltpu.PrefetchScalarGridSpec(
        num_scalar_prefetch=len(tables),
        grid=(n_tiles,),
        in_specs=[pl.BlockSpec(memory_space=pl.ANY),
                  pl.BlockSpec((1, d), cmap),
                  pl.BlockSpec((n_eg, d, de), wmap),
                  pl.BlockSpec((n_eg, d, de), wmap),
                  pl.BlockSpec((n_eg, de, d), wmap),
                  pl.BlockSpec((1, d), cmap)],
        out_specs=pl.BlockSpec(memory_space=pl.ANY),
        scratch_shapes=[pltpu.VMEM((tm // SUBLANES + 1, SUBLANES, dx), F32),
                        pltpu.VMEM((tm // SUBLANES + 1, SUBLANES, d), F32),
                        pltpu.VMEM((tm, dx), F32),
                        pltpu.VMEM((tm, d), BF16),
                        pltpu.VMEM((tm, d), F32),
                        pltpu.SemaphoreType.DMA(()), pltpu.SemaphoreType.DMA(())])
    return pl.pallas_call(
        _moe_kernel,
        grid_spec=grid_spec,
        out_shape=jax.ShapeDtypeStruct((t, d), F32),
        compiler_params=pltpu.CompilerParams(dimension_semantics=("arbitrary",),
                                             vmem_limit_bytes=VMEM_LIMIT),
        name="moe",
    )(*tables, hx, g.reshape(1, d), w1.astype(BF16), w3.astype(BF16), w2.astype(BF16),
      g_final.reshape(1, d))


def _layer(h2, mem2, batch, seq, p, l, g_final, tm):
    d = h2.shape[1]
    n_chunks = tm // SSM_CHUNK
    n_scan = int(math.log2(n_chunks))
    tables = _ssm_tables(p["ssm_a_re"][l], p["ssm_a_im"][l], p["ssm_log_dt"][l], p["ssm_b_re"][l],
                         p["ssm_b_im"][l], p["ssm_c_re"][l], p["ssm_c_im"][l], p["ssm_d"][l],
                         SSM_CHUNK, n_scan, n_chunks // SUBLANES)
    h2 = _mixer(h2, seq, p["norm_mix"][l], p["w_in"][l], p["pool_w"][l], p["pool_scale"][l], tables,
                p["glu_w"][l], p["glu_b"][l], p["br_pool"][l], p["br_ssm"][l], p["w_out"][l], tm)

    m_len = mem2.shape[0] // batch
    hd = d // X_HEADS
    kv = _kv_proj(mem2, p["norm_mem"][l], p["x_wkv"][l])
    kt = kv[:, :d].reshape(batch, m_len, X_HEADS, hd).transpose(0, 2, 3, 1).astype(BF16)
    v = kv[:, d:].reshape(batch, m_len, X_HEADS, hd).transpose(0, 2, 1, 3).astype(BF16)

    n_g, n_eg = N_EXPERT_GROUPS, EXPERTS_PER_GROUP
    w_r = jnp.concatenate([p["router_g_w"][l], p["router_e_w"][l].reshape(d, n_g * n_eg)], axis=1)
    w_r = jnp.pad(w_r, ((0, 0), (0, LANES - w_r.shape[1])))
    b_r = jnp.concatenate([p["router_g_b"][l], p["router_e_b"][l].reshape(n_g * n_eg)])
    b_r = jnp.pad(b_r, (0, LANES - b_r.shape[0])).reshape(1, LANES)
    hx = _xattn(h2, seq, p["norm_x"][l], p["x_wq"][l], kt, v, p["x_wo"][l], p["norm_ffn"][l],
                w_r, b_r, tm)
    return _moe(hx, p["norm_ffn"][l], p["exp_w1"][l], p["exp_w3"][l], p["exp_w2"][l], g_final, tm)


def kernel(x, mem, norm_mix, w_in, pool_w, pool_scale, ssm_a_re, ssm_a_im, ssm_log_dt, ssm_b_re,
           ssm_b_im, ssm_c_re, ssm_c_im, ssm_d, glu_w, glu_b, br_pool, br_ssm, w_out, norm_x,
           norm_mem, x_wq, x_wkv, x_wo, norm_ffn, router_g_w, router_g_b, router_e_w, router_e_b,
           exp_w1, exp_w3, exp_w2, norm_final):
    p = dict(norm_mix=norm_mix, w_in=w_in, pool_w=pool_w, pool_scale=pool_scale, ssm_a_re=ssm_a_re,
             ssm_a_im=ssm_a_im, ssm_log_dt=ssm_log_dt, ssm_b_re=ssm_b_re, ssm_b_im=ssm_b_im,
             ssm_c_re=ssm_c_re, ssm_c_im=ssm_c_im, ssm_d=ssm_d, glu_w=glu_w, glu_b=glu_b,
             br_pool=br_pool, br_ssm=br_ssm, w_out=w_out, norm_x=norm_x, norm_mem=norm_mem,
             x_wq=x_wq, x_wkv=x_wkv, x_wo=x_wo, norm_ffn=norm_ffn, router_g_w=router_g_w,
             router_g_b=router_g_b, router_e_w=router_e_w, router_e_b=router_e_b, exp_w1=exp_w1,
             exp_w3=exp_w3, exp_w2=exp_w2)
    batch, seq, d = x.shape
    assert norm_mix.shape[0] == 1, "final norm is fused into the (single) layer's last stage"
    tm = min(512, seq)
    h2 = x.reshape(batch * seq, d)
    mem2 = mem.reshape(-1, d)
    out = _layer(h2, mem2, batch, seq, p, 0, norm_final, tm)
    return out.reshape(batch, seq, d)
```

```python
import functools
import math

import jax
import jax.numpy as jnp
from jax import lax
from jax.experimental import pallas as pl
from jax.experimental.pallas import tpu as pltpu

F32 = jnp.float32
BF16 = jnp.bfloat16

RMS_EPS = 1e-6
POOL_WINDOWS = (2, 4, 8, 16)
POOL_HALO = 16
SSM_GROUP_DIM = 16
SSM_STATE = 64
LANES = 128
SUBLANES = 8
OCT = LANES // SSM_GROUP_DIM
SSM_CHUNK = 4
STRAND_PAD = 8
X_HEADS = 4
MIXER_TILE = 1024
XATTN_TILE = 1024
N_EXPERT_GROUPS = 4
EXPERTS_PER_GROUP = 4
EXPERT_PAIRS = ((0, 1), (0, 2), (0, 3), (1, 2), (1, 3), (2, 3))
VMEM_LIMIT = 56 * 1024 * 1024


def _rms(x, g):
    return x * lax.rsqrt(jnp.mean(x * x, axis=-1, keepdims=True) + RMS_EPS) * g


def _const_spec(shape):
    nd = len(shape)
    return pl.BlockSpec(shape, lambda *_: (0,) * nd, pipeline_mode=pl.Buffered(1))


def _ssm_tables(a_re, a_im, log_dt, b_re, b_im, c_re, c_im, d_skip, chunk, n_scan, n_pow):
    hi = lax.Precision.HIGHEST
    G, N = a_re.shape
    C = SSM_GROUP_DIM
    Q = G // OCT
    dt = jnp.exp(log_dt)[:, None]
    mag = jnp.exp(a_re * dt)
    lb_re, lb_im = mag * jnp.cos(a_im * dt), mag * jnp.sin(a_im * dt)
    den = a_re * a_re + a_im * a_im
    nr, ni = lb_re - 1.0, lb_im
    f_re = ((nr * a_re + ni * a_im) / den)[..., None]
    f_im = ((ni * a_re - nr * a_im) / den)[..., None]
    bb_re = f_re * b_re - f_im * b_im
    bb_im = f_re * b_im + f_im * b_re

    pw_re, pw_im = [jnp.ones_like(lb_re)], [jnp.zeros_like(lb_re)]
    for _ in range(chunk):
        r, i = pw_re[-1], pw_im[-1]
        pw_re.append(r * lb_re - i * lb_im)
        pw_im.append(r * lb_im + i * lb_re)
    pw_re, pw_im = jnp.stack(pw_re), jnp.stack(pw_im)

    lbb_re = pw_re[:chunk, :, :, None] * bb_re - pw_im[:chunk, :, :, None] * bb_im
    lbb_im = pw_re[:chunk, :, :, None] * bb_im + pw_im[:chunk, :, :, None] * bb_re
    kern = (jnp.einsum('gon,dgnc->dgoc', c_re, lbb_re, precision=hi)
            - jnp.einsum('gon,dgnc->dgoc', c_im, lbb_im, precision=hi))
    kern = kern.at[0].add(d_skip.reshape(G, C)[:, :, None] * jnp.eye(C, dtype=F32))

    lane_g = jnp.arange(LANES) // C
    state_g = jnp.arange(OCT * N) // N
    rep_c = (jnp.arange(C)[:, None] == (jnp.arange(LANES) % C)[None, :]).astype(F32)
    rep_n = (jnp.arange(N)[:, None] == (jnp.arange(OCT * N) % N)[None, :]).astype(F32)
    same_cc = (lane_g[:, None] == lane_g[None, :]).astype(F32)
    same_cn = (lane_g[:, None] == state_g[None, :]).astype(F32)

    kt = kern.transpose(0, 1, 3, 2).reshape(chunk, Q, LANES, C)
    blk = jnp.einsum('dqrc,cl->dqrl', kt, rep_c, precision=hi) * same_cc
    zero = jnp.zeros((Q, LANES, LANES), F32)
    m_intra = jnp.concatenate(
        [jnp.concatenate([blk[jp - j] if jp >= j else zero for jp in range(chunk)], axis=2)
         for j in range(chunk)], axis=1)

    def state_half(lbb):
        z = lbb[::-1].transpose(0, 1, 3, 2).reshape(chunk, Q, LANES, N)
        w = jnp.einsum('jqrn,nl->qjrl', z, rep_n, precision=hi) * same_cn
        return w.reshape(Q, chunk * LANES, OCT * N)
    w_state = jnp.concatenate([state_half(lbb_re), state_half(lbb_im)], axis=-1)

    p_re = c_re[None] * pw_re[1:, :, None, :] - c_im[None] * pw_im[1:, :, None, :]
    p_im = c_re[None] * pw_im[1:, :, None, :] + c_im[None] * pw_re[1:, :, None, :]
    def out_half(p):
        z = p.transpose(0, 1, 3, 2).reshape(chunk, Q, OCT * N, C)
        w = jnp.einsum('jqrc,cl->qrjl', z, rep_c, precision=hi)
        w = w * same_cn.T[None, :, None, :]
        return w.reshape(Q, OCT * N, chunk * LANES)
    w_out = jnp.concatenate([out_half(p_re), out_half(-p_im)], axis=1)

    lr, li = [pw_re[chunk]], [pw_im[chunk]]
    for _ in range(n_scan - 1):
        r, i = lr[-1], li[-1]
        lr.append(r * r - i * i)
        li.append(2.0 * r * i)
    qr, qi = [jnp.ones_like(lb_re)], [jnp.zeros_like(lb_re)]
    for _ in range(n_pow - 1):
        r, i = qr[-1], qi[-1]
        qr.append(r * pw_re[chunk] - i * pw_im[chunk])
        qi.append(r * pw_im[chunk] + i * pw_re[chunk])
    lam_re = jnp.stack(lr + qr).reshape(n_scan + n_pow, G * N)
    lam_im = jnp.stack(li + qi).reshape(n_scan + n_pow, G * N)
    return m_intra.astype(BF16), w_state.astype(BF16), w_out.astype(BF16), lam_re, lam_im


def _mixer_kernel(x_ref, nm_ref, win_ref, poolw_ref, pscale_ref, mi_ref, ws_ref, wh_ref,
                  lre_ref, lim_ref, gluw_ref, glub_ref, brp_ref, brs_ref, wout_ref,
                  o_ref, zext_ref, v_ref, y_ref, carry_ref, *, tiles_per_seq, chunk):
    tm, d_model = x_ref.shape
    pool_w = zext_ref.shape[1]
    n_oct = v_ref.shape[0]
    ssm_w = n_oct * LANES
    n_chunks = tm // chunk
    n_str = SUBLANES
    per = n_chunks // n_str
    span = per * chunk
    pitch = v_ref.shape[1] // n_str
    n_scan = lre_ref.shape[0] - per
    seq_tile = pl.program_id(0) % tiles_per_seq

    @pl.when(seq_tile == 0)
    def _():
        zext_ref[0:POOL_HALO, :] = jnp.zeros((POOL_HALO, pool_w), F32)
        carry_ref[...] = jnp.zeros_like(carry_ref)

    x = x_ref[...]
    u = _rms(x, nm_ref[...]).astype(BF16)
    proj = jnp.dot(u, win_ref[...], preferred_element_type=F32)
    z = proj[:, :pool_w]
    zext_ref[POOL_HALO:, :] = z
    for q in range(n_oct):
        for r in range(n_str):
            v_ref[q, r * pitch:r * pitch + span, :] = (
                proj[r * span:(r + 1) * span, pool_w + q * LANES:pool_w + (q + 1) * LANES])
    g_pool = jax.nn.sigmoid(proj[:, pool_w + ssm_w:pool_w + ssm_w + d_model])
    g_ssm = jax.nn.sigmoid(proj[:, pool_w + ssm_w + d_model:])

    pos = (seq_tile * tm + lax.broadcasted_iota(jnp.int32, (tm, 1), 0) + 1).astype(F32)
    gdim = pool_w // len(POOL_WINDOWS)
    ypool = []
    for gi, w in enumerate(POOL_WINDOWS):
        cols = slice(gi * gdim, (gi + 1) * gdim)
        acc = z[:, cols]
        for dlt in range(1, w):
            acc = acc + zext_ref[POOL_HALO - dlt:POOL_HALO - dlt + tm, cols]
        diff = acc / jnp.minimum(pos, float(w)) - z[:, cols]
        ypool.append(jnp.dot(diff.astype(BF16), poolw_ref[gi], preferred_element_type=F32))
    ypool = jnp.concatenate(ypool, axis=1) * pscale_ref[...]
    zext_ref[0:POOL_HALO, :] = zext_ref[tm:tm + POOL_HALO, :]
    p_br = jnp.dot(ypool.astype(BF16), brp_ref[...], preferred_element_type=F32)

    srow = lax.broadcasted_iota(jnp.int32, (n_str, 1), 0)
    half = OCT * SSM_STATE
    log_per = per.bit_length() - 1
    for q in range(n_oct):
        st = slice(q * half, (q + 1) * half)

        def lam(k):
            return lre_ref[k:k + 1, st], lim_ref[k:k + 1, st]

        vq = jnp.concatenate(
            [jnp.concatenate([v_ref[q, pl.ds(b * chunk + j, n_str, stride=pitch), :]
                              for j in range(chunk)], axis=1) for b in range(per)],
            axis=0).astype(BF16)
        s = jnp.dot(vq, ws_ref[q], preferred_element_type=F32)
        a_re, a_im = lam(0)
        loc_re, loc_im = [s[0:n_str, :half]], [s[0:n_str, half:]]
        for b in range(1, per):
            p_re, p_im = loc_re[-1], loc_im[-1]
            loc_re.append(s[b * n_str:(b + 1) * n_str, :half] + a_re * p_re - a_im * p_im)
            loc_im.append(s[b * n_str:(b + 1) * n_str, half:] + a_re * p_im + a_im * p_re)
        c_re, c_im = carry_ref[0:1, st], carry_ref[1:2, st]
        e_re, e_im = loc_re[-1], loc_im[-1]
        l_re, l_im = lam(log_per)
        e_re = e_re + jnp.where(srow == 0, l_re * c_re - l_im * c_im, 0.0)
        e_im = e_im + jnp.where(srow == 0, l_re * c_im + l_im * c_re, 0.0)
        for k in range(n_str.bit_length() - 1):
            sh = 1 << k
            r_re = jnp.where(srow >= sh, pltpu.roll(e_re, sh, axis=0), 0.0)
            r_im = jnp.where(srow >= sh, pltpu.roll(e_im, sh, axis=0), 0.0)
            l_re, l_im = lam(log_per + k)
            e_re, e_im = e_re + l_re * r_re - l_im * r_im, e_im + l_re * r_im + l_im * r_re
        carry_ref[0:1, st] = e_re[n_str - 1:n_str, :]
        carry_ref[1:2, st] = e_im[n_str - 1:n_str, :]
        i_re = jnp.where(srow == 0, c_re, pltpu.roll(e_re, 1, axis=0))
        i_im = jnp.where(srow == 0, c_im, pltpu.roll(e_im, 1, axis=0))
        h0_re, h0_im = [i_re], [i_im]
        for b in range(1, per):
            l_re, l_im = lam(n_scan + b)
            h0_re.append(loc_re[b - 1] + l_re * i_re - l_im * i_im)
            h0_im.append(loc_im[b - 1] + l_re * i_im + l_im * i_re)
        h0_re = jnp.concatenate(h0_re, axis=0).astype(BF16)
        h0_im = jnp.concatenate(h0_im, axis=0).astype(BF16)
        yq = (jnp.dot(vq, mi_ref[q], preferred_element_type=F32)
              + jnp.dot(h0_re, wh_ref[q, :half, :], preferred_element_type=F32)
              + jnp.dot(h0_im, wh_ref[q, half:, :], preferred_element_type=F32))
        for b in range(per):
            for j in range(chunk):
                y_ref[q, pl.ds(b * chunk + j, n_str, stride=pitch), :] = (
                    yq[b * n_str:(b + 1) * n_str, j * LANES:(j + 1) * LANES])

    y = jnp.concatenate(
        [jnp.concatenate([y_ref[q, r * pitch:r * pitch + span, :] for r in range(n_str)], axis=0)
         for q in range(n_oct)], axis=1)
    ys = jax.nn.gelu(y, approximate=True)
    glu = jnp.dot(ys.astype(BF16), gluw_ref[...], preferred_element_type=F32) + glub_ref[...]
    ys = ys * jax.nn.sigmoid(glu)
    s_br = jnp.dot(ys.astype(BF16), brs_ref[...], preferred_element_type=F32)

    merged = g_pool * p_br + g_ssm * s_br
    o_ref[...] = x + jnp.dot(merged.astype(BF16), wout_ref[...], preferred_element_type=F32)


def _mixer(x2, seq, nm, w_in, pool_w, pool_scale, tables, glu_w, glu_b, br_pool, br_ssm, w_out, tm):
    t, d = x2.shape
    mi, ws, wh, lre, lim = tables
    pool_width = pool_w.shape[0] * pool_w.shape[1]
    ssm_width = glu_w.shape[0]
    consts = [nm.reshape(1, d), w_in.astype(BF16), pool_w.astype(BF16),
              pool_scale.reshape(1, pool_width), mi, ws, wh, lre, lim,
              glu_w.astype(BF16), glu_b.reshape(1, ssm_width), br_pool.astype(BF16),
              br_ssm.astype(BF16), w_out.astype(BF16)]
    kern = functools.partial(_mixer_kernel, tiles_per_seq=seq // tm, chunk=SSM_CHUNK)
    return pl.pallas_call(
        kern,
        grid=(t // tm,),
        in_specs=[pl.BlockSpec((tm, d), lambda i: (i, 0))] + [_const_spec(c.shape) for c in consts],
        out_specs=pl.BlockSpec((tm, d), lambda i: (i, 0)),
        out_shape=jax.ShapeDtypeStruct((t, d), F32),
        scratch_shapes=[pltpu.VMEM((POOL_HALO + tm, pool_width), F32),
                        pltpu.VMEM((ssm_width // LANES, tm + SUBLANES * STRAND_PAD, LANES), F32),
                        pltpu.VMEM((ssm_width // LANES, tm + SUBLANES * STRAND_PAD, LANES), F32),
                        pltpu.VMEM((2, lre.shape[1]), F32)],
        compiler_params=pltpu.CompilerParams(dimension_semantics=("arbitrary",),
                                             vmem_limit_bytes=VMEM_LIMIT),
        name="mixer",
    )(x2, *consts)


def _kv_kernel(mem_ref, g_ref, wkv_ref, o_ref):
    mn = _rms(mem_ref[...], g_ref[...]).astype(BF16)
    o_ref[...] = jnp.dot(mn, wkv_ref[...], preferred_element_type=F32)


def _kv_proj(mem2, g, w_kv):
    m, d = mem2.shape
    return pl.pallas_call(
        _kv_kernel,
        grid=(1,),
        in_specs=[pl.BlockSpec((m, d), lambda i: (0, 0)), _const_spec((1, d)),
                  _const_spec(w_kv.shape)],
        out_specs=pl.BlockSpec((m, w_kv.shape[1]), lambda i: (0, 0)),
        out_shape=jax.ShapeDtypeStruct((m, w_kv.shape[1]), F32),
        compiler_params=pltpu.CompilerParams(vmem_limit_bytes=VMEM_LIMIT),
        name="kv_proj",
    )(mem2, g.reshape(1, d), w_kv.astype(BF16))


def _route(logits):
    n_g, n_eg = N_EXPERT_GROUPS, EXPERTS_PER_GROUP
    tm = logits.shape[0]
    lt = logits.T
    neg = jnp.float32(-jnp.inf)
    big = jnp.int32(1 << 20)

    def first_max(v):
        row = lax.broadcasted_iota(jnp.int32, v.shape, 0)
        m = jnp.max(v, axis=0, keepdims=True)
        return m, jnp.min(jnp.where(v == m, row, big), axis=0, keepdims=True)

    gl = lt[0:n_g]
    gmax, g_idx = first_max(gl)
    g_gate = 1.0 / jnp.sum(jnp.exp(gl - gmax), axis=0, keepdims=True)
    el = lt[n_g:n_g + n_eg]
    for g in range(1, n_g):
        el = jnp.where(g_idx == g, lt[n_g + g * n_eg:n_g + (g + 1) * n_eg], el)
    row = lax.broadcasted_iota(jnp.int32, el.shape, 0)
    t1, i1 = first_max(el)
    t2, i2 = first_max(jnp.where(row == i1, neg, el))
    e2 = jnp.exp(t2 - t1)
    w1 = 1.0 / (1.0 + e2)
    w2 = e2 / (1.0 + e2)
    comb = jnp.where(row == i1, w1, jnp.where(row == i2, w2, 0.0)) * g_gate
    lo, hi = jnp.minimum(i1, i2), jnp.maximum(i1, i2)
    pair = jnp.where(lo == 0, hi - 1, jnp.where(lo == 1, hi + 1, len(EXPERT_PAIRS) - 1))
    cls = (g_idx * len(EXPERT_PAIRS) + pair).astype(F32)
    info = jnp.concatenate([comb, cls, jnp.zeros((LANES - n_eg - 1, tm), F32)], axis=0)
    return info.T


def _dot_bf16x3(x, w_hilo):
    n = w_hilo.shape[1] // 2
    x_hi = x.astype(BF16)
    x_lo = (x - x_hi.astype(F32)).astype(BF16)
    both = jnp.dot(x_hi, w_hilo, preferred_element_type=F32)
    return both[:, :n] + both[:, n:] + jnp.dot(x_lo, w_hilo[:, :n], preferred_element_type=F32)


def _xattn_kernel(h_ref, g_ref, wq_ref, kt_ref, v_ref, wo_ref, gffn_ref, wr_ref, br_ref, o_ref,
                  *, scale):
    h = h_ref[...]
    d = h.shape[1]
    hn = _rms(h, g_ref[...]).astype(BF16)
    q = jnp.dot(hn, wq_ref[...], preferred_element_type=F32)
    n_heads, hd, _ = kt_ref.shape[1:]
    outs = []
    for hh in range(n_heads):
        qh = q[:, hh * hd:(hh + 1) * hd].astype(BF16)
        s = jnp.dot(qh, kt_ref[0, hh], preferred_element_type=F32) * scale
        s = s - jnp.max(s, axis=-1, keepdims=True)
        p = jnp.exp(s)
        p = p / jnp.sum(p, axis=-1, keepdims=True)
        outs.append(jnp.dot(p.astype(BF16), v_ref[0, hh], preferred_element_type=F32))
    o = jnp.concatenate(outs, axis=1).astype(BF16)
    h_out = h + jnp.dot(o, wo_ref[...], preferred_element_type=F32)
    zn = _rms(h_out, gffn_ref[...])
    logits = _dot_bf16x3(zn, wr_ref[...]) + br_ref[...]
    o_ref[:, :d] = h_out
    o_ref[:, d:] = _route(logits)


def _xattn(h2, seq, g, w_q, kt, v, w_o, g_ffn, w_r, b_r, tm):
    t, d = h2.shape
    tiles_per_seq = seq // tm
    hd = d // X_HEADS
    kern = functools.partial(_xattn_kernel, scale=hd ** -0.5)
    w_r_hi = w_r.astype(BF16)
    w_r_lo = (w_r - w_r_hi.astype(F32)).astype(BF16)
    w_r_hilo = jnp.concatenate([w_r_hi, w_r_lo], axis=1)
    return pl.pallas_call(
        kern,
        grid=(t // tm,),
        in_specs=[pl.BlockSpec((tm, d), lambda i: (i, 0)), _const_spec((1, d)),
                  _const_spec(w_q.shape),
                  pl.BlockSpec((1,) + kt.shape[1:], lambda i: (i // tiles_per_seq, 0, 0, 0)),
                  pl.BlockSpec((1,) + v.shape[1:], lambda i: (i // tiles_per_seq, 0, 0, 0)),
                  _const_spec(w_o.shape), _const_spec((1, d)), _const_spec(w_r_hilo.shape),
                  _const_spec(b_r.shape)],
        out_specs=pl.BlockSpec((tm, d + LANES), lambda i: (i, 0)),
        out_shape=jax.ShapeDtypeStruct((t, d + LANES), F32),
        compiler_params=pltpu.CompilerParams(dimension_semantics=("arbitrary",),
                                             vmem_limit_bytes=VMEM_LIMIT),
        name="xattn",
    )(h2, g.reshape(1, d), w_q.astype(BF16), kt, v, w_o.astype(BF16), g_ffn.reshape(1, d),
      w_r_hilo, b_r)


def _dispatch_tables(cls, tm):
    t = cls.shape[0]
    n_g, n_eg, n_pair = N_EXPERT_GROUPS, EXPERTS_PER_GROUP, len(EXPERT_PAIRS)
    n_tiles = t // tm + n_g
    i32 = jnp.int32
    _, tok_sorted = lax.sort((cls, jnp.arange(t, dtype=i32)), num_keys=1, is_stable=True)
    tok_sorted = jnp.concatenate([tok_sorted, jnp.zeros((tm,), i32)])
    counts = jnp.sum((cls[:, None] == jnp.arange(n_g * n_pair, dtype=i32)[None, :]).astype(i32), axis=0)
    c_start = jnp.cumsum(counts) - counts
    g_count = counts.reshape(n_g, n_pair).sum(axis=1)
    g_start = jnp.cumsum(g_count) - g_count
    tiles_g = (g_count + tm - 1) // tm
    tile_end = jnp.cumsum(tiles_g)
    tile_first = tile_end - tiles_g
    tile_ids = jnp.arange(n_tiles, dtype=i32)
    tile_group = jnp.minimum(jnp.sum((tile_ids[:, None] >= tile_end[None, :]).astype(i32), axis=1),
                             n_g - 1)
    k_in_group = tile_ids - tile_first[tile_group]
    tile_valid = jnp.clip(g_count[tile_group] - k_in_group * tm, 0, tm)
    tile_valid = jnp.where(tile_ids < tile_end[-1], tile_valid, 0)
    tile_start = jnp.where(tile_valid > 0, g_start[tile_group] + k_in_group * tm, 0)
    lo, hi = tile_start[:, None], (tile_start + tile_valid)[:, None]
    inter = (counts[None, :] > 0) & (c_start[None, :] < hi) & ((c_start + counts)[None, :] > lo)
    inter = inter.reshape(n_tiles, n_g, n_pair).any(axis=1)
    pair_has = jnp.array([[e in p for e in range(n_eg)] for p in EXPERT_PAIRS])
    need = (inter[:, :, None] & pair_has[None]).any(axis=1)
    tile_nexp = need.sum(axis=1)
    tile_exp = jnp.argsort(jnp.logical_not(need), axis=1, stable=True)
    return (tok_sorted, tile_start.astype(i32), tile_group.astype(i32), tile_valid.astype(i32),
            tile_nexp.astype(i32), tile_exp.reshape(-1).astype(i32))


def _moe_kernel(tok_ref, ts_ref, tg_ref, tv_ref, tn_ref, te_ref, hx_hbm, g_ref, w1_ref, w3_ref, w2_ref,
                gf_ref, out_hbm, xnext, osend, xcur, zn_ref, acc_ref, gsem, ssem):
    i = pl.program_id(0)
    n_blk, rows = xnext.shape[0] - 1, xnext.shape[1]
    tm = n_blk * rows
    d = osend.shape[2]
    n_eg = w1_ref.shape[0]

    def row_in(tok, blk, k):
        return pltpu.make_async_copy(hx_hbm.at[pl.ds(tok, 1), :], xnext.at[blk, pl.ds(k, 1), :], gsem)

    def row_out(tok, blk, k):
        return pltpu.make_async_copy(osend.at[blk, pl.ds(k, 1), :], out_hbm.at[pl.ds(tok, 1), :], ssem)

    def tile_in_wait():
        pltpu.make_async_copy(xnext.at[pl.ds(0, n_blk)], xnext.at[pl.ds(0, n_blk)], gsem).wait()

    def tile_out_wait():
        pltpu.make_async_copy(osend.at[pl.ds(0, n_blk)], osend.at[pl.ds(0, n_blk)], ssem).wait()

    def anchor(width):
        zero = xnext[n_blk, 0:1, 0:width]
        osend[n_blk, 0:1, 0:width] = zero
        return zero + osend[n_blk, 0:1, 0:width]

    def expert(zn, k, move_rows=None):
        e = te_ref[i * n_eg + k]
        if move_rows is not None:
            move_rows(0)
        a = jnp.dot(zn, w1_ref[e], preferred_element_type=F32)
        if move_rows is not None:
            a = a + anchor(a.shape[1])
            move_rows(1)
        b = jnp.dot(zn, w3_ref[e], preferred_element_type=F32)
        hid = (a * jax.nn.sigmoid(a) * b).astype(BF16)
        route = xcur[:, d:]
        lane = lax.broadcasted_iota(jnp.int32, route.shape, 1)
        comb = jnp.sum(jnp.where(lane == e, route, 0.0), axis=1, keepdims=True)
        out = comb * jnp.dot(hid, w2_ref[e], preferred_element_type=F32)
        if move_rows is not None:
            out = out + anchor(d)
        return out

    busy = tv_ref[i] > 0
    after_busy = tv_ref[jnp.maximum(i - 1, 0)] > 0
    drain = jnp.logical_and(jnp.logical_not(busy), jnp.logical_and(i >= 1, after_busy))

    @pl.when(i == 0)
    def _():
        xnext[n_blk] = jnp.zeros(xnext.shape[1:], F32)
        osend[...] = jnp.zeros(osend.shape, F32)
        base = ts_ref[0]

        def body(blk, c):
            for k in range(rows):
                row_in(tok_ref[base + blk * rows + k], blk, k).start()
            return c
        lax.fori_loop(0, n_blk, body, 0)

    @pl.when(jnp.logical_or(busy, drain))
    def _():
        tile_in_wait()

        @pl.when(i >= 1)
        def _():
            tile_out_wait()
            osend[0:n_blk] = acc_ref[...].reshape(n_blk, rows, d)
        xcur[...] = xnext[0:n_blk].reshape(tm, xnext.shape[2])

    n_always = 2

    @pl.when(busy)
    def _():
        h = xcur[:, :d]
        zn = _rms(h, g_ref[...]).astype(BF16)
        zn_ref[...] = zn
        acc = h
        nxt = ts_ref[i + 1]
        prv = ts_ref[jnp.maximum(i - 1, 0)]
        n_batch = 2 * n_always
        for k in range(n_always):
            def move_rows(part, k=k):
                q = 2 * k + part
                for r in range(q * tm // n_batch, (q + 1) * tm // n_batch):
                    row_in(tok_ref[nxt + r], r // rows, r % rows).start()
                    row_out(tok_ref[prv + r], r // rows, r % rows).start()
            acc = acc + expert(zn, k, move_rows)
        acc_ref[...] = acc

    for k in range(n_always, n_eg):
        @pl.when(tn_ref[i] > k)
        def _():
            acc_ref[...] += expert(zn_ref[...], k)

    @pl.when(busy)
    def _():
        acc_ref[...] = _rms(acc_ref[...], gf_ref[...])

    @pl.when(drain)
    def _():
        base = ts_ref[i - 1]
        nv = tv_ref[i - 1]
        full = nv // rows

        def body(blk, c):
            for k in range(rows):
                row_out(tok_ref[base + blk * rows + k], blk, k).start()
            return c
        lax.fori_loop(0, full, body, 0)
        for k in range(rows - 1):

            @pl.when(full * rows + k < nv)
            def _():
                row_out(tok_ref[base + full * rows + k], full, k).start()

        @pl.when(full > 0)
        def _():
            pltpu.make_async_copy(osend.at[pl.ds(0, full)], osend.at[pl.ds(0, full)], ssem).wait()
        for k in range(rows - 1):

            @pl.when(full * rows + k < nv)
            def _():
                row_out(0, 0, 0).wait()


def _moe(hx, g, w1, w3, w2, g_final, tm):
    t, dx = hx.shape
    d = dx - LANES
    n_eg = EXPERTS_PER_GROUP
    de = w1.shape[2]
    cls = hx[:, d + n_eg].astype(jnp.int32)
    tables = _dispatch_tables(cls, tm)
    n_tiles = tables[2].shape[0]

    def wmap(i, tok, ts, tg, tv, tn, te):
        return (tg[i], 0, 0)

    def cmap(i, tok, ts, tg, tv, tn, te):
        return (0, 0)

    grid_spec = pltpu.PrefetchScalarGridSpec(
        num_scalar_prefetch=len(tables),
        grid=(n_tiles,),
        in_specs=[pl.BlockSpec(memory_space=pl.ANY),
                  pl.BlockSpec((1, d), cmap),
                  pl.BlockSpec((n_eg, d, de), wmap),
                  pl.BlockSpec((n_eg, d, de), wmap),
                  pl.BlockSpec((n_eg, de, d), wmap),
                  pl.BlockSpec((1, d), cmap)],
        out_specs=pl.BlockSpec(memory_space=pl.ANY),
        scratch_shapes=[pltpu.VMEM((tm // SUBLANES + 1, SUBLANES, dx), F32),
                        pltpu.VMEM((tm // SUBLANES + 1, SUBLANES, d), F32),
                        pltpu.VMEM((tm, dx), F32),
                        pltpu.VMEM((tm, d), BF16),
                        pltpu.VMEM((tm, d), F32),
                        pltpu.SemaphoreType.DMA(()), pltpu.SemaphoreType.DMA(())])
    return pl.pallas_call(
        _moe_kernel,
        grid_spec=grid_spec,
        out_shape=jax.ShapeDtypeStruct((t, d), F32),
        compiler_params=pltpu.CompilerParams(dimension_semantics=("arbitrary",),
                                             vmem_limit_bytes=VMEM_LIMIT),
        name="moe",
    )(*tables, hx, g.reshape(1, d), w1.astype(BF16), w3.astype(BF16), w2.astype(BF16),
      g_final.reshape(1, d))


def _layer(h2, mem2, batch, seq, p, l, g_final, tm):
    d = h2.shape[1]
    tm_mix = min(MIXER_TILE, seq)
    n_chunks = tm_mix // SSM_CHUNK
    n_scan = int(math.log2(n_chunks))
    tables = _ssm_tables(p["ssm_a_re"][l], p["ssm_a_im"][l], p["ssm_log_dt"][l], p["ssm_b_re"][l],
                         p["ssm_b_im"][l], p["ssm_c_re"][l], p["ssm_c_im"][l], p["ssm_d"][l],
                         SSM_CHUNK, n_scan, n_chunks // SUBLANES)
    h2 = _mixer(h2, seq, p["norm_mix"][l], p["w_in"][l], p["pool_w"][l], p["pool_scale"][l], tables,
                p["glu_w"][l], p["glu_b"][l], p["br_pool"][l], p["br_ssm"][l], p["w_out"][l],
                tm_mix)

    m_len = mem2.shape[0] // batch
    hd = d // X_HEADS
    kv = _kv_proj(mem2, p["norm_mem"][l], p["x_wkv"][l])
    kt = kv[:, :d].reshape(batch, m_len, X_HEADS, hd).transpose(0, 2, 3, 1).astype(BF16)
    v = kv[:, d:].reshape(batch, m_len, X_HEADS, hd).transpose(0, 2, 1, 3).astype(BF16)

    n_g, n_eg = N_EXPERT_GROUPS, EXPERTS_PER_GROUP
    w_r = jnp.concatenate([p["router_g_w"][l], p["router_e_w"][l].reshape(d, n_g * n_eg)], axis=1)
    w_r = jnp.pad(w_r, ((0, 0), (0, LANES - w_r.shape[1])))
    b_r = jnp.concatenate([p["router_g_b"][l], p["router_e_b"][l].reshape(n_g * n_eg)])
    b_r = jnp.pad(b_r, (0, LANES - b_r.shape[0])).reshape(1, LANES)
    hx = _xattn(h2, seq, p["norm_x"][l], p["x_wq"][l], kt, v, p["x_wo"][l], p["norm_ffn"][l],
                w_r, b_r, min(XATTN_TILE, seq))
    return _moe(hx, p["norm_ffn"][l], p["exp_w1"][l], p["exp_w3"][l], p["exp_w2"][l], g_final, tm)


def kernel(x, mem, norm_mix, w_in, pool_w, pool_scale, ssm_a_re, ssm_a_im, ssm_log_dt, ssm_b_re,
           ssm_b_im, ssm_c_re, ssm_c_im, ssm_d, glu_w, glu_b, br_pool, br_ssm, w_out, norm_x,
           norm_mem, x_wq, x_wkv, x_wo, norm_ffn, router_g_w, router_g_b, router_e_w, router_e_b,
           exp_w1, exp_w3, exp_w2, norm_final):
    p = dict(norm_mix=norm_mix, w_in=w_in, pool_w=pool_w, pool_scale=pool_scale, ssm_a_re=ssm_a_re,
             ssm_a_im=ssm_a_im, ssm_log_dt=ssm_log_dt, ssm_b_re=ssm_b_re, ssm_b_im=ssm_b_im,
             ssm_c_re=ssm_c_re, ssm_c_im=ssm_c_im, ssm_d=ssm_d, glu_w=glu_w, glu_b=glu_b,
             br_pool=br_pool, br_ssm=br_ssm, w_out=w_out, norm_x=norm_x, norm_mem=norm_mem,
             x_wq=x_wq, x_wkv=x_wkv, x_wo=x_wo, norm_ffn=norm_ffn, router_g_w=router_g_w,
             router_g_b=router_g_b, router_e_w=router_e_w, router_e_b=router_e_b, exp_w1=exp_w1,
             exp_w3=exp_w3, exp_w2=exp_w2)
    batch, seq, d = x.shape
    assert norm_mix.shape[0] == 1, "final norm is fused into the (single) layer's last stage"
    tm = min(512, seq)
    h2 = x.reshape(batch * seq, d)
    mem2 = mem.reshape(-1, d)
    out = _layer(h2, mem2, batch, seq, p, 0, norm_final, tm)
    return out.reshape(batch, seq, d)
```

```python
import functools
import math

import jax
import jax.numpy as jnp
from jax import lax
from jax.experimental import pallas as pl
from jax.experimental.pallas import tpu as pltpu

F32 = jnp.float32
BF16 = jnp.bfloat16

RMS_EPS = 1e-6
POOL_WINDOWS = (2, 4, 8, 16)
POOL_HALO = 16
SSM_GROUP_DIM = 16
SSM_STATE = 64
LANES = 128
SUBLANES = 8
OCT = LANES // SSM_GROUP_DIM
SSM_CHUNK = 4
STRAND_PAD = 8
X_HEADS = 4
MIXER_TILE = 1024
XATTN_TILE = 1024
N_EXPERT_GROUPS = 4
EXPERTS_PER_GROUP = 4
EXPERT_PAIRS = ((0, 1), (0, 2), (0, 3), (1, 3), (1, 2), (2, 3))
N_CLASS_ROWS = 32
VMEM_LIMIT = 56 * 1024 * 1024


def _rms(x, g):
    return x * lax.rsqrt(jnp.mean(x * x, axis=-1, keepdims=True) + RMS_EPS) * g


def _const_spec(shape):
    nd = len(shape)
    return pl.BlockSpec(shape, lambda *_: (0,) * nd, pipeline_mode=pl.Buffered(1))


def _ssm_tables(a_re, a_im, log_dt, b_re, b_im, c_re, c_im, d_skip, chunk, n_scan, n_pow):
    hi = lax.Precision.HIGHEST
    G, N = a_re.shape
    C = SSM_GROUP_DIM
    Q = G // OCT
    dt = jnp.exp(log_dt)[:, None]
    mag = jnp.exp(a_re * dt)
    lb_re, lb_im = mag * jnp.cos(a_im * dt), mag * jnp.sin(a_im * dt)
    den = a_re * a_re + a_im * a_im
    nr, ni = lb_re - 1.0, lb_im
    f_re = ((nr * a_re + ni * a_im) / den)[..., None]
    f_im = ((ni * a_re - nr * a_im) / den)[..., None]
    bb_re = f_re * b_re - f_im * b_im
    bb_im = f_re * b_im + f_im * b_re

    pw_re, pw_im = [jnp.ones_like(lb_re)], [jnp.zeros_like(lb_re)]
    for _ in range(chunk):
        r, i = pw_re[-1], pw_im[-1]
        pw_re.append(r * lb_re - i * lb_im)
        pw_im.append(r * lb_im + i * lb_re)
    pw_re, pw_im = jnp.stack(pw_re), jnp.stack(pw_im)

    lbb_re = pw_re[:chunk, :, :, None] * bb_re - pw_im[:chunk, :, :, None] * bb_im
    lbb_im = pw_re[:chunk, :, :, None] * bb_im + pw_im[:chunk, :, :, None] * bb_re
    kern = (jnp.einsum('gon,dgnc->dgoc', c_re, lbb_re, precision=hi)
            - jnp.einsum('gon,dgnc->dgoc', c_im, lbb_im, precision=hi))
    kern = kern.at[0].add(d_skip.reshape(G, C)[:, :, None] * jnp.eye(C, dtype=F32))

    lane_g = jnp.arange(LANES) // C
    state_g = jnp.arange(OCT * N) // N
    rep_c = (jnp.arange(C)[:, None] == (jnp.arange(LANES) % C)[None, :]).astype(F32)
    rep_n = (jnp.arange(N)[:, None] == (jnp.arange(OCT * N) % N)[None, :]).astype(F32)
    same_cc = (lane_g[:, None] == lane_g[None, :]).astype(F32)
    same_cn = (lane_g[:, None] == state_g[None, :]).astype(F32)

    kt = kern.transpose(0, 1, 3, 2).reshape(chunk, Q, LANES, C)
    blk = jnp.einsum('dqrc,cl->dqrl', kt, rep_c, precision=hi) * same_cc
    zero = jnp.zeros((Q, LANES, LANES), F32)
    m_intra = jnp.concatenate(
        [jnp.concatenate([blk[jp - j] if jp >= j else zero for jp in range(chunk)], axis=2)
         for j in range(chunk)], axis=1)

    def state_half(lbb):
        z = lbb[::-1].transpose(0, 1, 3, 2).reshape(chunk, Q, LANES, N)
        w = jnp.einsum('jqrn,nl->qjrl', z, rep_n, precision=hi) * same_cn
        return w.reshape(Q, chunk * LANES, OCT * N)
    w_state = jnp.concatenate([state_half(lbb_re), state_half(lbb_im)], axis=-1)

    p_re = c_re[None] * pw_re[1:, :, None, :] - c_im[None] * pw_im[1:, :, None, :]
    p_im = c_re[None] * pw_im[1:, :, None, :] + c_im[None] * pw_re[1:, :, None, :]
    def out_half(p):
        z = p.transpose(0, 1, 3, 2).reshape(chunk, Q, OCT * N, C)
        w = jnp.einsum('jqrc,cl->qrjl', z, rep_c, precision=hi)
        w = w * same_cn.T[None, :, None, :]
        return w.reshape(Q, OCT * N, chunk * LANES)
    w_out = jnp.concatenate([out_half(p_re), out_half(-p_im)], axis=1)

    lr, li = [pw_re[chunk]], [pw_im[chunk]]
    for _ in range(n_scan - 1):
        r, i = lr[-1], li[-1]
        lr.append(r * r - i * i)
        li.append(2.0 * r * i)
    qr, qi = [jnp.ones_like(lb_re)], [jnp.zeros_like(lb_re)]
    for _ in range(n_pow - 1):
        r, i = qr[-1], qi[-1]
        qr.append(r * pw_re[chunk] - i * pw_im[chunk])
        qi.append(r * pw_im[chunk] + i * pw_re[chunk])
    lam_re = jnp.stack(lr + qr).reshape(n_scan + n_pow, G * N)
    lam_im = jnp.stack(li + qi).reshape(n_scan + n_pow, G * N)
    return m_intra.astype(BF16), w_state.astype(BF16), w_out.astype(BF16), lam_re, lam_im


def _mixer_kernel(x_ref, nm_ref, win_ref, poolw_ref, pscale_ref, mi_ref, ws_ref, wh_ref,
                  lre_ref, lim_ref, gluw_ref, glub_ref, brp_ref, brs_ref, wout_ref,
                  o_ref, zext_ref, v_ref, y_ref, carry_ref, *, tiles_per_seq, chunk):
    tm, d_model = x_ref.shape
    pool_w = zext_ref.shape[1]
    n_oct = v_ref.shape[0]
    ssm_w = n_oct * LANES
    n_chunks = tm // chunk
    n_str = SUBLANES
    per = n_chunks // n_str
    span = per * chunk
    pitch = v_ref.shape[1] // n_str
    n_scan = lre_ref.shape[0] - per
    seq_tile = pl.program_id(0) % tiles_per_seq

    @pl.when(seq_tile == 0)
    def _():
        zext_ref[0:POOL_HALO, :] = jnp.zeros((POOL_HALO, pool_w), F32)
        carry_ref[...] = jnp.zeros_like(carry_ref)

    x = x_ref[...]
    u = _rms(x, nm_ref[...]).astype(BF16)
    proj = jnp.dot(u, win_ref[...], preferred_element_type=F32)
    z = proj[:, :pool_w]
    zext_ref[POOL_HALO:, :] = z
    for q in range(n_oct):
        for r in range(n_str):
            v_ref[q, r * pitch:r * pitch + span, :] = (
                proj[r * span:(r + 1) * span, pool_w + q * LANES:pool_w + (q + 1) * LANES])
    g_pool = jax.nn.sigmoid(proj[:, pool_w + ssm_w:pool_w + ssm_w + d_model])
    g_ssm = jax.nn.sigmoid(proj[:, pool_w + ssm_w + d_model:])

    pos = (seq_tile * tm + lax.broadcasted_iota(jnp.int32, (tm, 1), 0) + 1).astype(F32)
    gdim = pool_w // len(POOL_WINDOWS)
    ypool = []
    for gi, w in enumerate(POOL_WINDOWS):
        cols = slice(gi * gdim, (gi + 1) * gdim)
        acc = z[:, cols]
        for dlt in range(1, w):
            acc = acc + zext_ref[POOL_HALO - dlt:POOL_HALO - dlt + tm, cols]
        diff = acc / jnp.minimum(pos, float(w)) - z[:, cols]
        ypool.append(jnp.dot(diff.astype(BF16), poolw_ref[gi], preferred_element_type=F32))
    ypool = jnp.concatenate(ypool, axis=1) * pscale_ref[...]
    zext_ref[0:POOL_HALO, :] = zext_ref[tm:tm + POOL_HALO, :]
    p_br = jnp.dot(ypool.astype(BF16), brp_ref[...], preferred_element_type=F32)

    srow = lax.broadcasted_iota(jnp.int32, (n_str, 1), 0)
    half = OCT * SSM_STATE
    log_per = per.bit_length() - 1
    for q in range(n_oct):
        st = slice(q * half, (q + 1) * half)

        def lam(k):
            return lre_ref[k:k + 1, st], lim_ref[k:k + 1, st]

        vq = jnp.concatenate(
            [jnp.concatenate([v_ref[q, pl.ds(b * chunk + j, n_str, stride=pitch), :]
                              for j in range(chunk)], axis=1) for b in range(per)],
            axis=0).astype(BF16)
        s = jnp.dot(vq, ws_ref[q], preferred_element_type=F32)
        a_re, a_im = lam(0)
        loc_re, loc_im = [s[0:n_str, :half]], [s[0:n_str, half:]]
        for b in range(1, per):
            p_re, p_im = loc_re[-1], loc_im[-1]
            loc_re.append(s[b * n_str:(b + 1) * n_str, :half] + a_re * p_re - a_im * p_im)
            loc_im.append(s[b * n_str:(b + 1) * n_str, half:] + a_re * p_im + a_im * p_re)
        c_re, c_im = carry_ref[0:1, st], carry_ref[1:2, st]
        e_re, e_im = loc_re[-1], loc_im[-1]
        l_re, l_im = lam(log_per)
        e_re = e_re + jnp.where(srow == 0, l_re * c_re - l_im * c_im, 0.0)
        e_im = e_im + jnp.where(srow == 0, l_re * c_im + l_im * c_re, 0.0)
        for k in range(n_str.bit_length() - 1):
            sh = 1 << k
            r_re = jnp.where(srow >= sh, pltpu.roll(e_re, sh, axis=0), 0.0)
            r_im = jnp.where(srow >= sh, pltpu.roll(e_im, sh, axis=0), 0.0)
            l_re, l_im = lam(log_per + k)
            e_re, e_im = e_re + l_re * r_re - l_im * r_im, e_im + l_re * r_im + l_im * r_re
        carry_ref[0:1, st] = e_re[n_str - 1:n_str, :]
        carry_ref[1:2, st] = e_im[n_str - 1:n_str, :]
        i_re = jnp.where(srow == 0, c_re, pltpu.roll(e_re, 1, axis=0))
        i_im = jnp.where(srow == 0, c_im, pltpu.roll(e_im, 1, axis=0))
        h0_re, h0_im = [i_re], [i_im]
        for b in range(1, per):
            l_re, l_im = lam(n_scan + b)
            h0_re.append(loc_re[b - 1] + l_re * i_re - l_im * i_im)
            h0_im.append(loc_im[b - 1] + l_re * i_im + l_im * i_re)
        h0_re = jnp.concatenate(h0_re, axis=0).astype(BF16)
        h0_im = jnp.concatenate(h0_im, axis=0).astype(BF16)
        yq = (jnp.dot(vq, mi_ref[q], preferred_element_type=F32)
              + jnp.dot(h0_re, wh_ref[q, :half, :], preferred_element_type=F32)
              + jnp.dot(h0_im, wh_ref[q, half:, :], preferred_element_type=F32))
        for b in range(per):
            for j in range(chunk):
                y_ref[q, pl.ds(b * chunk + j, n_str, stride=pitch), :] = (
                    yq[b * n_str:(b + 1) * n_str, j * LANES:(j + 1) * LANES])

    y = jnp.concatenate(
        [jnp.concatenate([y_ref[q, r * pitch:r * pitch + span, :] for r in range(n_str)], axis=0)
         for q in range(n_oct)], axis=1)
    ys = jax.nn.gelu(y, approximate=True)
    glu = jnp.dot(ys.astype(BF16), gluw_ref[...], preferred_element_type=F32) + glub_ref[...]
    ys = ys * jax.nn.sigmoid(glu)
    s_br = jnp.dot(ys.astype(BF16), brs_ref[...], preferred_element_type=F32)

    merged = g_pool * p_br + g_ssm * s_br
    o_ref[...] = x + jnp.dot(merged.astype(BF16), wout_ref[...], preferred_element_type=F32)


def _mixer(x2, seq, nm, w_in, pool_w, pool_scale, tables, glu_w, glu_b, br_pool, br_ssm, w_out, tm):
    t, d = x2.shape
    mi, ws, wh, lre, lim = tables
    pool_width = pool_w.shape[0] * pool_w.shape[1]
    ssm_width = glu_w.shape[0]
    consts = [nm.reshape(1, d), w_in.astype(BF16), pool_w.astype(BF16),
              pool_scale.reshape(1, pool_width), mi, ws, wh, lre, lim,
              glu_w.astype(BF16), glu_b.reshape(1, ssm_width), br_pool.astype(BF16),
              br_ssm.astype(BF16), w_out.astype(BF16)]
    kern = functools.partial(_mixer_kernel, tiles_per_seq=seq // tm, chunk=SSM_CHUNK)
    return pl.pallas_call(
        kern,
        grid=(t // tm,),
        in_specs=[pl.BlockSpec((tm, d), lambda i: (i, 0))] + [_const_spec(c.shape) for c in consts],
        out_specs=pl.BlockSpec((tm, d), lambda i: (i, 0)),
        out_shape=jax.ShapeDtypeStruct((t, d), F32),
        scratch_shapes=[pltpu.VMEM((POOL_HALO + tm, pool_width), F32),
                        pltpu.VMEM((ssm_width // LANES, tm + SUBLANES * STRAND_PAD, LANES), F32),
                        pltpu.VMEM((ssm_width // LANES, tm + SUBLANES * STRAND_PAD, LANES), F32),
                        pltpu.VMEM((2, lre.shape[1]), F32)],
        compiler_params=pltpu.CompilerParams(dimension_semantics=("arbitrary",),
                                             vmem_limit_bytes=VMEM_LIMIT),
        name="mixer",
    )(x2, *consts)


def _kv_fold_kernel(mem_ref, g_ref, wkv_ref, wq_ref, wo_ref, wqk_ref, wvo_ref):
    m, d = mem_ref.shape
    hd = d // X_HEADS
    mn = _rms(mem_ref[...], g_ref[...]).astype(BF16)
    kv = jnp.dot(mn, wkv_ref[...], preferred_element_type=F32)
    for h in range(X_HEADS):
        k_h = kv[:, h * hd:(h + 1) * hd].astype(BF16)
        v_h = kv[:, d + h * hd:d + (h + 1) * hd].astype(BF16)
        qk = lax.dot_general(wq_ref[:, h * hd:(h + 1) * hd], k_h, (((1,), (1,)), ((), ())),
                             preferred_element_type=F32)
        wqk_ref[0, :, h * m:(h + 1) * m] = qk.astype(BF16)
        wvo_ref[0, h * m:(h + 1) * m, :] = jnp.dot(
            v_h, wo_ref[h * hd:(h + 1) * hd, :], preferred_element_type=F32).astype(BF16)


def _kv_fold(mem2, batch, g, w_kv, w_q, w_o):
    d = mem2.shape[1]
    m = mem2.shape[0] // batch
    hm = X_HEADS * m
    return pl.pallas_call(
        _kv_fold_kernel,
        grid=(batch,),
        in_specs=[pl.BlockSpec((m, d), lambda b: (b, 0)), _const_spec((1, d)),
                  _const_spec(w_kv.shape), _const_spec(w_q.shape), _const_spec(w_o.shape)],
        out_specs=[pl.BlockSpec((1, d, hm), lambda b: (b, 0, 0)),
                   pl.BlockSpec((1, hm, d), lambda b: (b, 0, 0))],
        out_shape=[jax.ShapeDtypeStruct((batch, d, hm), BF16),
                   jax.ShapeDtypeStruct((batch, hm, d), BF16)],
        compiler_params=pltpu.CompilerParams(dimension_semantics=("arbitrary",),
                                             vmem_limit_bytes=VMEM_LIMIT),
        name="kv_fold",
    )(mem2, g.reshape(1, d), w_kv.astype(BF16), w_q.astype(BF16), w_o.astype(BF16))


def _route(logits):
    n_g, n_eg = N_EXPERT_GROUPS, EXPERTS_PER_GROUP
    tm = logits.shape[0]
    lt = logits.T
    neg = jnp.float32(-jnp.inf)
    big = jnp.int32(1 << 20)

    def first_max(v):
        row = lax.broadcasted_iota(jnp.int32, v.shape, 0)
        m = jnp.max(v, axis=0, keepdims=True)
        return m, jnp.min(jnp.where(v == m, row, big), axis=0, keepdims=True)

    gl = lt[0:n_g]
    gmax, g_idx = first_max(gl)
    g_gate = 1.0 / jnp.sum(jnp.exp(gl - gmax), axis=0, keepdims=True)
    el = lt[n_g:n_g + n_eg]
    for g in range(1, n_g):
        el = jnp.where(g_idx == g, lt[n_g + g * n_eg:n_g + (g + 1) * n_eg], el)
    row = lax.broadcasted_iota(jnp.int32, el.shape, 0)
    t1, i1 = first_max(el)
    t2, i2 = first_max(jnp.where(row == i1, neg, el))
    e2 = jnp.exp(t2 - t1)
    w1 = 1.0 / (1.0 + e2)
    w2 = e2 / (1.0 + e2)
    comb = jnp.where(row == i1, w1, jnp.where(row == i2, w2, 0.0)) * g_gate
    lo, hi = jnp.minimum(i1, i2), jnp.maximum(i1, i2)
    pair = jnp.zeros_like(lo)
    for idx, (e_lo, e_hi) in enumerate(EXPERT_PAIRS):
        pair = jnp.where((lo == e_lo) & (hi == e_hi), idx, pair)
    cls = g_idx * len(EXPERT_PAIRS) + pair
    info = jnp.concatenate([comb, jnp.zeros((LANES - n_eg, tm), F32)], axis=0)
    hist = jnp.sum((cls == lax.broadcasted_iota(jnp.int32, (N_CLASS_ROWS, tm), 0)).astype(F32),
                   axis=1, keepdims=True)
    return info.T, cls, hist


def _dot_bf16x3(x, w_hilo):
    n = w_hilo.shape[1] // 2
    x_hi = x.astype(BF16)
    x_lo = (x - x_hi.astype(F32)).astype(BF16)
    both = jnp.dot(x_hi, w_hilo, preferred_element_type=F32)
    return both[:, :n] + both[:, n:] + jnp.dot(x_lo, w_hilo[:, :n], preferred_element_type=F32)


def _xattn_kernel(h_ref, g_ref, wqk_ref, wvo_ref, gffn_ref, wr_ref, br_ref, o_ref, cls_ref, hist_ref,
                  *, scale):
    h = h_ref[...]
    d = h.shape[1]
    m = wqk_ref.shape[2] // X_HEADS
    hn = _rms(h, g_ref[...]).astype(BF16)
    s_all = jnp.dot(hn, wqk_ref[0], preferred_element_type=F32) * scale
    probs = []
    for hh in range(X_HEADS):
        s = s_all[:, hh * m:(hh + 1) * m]
        s = s - jnp.max(s, axis=-1, keepdims=True)
        p = jnp.exp(s)
        probs.append((p / jnp.sum(p, axis=-1, keepdims=True)).astype(BF16))
    h_out = h + jnp.dot(jnp.concatenate(probs, axis=1), wvo_ref[0], preferred_element_type=F32)
    zn = _rms(h_out, gffn_ref[...])
    logits = _dot_bf16x3(zn, wr_ref[...]) + br_ref[...]
    route, cls, hist = _route(logits)
    o_ref[:, :d] = h_out
    o_ref[:, d:] = route
    cls_ref[0] = cls
    hist_ref[0] = jnp.broadcast_to(hist, hist_ref.shape[1:])


def _xattn(h2, seq, g, wqk, wvo, g_ffn, w_r, b_r, tm):
    t, d = h2.shape
    tiles_per_seq = seq // tm
    n_tiles = t // tm
    kern = functools.partial(_xattn_kernel, scale=(d // X_HEADS) ** -0.5)
    w_r_hi = w_r.astype(BF16)
    w_r_lo = (w_r - w_r_hi.astype(F32)).astype(BF16)
    w_r_hilo = jnp.concatenate([w_r_hi, w_r_lo], axis=1)
    hx, cls, hist = pl.pallas_call(
        kern,
        grid=(n_tiles,),
        in_specs=[pl.BlockSpec((tm, d), lambda i: (i, 0)), _const_spec((1, d)),
                  pl.BlockSpec((1,) + wqk.shape[1:], lambda i: (i // tiles_per_seq, 0, 0)),
                  pl.BlockSpec((1,) + wvo.shape[1:], lambda i: (i // tiles_per_seq, 0, 0)),
                  _const_spec((1, d)), _const_spec(w_r_hilo.shape), _const_spec(b_r.shape)],
        out_specs=[pl.BlockSpec((tm, d + LANES), lambda i: (i, 0)),
                   pl.BlockSpec((1, 1, tm), lambda i: (i, 0, 0)),
                   pl.BlockSpec((1, N_CLASS_ROWS, LANES), lambda i: (i, 0, 0))],
        out_shape=[jax.ShapeDtypeStruct((t, d + LANES), F32),
                   jax.ShapeDtypeStruct((n_tiles, 1, tm), jnp.int32),
                   jax.ShapeDtypeStruct((n_tiles, N_CLASS_ROWS, LANES), F32)],
        compiler_params=pltpu.CompilerParams(dimension_semantics=("arbitrary",),
                                             vmem_limit_bytes=VMEM_LIMIT),
        name="xattn",
    )(h2, g.reshape(1, d), wqk, wvo, g_ffn.reshape(1, d), w_r_hilo, b_r)
    return hx, cls.reshape(t), jnp.sum(hist[:, :, 0], axis=0).astype(jnp.int32)


def _dispatch_tables(cls, counts, tm):
    t = cls.shape[0]
    n_g, n_eg, n_pair = N_EXPERT_GROUPS, EXPERTS_PER_GROUP, len(EXPERT_PAIRS)
    n_tiles = t // tm + n_g
    i32 = jnp.int32
    _, tok_sorted = lax.sort((cls, jnp.arange(t, dtype=i32)), num_keys=1, is_stable=True)
    tok_sorted = jnp.concatenate([tok_sorted, jnp.zeros((tm,), i32)])
    counts = counts[:n_g * n_pair]
    c_start = jnp.cumsum(counts) - counts
    g_count = counts.reshape(n_g, n_pair).sum(axis=1)
    g_start = jnp.cumsum(g_count) - g_count
    tiles_g = (g_count + tm - 1) // tm
    tile_end = jnp.cumsum(tiles_g)
    tile_first = tile_end - tiles_g
    tile_ids = jnp.arange(n_tiles, dtype=i32)
    tile_group = jnp.minimum(jnp.sum((tile_ids[:, None] >= tile_end[None, :]).astype(i32), axis=1),
                             n_g - 1)
    k_in_group = tile_ids - tile_first[tile_group]
    tile_valid = jnp.clip(g_count[tile_group] - k_in_group * tm, 0, tm)
    tile_valid = jnp.where(tile_ids < tile_end[-1], tile_valid, 0)
    tile_start = jnp.where(tile_valid > 0, g_start[tile_group] + k_in_group * tm, 0)
    lo, hi = tile_start[:, None], (tile_start + tile_valid)[:, None]
    inter = (counts[None, :] > 0) & (c_start[None, :] < hi) & ((c_start + counts)[None, :] > lo)
    inter = inter.reshape(n_tiles, n_g, n_pair).any(axis=1)
    pair_has = jnp.array([[e in p for e in range(n_eg)] for p in EXPERT_PAIRS])
    need = (inter[:, :, None] & pair_has[None]).any(axis=1)
    tile_nexp = need.sum(axis=1)
    tile_exp = jnp.argsort(jnp.logical_not(need), axis=1, stable=True)
    return (tok_sorted, tile_start.astype(i32), tile_group.astype(i32), tile_valid.astype(i32),
            tile_nexp.astype(i32), tile_exp.reshape(-1).astype(i32))


def _moe_kernel(tok_ref, ts_ref, tg_ref, tv_ref, tn_ref, te_ref, hx_hbm, g_ref, w1_ref, w3_ref, w2_ref,
                gf_ref, out_hbm, xnext, osend, xcur, zn_ref, acc_ref, gsem, ssem):
    i = pl.program_id(0)
    n_blk, rows = xnext.shape[0] - 1, xnext.shape[1]
    tm = n_blk * rows
    d = osend.shape[2]
    n_eg = w1_ref.shape[0]

    def row_in(tok, blk, k):
        return pltpu.make_async_copy(hx_hbm.at[pl.ds(tok, 1), :], xnext.at[blk, pl.ds(k, 1), :], gsem)

    def row_out(tok, blk, k):
        return pltpu.make_async_copy(osend.at[blk, pl.ds(k, 1), :], out_hbm.at[pl.ds(tok, 1), :], ssem)

    def tile_in_wait():
        pltpu.make_async_copy(xnext.at[pl.ds(0, n_blk)], xnext.at[pl.ds(0, n_blk)], gsem).wait()

    def tile_out_wait():
        pltpu.make_async_copy(osend.at[pl.ds(0, n_blk)], osend.at[pl.ds(0, n_blk)], ssem).wait()

    def anchor(width):
        zero = xnext[n_blk, 0:1, 0:width]
        osend[n_blk, 0:1, 0:width] = zero
        return zero + osend[n_blk, 0:1, 0:width]

    def expert(zn, k, move_rows=None):
        e = te_ref[i * n_eg + k]
        if move_rows is not None:
            move_rows(0)
        a = jnp.dot(zn, w1_ref[e], preferred_element_type=F32)
        if move_rows is not None:
            a = a + anchor(a.shape[1])
            move_rows(1)
        b = jnp.dot(zn, w3_ref[e], preferred_element_type=F32)
        hid = (a * jax.nn.sigmoid(a) * b).astype(BF16)
        route = xcur[:, d:]
        lane = lax.broadcasted_iota(jnp.int32, route.shape, 1)
        comb = jnp.sum(jnp.where(lane == e, route, 0.0), axis=1, keepdims=True)
        out = comb * jnp.dot(hid, w2_ref[e], preferred_element_type=F32)
        if move_rows is not None:
            out = out + anchor(d)
        return out

    busy = tv_ref[i] > 0
    after_busy = tv_ref[jnp.maximum(i - 1, 0)] > 0
    drain = jnp.logical_and(jnp.logical_not(busy), jnp.logical_and(i >= 1, after_busy))

    @pl.when(i == 0)
    def _():
        xnext[n_blk] = jnp.zeros(xnext.shape[1:], F32)
        osend[...] = jnp.zeros(osend.shape, F32)
        base = ts_ref[0]

        def body(blk, c):
            for k in range(rows):
                row_in(tok_ref[base + blk * rows + k], blk, k).start()
            return c
        lax.fori_loop(0, n_blk, body, 0)

    @pl.when(jnp.logical_or(busy, drain))
    def _():
        tile_in_wait()

        @pl.when(i >= 1)
        def _():
            tile_out_wait()
            osend[0:n_blk] = acc_ref[...].reshape(n_blk, rows, d)
        xcur[...] = xnext[0:n_blk].reshape(tm, xnext.shape[2])

    n_always = 2

    @pl.when(busy)
    def _():
        h = xcur[:, :d]
        zn = _rms(h, g_ref[...]).astype(BF16)
        zn_ref[...] = zn
        acc = h
        nxt = ts_ref[i + 1]
        prv = ts_ref[jnp.maximum(i - 1, 0)]
        n_batch = 2 * n_always
        for k in range(n_always):
            def move_rows(part, k=k):
                q = 2 * k + part
                for r in range(q * tm // n_batch, (q + 1) * tm // n_batch):
                    row_in(tok_ref[nxt + r], r // rows, r % rows).start()
                    row_out(tok_ref[prv + r], r // rows, r % rows).start()
            acc = acc + expert(zn, k, move_rows)
        acc_ref[...] = acc

    for k in range(n_always, n_eg):
        @pl.when(tn_ref[i] > k)
        def _():
            acc_ref[...] += expert(zn_ref[...], k)

    @pl.when(busy)
    def _():
        acc_ref[...] = _rms(acc_ref[...], gf_ref[...])

    @pl.when(drain)
    def _():
        base = ts_ref[i - 1]
        nv = tv_ref[i - 1]
        full = nv // rows

        def body(blk, c):
            for k in range(rows):
                row_out(tok_ref[base + blk * rows + k], blk, k).start()
            return c
        lax.fori_loop(0, full, body, 0)
        for k in range(rows - 1):

            @pl.when(full * rows + k < nv)
            def _():
                row_out(tok_ref[base + full * rows + k], full, k).start()

        @pl.when(full > 0)
        def _():
            pltpu.make_async_copy(osend.at[pl.ds(0, full)], osend.at[pl.ds(0, full)], ssem).wait()
        for k in range(rows - 1):

            @pl.when(full * rows + k < nv)
            def _():
                row_out(0, 0, 0).wait()


def _moe(hx, cls, counts, g, w1, w3, w2, g_final, tm):
    t, dx = hx.shape
    d = dx - LANES
    n_eg = EXPERTS_PER_GROUP
    de = w1.shape[2]
    tables = _dispatch_tables(cls, counts, tm)
    n_tiles = tables[2].shape[0]

    def wmap(i, tok, ts, tg, tv, tn, te):
        return (tg[i], 0, 0)

    def cmap(i, tok, ts, tg, tv, tn, te):
        return (0, 0)

    grid_spec = pltpu.PrefetchScalarGridSpec(
        num_scalar_prefetch=len(tables),
        grid=(n_tiles,),
        in_specs=[pl.BlockSpec(memory_space=pl.ANY),
                  pl.BlockSpec((1, d), cmap),
                  pl.BlockSpec((n_eg, d, de), wmap),
                  pl.BlockSpec((n_eg, d, de), wmap),
                  pl.BlockSpec((n_eg, de, d), wmap),
                  pl.BlockSpec((1, d), cmap)],
        out_specs=pl.BlockSpec(memory_space=pl.ANY),
        scratch_shapes=[pltpu.VMEM((tm // SUBLANES + 1, SUBLANES, dx), F32),
                        pltpu.VMEM((tm // SUBLANES + 1, SUBLANES, d), F32),
                        pltpu.VMEM((tm, dx), F32),
                        pltpu.VMEM((tm, d), BF16),
                        pltpu.VMEM((tm, d), F32),
                        pltpu.SemaphoreType.DMA(()), pltpu.SemaphoreType.DMA(())])
    return pl.pallas_call(
        _moe_kernel,
        grid_spec=grid_spec,
        out_shape=jax.ShapeDtypeStruct((t, d), F32),
        compiler_params=pltpu.CompilerParams(dimension_semantics=("arbitrary",),
                                             vmem_limit_bytes=VMEM_LIMIT),
        name="moe",
    )(*tables, hx, g.reshape(1, d), w1.astype(BF16), w3.astype(BF16), w2.astype(BF16),
      g_final.reshape(1, d))


def _layer(h2, mem2, batch, seq, p, l, g_final, tm):
    d = h2.shape[1]
    tm_mix = min(MIXER_TILE, seq)
    n_chunks = tm_mix // SSM_CHUNK
    n_scan = int(math.log2(n_chunks))
    tables = _ssm_tables(p["ssm_a_re"][l], p["ssm_a_im"][l], p["ssm_log_dt"][l], p["ssm_b_re"][l],
                         p["ssm_b_im"][l], p["ssm_c_re"][l], p["ssm_c_im"][l], p["ssm_d"][l],
                         SSM_CHUNK, n_scan, n_chunks // SUBLANES)
    h2 = _mixer(h2, seq, p["norm_mix"][l], p["w_in"][l], p["pool_w"][l], p["pool_scale"][l], tables,
                p["glu_w"][l], p["glu_b"][l], p["br_pool"][l], p["br_ssm"][l], p["w_out"][l],
                tm_mix)

    wqk, wvo = _kv_fold(mem2, batch, p["norm_mem"][l], p["x_wkv"][l], p["x_wq"][l], p["x_wo"][l])

    n_g, n_eg = N_EXPERT_GROUPS, EXPERTS_PER_GROUP
    w_r = jnp.concatenate([p["router_g_w"][l], p["router_e_w"][l].reshape(d, n_g * n_eg)], axis=1)
    w_r = jnp.pad(w_r, ((0, 0), (0, LANES - w_r.shape[1])))
    b_r = jnp.concatenate([p["router_g_b"][l], p["router_e_b"][l].reshape(n_g * n_eg)])
    b_r = jnp.pad(b_r, (0, LANES - b_r.shape[0])).reshape(1, LANES)
    hx, cls, counts = _xattn(h2, seq, p["norm_x"][l], wqk, wvo, p["norm_ffn"][l], w_r, b_r,
                             min(XATTN_TILE, seq))
    return _moe(hx, cls, counts, p["norm_ffn"][l], p["exp_w1"][l], p["exp_w3"][l], p["exp_w2"][l], g_final, tm)


def kernel(x, mem, norm_mix, w_in, pool_w, pool_scale, ssm_a_re, ssm_a_im, ssm_log_dt, ssm_b_re,
           ssm_b_im, ssm_c_re, ssm_c_im, ssm_d, glu_w, glu_b, br_pool, br_ssm, w_out, norm_x,
           norm_mem, x_wq, x_wkv, x_wo, norm_ffn, router_g_w, router_g_b, router_e_w, router_e_b,
           exp_w1, exp_w3, exp_w2, norm_final):
    p = dict(norm_mix=norm_mix, w_in=w_in, pool_w=pool_w, pool_scale=pool_scale, ssm_a_re=ssm_a_re,
             ssm_a_im=ssm_a_im, ssm_log_dt=ssm_log_dt, ssm_b_re=ssm_b_re, ssm_b_im=ssm_b_im,
             ssm_c_re=ssm_c_re, ssm_c_im=ssm_c_im, ssm_d=ssm_d, glu_w=glu_w, glu_b=glu_b,
             br_pool=br_pool, br_ssm=br_ssm, w_out=w_out, norm_x=norm_x, norm_mem=norm_mem,
             x_wq=x_wq, x_wkv=x_wkv, x_wo=x_wo, norm_ffn=norm_ffn, router_g_w=router_g_w,
             router_g_b=router_g_b, router_e_w=router_e_w, router_e_b=router_e_b, exp_w1=exp_w1,
             exp_w3=exp_w3, exp_w2=exp_w2)
    batch, seq, d = x.shape
    assert norm_mix.shape[0] == 1, "final norm is fused into the (single) layer's last stage"
    tm = min(512, seq)
    h2 = x.reshape(batch * seq, d)
    mem2 = mem.reshape(-1, d)
    out = _layer(h2, mem2, batch, seq, p, 0, norm_final, tm)
    return out.reshape(batch, seq, d)
```

```python
import functools
import math

import jax
import jax.numpy as jnp
from jax import lax
from jax.experimental import pallas as pl
from jax.experimental.pallas import tpu as pltpu

F32 = jnp.float32
BF16 = jnp.bfloat16

RMS_EPS = 1e-6
POOL_WINDOWS = (2, 4, 8, 16)
POOL_HALO = 16
SSM_GROUP_DIM = 16
SSM_STATE = 64
LANES = 128
SUBLANES = 8
OCT = LANES // SSM_GROUP_DIM
SSM_CHUNK = 4
STRAND_PAD = 8
X_HEADS = 4
MIXER_TILE = 1024
XATTN_TILE = 1024
N_EXPERT_GROUPS = 4
EXPERTS_PER_GROUP = 4
EXPERT_PAIRS = ((0, 1), (0, 2), (0, 3), (1, 3), (1, 2), (2, 3))
N_CLASS_ROWS = 32
VMEM_LIMIT = 56 * 1024 * 1024


def _rms(x, g):
    return x * lax.rsqrt(jnp.mean(x * x, axis=-1, keepdims=True) + RMS_EPS) * g


def _const_spec(shape):
    nd = len(shape)
    return pl.BlockSpec(shape, lambda *_: (0,) * nd, pipeline_mode=pl.Buffered(1))


def _ssm_tables(a_re, a_im, log_dt, b_re, b_im, c_re, c_im, d_skip, chunk, n_scan, n_pow):
    hi = lax.Precision.HIGHEST
    G, N = a_re.shape
    C = SSM_GROUP_DIM
    Q = G // OCT
    dt = jnp.exp(log_dt)[:, None]
    mag = jnp.exp(a_re * dt)
    lb_re, lb_im = mag * jnp.cos(a_im * dt), mag * jnp.sin(a_im * dt)
    den = a_re * a_re + a_im * a_im
    nr, ni = lb_re - 1.0, lb_im
    f_re = ((nr * a_re + ni * a_im) / den)[..., None]
    f_im = ((ni * a_re - nr * a_im) / den)[..., None]
    bb_re = f_re * b_re - f_im * b_im
    bb_im = f_re * b_im + f_im * b_re

    pw_re, pw_im = [jnp.ones_like(lb_re)], [jnp.zeros_like(lb_re)]
    for _ in range(chunk):
        r, i = pw_re[-1], pw_im[-1]
        pw_re.append(r * lb_re - i * lb_im)
        pw_im.append(r * lb_im + i * lb_re)
    pw_re, pw_im = jnp.stack(pw_re), jnp.stack(pw_im)

    lbb_re = pw_re[:chunk, :, :, None] * bb_re - pw_im[:chunk, :, :, None] * bb_im
    lbb_im = pw_re[:chunk, :, :, None] * bb_im + pw_im[:chunk, :, :, None] * bb_re
    kern = (jnp.einsum('gon,dgnc->dgoc', c_re, lbb_re, precision=hi)
            - jnp.einsum('gon,dgnc->dgoc', c_im, lbb_im, precision=hi))
    kern = kern.at[0].add(d_skip.reshape(G, C)[:, :, None] * jnp.eye(C, dtype=F32))

    lane_g = jnp.arange(LANES) // C
    state_g = jnp.arange(OCT * N) // N
    rep_c = (jnp.arange(C)[:, None] == (jnp.arange(LANES) % C)[None, :]).astype(F32)
    rep_n = (jnp.arange(N)[:, None] == (jnp.arange(OCT * N) % N)[None, :]).astype(F32)
    same_cc = (lane_g[:, None] == lane_g[None, :]).astype(F32)
    same_cn = (lane_g[:, None] == state_g[None, :]).astype(F32)

    kt = kern.transpose(0, 1, 3, 2).reshape(chunk, Q, LANES, C)
    blk = jnp.einsum('dqrc,cl->dqrl', kt, rep_c, precision=hi) * same_cc
    zero = jnp.zeros((Q, LANES, LANES), F32)
    m_intra = jnp.concatenate(
        [jnp.concatenate([blk[jp - j] if jp >= j else zero for jp in range(chunk)], axis=2)
         for j in range(chunk)], axis=1)

    def state_half(lbb):
        z = lbb[::-1].transpose(0, 1, 3, 2).reshape(chunk, Q, LANES, N)
        w = jnp.einsum('jqrn,nl->qjrl', z, rep_n, precision=hi) * same_cn
        return w.reshape(Q, chunk * LANES, OCT * N)
    w_state = jnp.concatenate([state_half(lbb_re), state_half(lbb_im)], axis=-1)

    p_re = c_re[None] * pw_re[1:, :, None, :] - c_im[None] * pw_im[1:, :, None, :]
    p_im = c_re[None] * pw_im[1:, :, None, :] + c_im[None] * pw_re[1:, :, None, :]
    def out_half(p):
        z = p.transpose(0, 1, 3, 2).reshape(chunk, Q, OCT * N, C)
        w = jnp.einsum('jqrc,cl->qrjl', z, rep_c, precision=hi)
        w = w * same_cn.T[None, :, None, :]
        return w.reshape(Q, OCT * N, chunk * LANES)
    w_out = jnp.concatenate([out_half(p_re), out_half(-p_im)], axis=1)

    lr, li = [pw_re[chunk]], [pw_im[chunk]]
    for _ in range(n_scan - 1):
        r, i = lr[-1], li[-1]
        lr.append(r * r - i * i)
        li.append(2.0 * r * i)
    qr, qi = [jnp.ones_like(lb_re)], [jnp.zeros_like(lb_re)]
    for _ in range(n_pow - 1):
        r, i = qr[-1], qi[-1]
        qr.append(r * pw_re[chunk] - i * pw_im[chunk])
        qi.append(r * pw_im[chunk] + i * pw_re[chunk])
    lam_re = jnp.stack(lr + qr).reshape(n_scan + n_pow, G * N)
    lam_im = jnp.stack(li + qi).reshape(n_scan + n_pow, G * N)
    return m_intra.astype(BF16), w_state.astype(BF16), w_out.astype(BF16), lam_re, lam_im


def _mixer_kernel(x_ref, nm_ref, win_ref, poolw_ref, pscale_ref, mi_ref, ws_ref, wh_ref,
                  lre_ref, lim_ref, gluw_ref, glub_ref, brp_ref, brs_ref, wout_ref,
                  o_ref, zext_ref, v_ref, y_ref, carry_ref, *, tiles_per_seq, chunk):
    tm, d_model = x_ref.shape
    pool_w = zext_ref.shape[1]
    n_oct = v_ref.shape[0]
    ssm_w = n_oct * LANES
    n_chunks = tm // chunk
    n_str = SUBLANES
    per = n_chunks // n_str
    span = per * chunk
    pitch = v_ref.shape[1] // n_str
    n_scan = lre_ref.shape[0] - per
    seq_tile = pl.program_id(0) % tiles_per_seq

    @pl.when(seq_tile == 0)
    def _():
        zext_ref[0:POOL_HALO, :] = jnp.zeros((POOL_HALO, pool_w), F32)
        carry_ref[...] = jnp.zeros_like(carry_ref)

    x = x_ref[...]
    u = _rms(x, nm_ref[...]).astype(BF16)
    proj = jnp.dot(u, win_ref[...], preferred_element_type=F32)
    z = proj[:, :pool_w]
    zext_ref[POOL_HALO:, :] = z
    for q in range(n_oct):
        for r in range(n_str):
            v_ref[q, r * pitch:r * pitch + span, :] = (
                proj[r * span:(r + 1) * span, pool_w + q * LANES:pool_w + (q + 1) * LANES])
    g_pool = jax.nn.sigmoid(proj[:, pool_w + ssm_w:pool_w + ssm_w + d_model])
    g_ssm = jax.nn.sigmoid(proj[:, pool_w + ssm_w + d_model:])

    pos = (seq_tile * tm + lax.broadcasted_iota(jnp.int32, (tm, 1), 0) + 1).astype(F32)
    gdim = pool_w // len(POOL_WINDOWS)
    ypool = []
    for gi, w in enumerate(POOL_WINDOWS):
        cols = slice(gi * gdim, (gi + 1) * gdim)
        acc = z[:, cols]
        for dlt in range(1, w):
            acc = acc + zext_ref[POOL_HALO - dlt:POOL_HALO - dlt + tm, cols]
        diff = acc / jnp.minimum(pos, float(w)) - z[:, cols]
        ypool.append(jnp.dot(diff.astype(BF16), poolw_ref[gi], preferred_element_type=F32))
    ypool = jnp.concatenate(ypool, axis=1) * pscale_ref[...]
    zext_ref[0:POOL_HALO, :] = zext_ref[tm:tm + POOL_HALO, :]
    p_br = jnp.dot(ypool.astype(BF16), brp_ref[...], preferred_element_type=F32)

    srow = lax.broadcasted_iota(jnp.int32, (n_str, 1), 0)
    half = OCT * SSM_STATE
    log_per = per.bit_length() - 1
    for q in range(n_oct):
        st = slice(q * half, (q + 1) * half)

        def lam(k):
            return lre_ref[k:k + 1, st], lim_ref[k:k + 1, st]

        vq = jnp.concatenate(
            [jnp.concatenate([v_ref[q, pl.ds(b * chunk + j, n_str, stride=pitch), :]
                              for j in range(chunk)], axis=1) for b in range(per)],
            axis=0).astype(BF16)
        s = jnp.dot(vq, ws_ref[q], preferred_element_type=F32)
        a_re, a_im = lam(0)
        loc_re, loc_im = [s[0:n_str, :half]], [s[0:n_str, half:]]
        for b in range(1, per):
            p_re, p_im = loc_re[-1], loc_im[-1]
            loc_re.append(s[b * n_str:(b + 1) * n_str, :half] + a_re * p_re - a_im * p_im)
            loc_im.append(s[b * n_str:(b + 1) * n_str, half:] + a_re * p_im + a_im * p_re)
        c_re, c_im = carry_ref[0:1, st], carry_ref[1:2, st]
        e_re, e_im = loc_re[-1], loc_im[-1]
        l_re, l_im = lam(log_per)
        e_re = e_re + jnp.where(srow == 0, l_re * c_re - l_im * c_im, 0.0)
        e_im = e_im + jnp.where(srow == 0, l_re * c_im + l_im * c_re, 0.0)
        for k in range(n_str.bit_length() - 1):
            sh = 1 << k
            r_re = jnp.where(srow >= sh, pltpu.roll(e_re, sh, axis=0), 0.0)
            r_im = jnp.where(srow >= sh, pltpu.roll(e_im, sh, axis=0), 0.0)
            l_re, l_im = lam(log_per + k)
            e_re, e_im = e_re + l_re * r_re - l_im * r_im, e_im + l_re * r_im + l_im * r_re
        carry_ref[0:1, st] = e_re[n_str - 1:n_str, :]
        carry_ref[1:2, st] = e_im[n_str - 1:n_str, :]
        i_re = jnp.where(srow == 0, c_re, pltpu.roll(e_re, 1, axis=0))
        i_im = jnp.where(srow == 0, c_im, pltpu.roll(e_im, 1, axis=0))
        h0_re, h0_im = [i_re], [i_im]
        for b in range(1, per):
            l_re, l_im = lam(n_scan + b)
            h0_re.append(loc_re[b - 1] + l_re * i_re - l_im * i_im)
            h0_im.append(loc_im[b - 1] + l_re * i_im + l_im * i_re)
        h0_re = jnp.concatenate(h0_re, axis=0).astype(BF16)
        h0_im = jnp.concatenate(h0_im, axis=0).astype(BF16)
        yq = (jnp.dot(vq, mi_ref[q], preferred_element_type=F32)
              + jnp.dot(h0_re, wh_ref[q, :half, :], preferred_element_type=F32)
              + jnp.dot(h0_im, wh_ref[q, half:, :], preferred_element_type=F32))
        for b in range(per):
            for j in range(chunk):
                y_ref[q, pl.ds(b * chunk + j, n_str, stride=pitch), :] = (
                    yq[b * n_str:(b + 1) * n_str, j * LANES:(j + 1) * LANES])

    y = jnp.concatenate(
        [jnp.concatenate([y_ref[q, r * pitch:r * pitch + span, :] for r in range(n_str)], axis=0)
         for q in range(n_oct)], axis=1)
    ys = jax.nn.gelu(y, approximate=True)
    glu = jnp.dot(ys.astype(BF16), gluw_ref[...], preferred_element_type=F32) + glub_ref[...]
    ys = ys * jax.nn.sigmoid(glu)
    s_br = jnp.dot(ys.astype(BF16), brs_ref[...], preferred_element_type=F32)

    merged = g_pool * p_br + g_ssm * s_br
    o_ref[...] = x + jnp.dot(merged.astype(BF16), wout_ref[...], preferred_element_type=F32)


def _mixer(x2, seq, nm, w_in, pool_w, pool_scale, tables, glu_w, glu_b, br_pool, br_ssm, w_out, tm):
    t, d = x2.shape
    mi, ws, wh, lre, lim = tables
    pool_width = pool_w.shape[0] * pool_w.shape[1]
    ssm_width = glu_w.shape[0]
    consts = [nm.reshape(1, d), w_in.astype(BF16), pool_w.astype(BF16),
              pool_scale.reshape(1, pool_width), mi, ws, wh, lre, lim,
              glu_w.astype(BF16), glu_b.reshape(1, ssm_width), br_pool.astype(BF16),
              br_ssm.astype(BF16), w_out.astype(BF16)]
    kern = functools.partial(_mixer_kernel, tiles_per_seq=seq // tm, chunk=SSM_CHUNK)
    return pl.pallas_call(
        kern,
        grid=(t // tm,),
        in_specs=[pl.BlockSpec((tm, d), lambda i: (i, 0))] + [_const_spec(c.shape) for c in consts],
        out_specs=pl.BlockSpec((tm, d), lambda i: (i, 0)),
        out_shape=jax.ShapeDtypeStruct((t, d), F32),
        scratch_shapes=[pltpu.VMEM((POOL_HALO + tm, pool_width), F32),
                        pltpu.VMEM((ssm_width // LANES, tm + SUBLANES * STRAND_PAD, LANES), F32),
                        pltpu.VMEM((ssm_width // LANES, tm + SUBLANES * STRAND_PAD, LANES), F32),
                        pltpu.VMEM((2, lre.shape[1]), F32)],
        compiler_params=pltpu.CompilerParams(dimension_semantics=("arbitrary",),
                                             vmem_limit_bytes=VMEM_LIMIT),
        name="mixer",
    )(x2, *consts)


def _kv_fold_kernel(mem_ref, g_ref, wkv_ref, wq_ref, wo_ref, wqk_ref, wvo_ref):
    m, d = mem_ref.shape
    hd = d // X_HEADS
    mn = _rms(mem_ref[...], g_ref[...]).astype(BF16)
    kv = jnp.dot(mn, wkv_ref[...], preferred_element_type=F32)
    for h in range(X_HEADS):
        k_h = kv[:, h * hd:(h + 1) * hd].astype(BF16)
        v_h = kv[:, d + h * hd:d + (h + 1) * hd].astype(BF16)
        qk = lax.dot_general(wq_ref[:, h * hd:(h + 1) * hd], k_h, (((1,), (1,)), ((), ())),
                             preferred_element_type=F32)
        wqk_ref[0, :, h * m:(h + 1) * m] = qk.astype(BF16)
        wvo_ref[0, h * m:(h + 1) * m, :] = jnp.dot(
            v_h, wo_ref[h * hd:(h + 1) * hd, :], preferred_element_type=F32).astype(BF16)


def _kv_fold(mem2, batch, g, w_kv, w_q, w_o):
    d = mem2.shape[1]
    m = mem2.shape[0] // batch
    hm = X_HEADS * m
    return pl.pallas_call(
        _kv_fold_kernel,
        grid=(batch,),
        in_specs=[pl.BlockSpec((m, d), lambda b: (b, 0)), _const_spec((1, d)),
                  _const_spec(w_kv.shape), _const_spec(w_q.shape), _const_spec(w_o.shape)],
        out_specs=[pl.BlockSpec((1, d, hm), lambda b: (b, 0, 0)),
                   pl.BlockSpec((1, hm, d), lambda b: (b, 0, 0))],
        out_shape=[jax.ShapeDtypeStruct((batch, d, hm), BF16),
                   jax.ShapeDtypeStruct((batch, hm, d), BF16)],
        compiler_params=pltpu.CompilerParams(dimension_semantics=("arbitrary",),
                                             vmem_limit_bytes=VMEM_LIMIT),
        name="kv_fold",
    )(mem2, g.reshape(1, d), w_kv.astype(BF16), w_q.astype(BF16), w_o.astype(BF16))


def _route(logits):
    n_g, n_eg = N_EXPERT_GROUPS, EXPERTS_PER_GROUP
    tm = logits.shape[0]
    lt = logits.T
    neg = jnp.float32(-jnp.inf)
    big = jnp.int32(1 << 20)

    def first_max(v):
        row = lax.broadcasted_iota(jnp.int32, v.shape, 0)
        m = jnp.max(v, axis=0, keepdims=True)
        return m, jnp.min(jnp.where(v == m, row, big), axis=0, keepdims=True)

    gl = lt[0:n_g]
    gmax, g_idx = first_max(gl)
    g_gate = 1.0 / jnp.sum(jnp.exp(gl - gmax), axis=0, keepdims=True)
    el = lt[n_g:n_g + n_eg]
    for g in range(1, n_g):
        el = jnp.where(g_idx == g, lt[n_g + g * n_eg:n_g + (g + 1) * n_eg], el)
    row = lax.broadcasted_iota(jnp.int32, el.shape, 0)
    t1, i1 = first_max(el)
    t2, i2 = first_max(jnp.where(row == i1, neg, el))
    e2 = jnp.exp(t2 - t1)
    w1 = 1.0 / (1.0 + e2)
    w2 = e2 / (1.0 + e2)
    comb = jnp.where(row == i1, w1, jnp.where(row == i2, w2, 0.0)) * g_gate
    lo, hi = jnp.minimum(i1, i2), jnp.maximum(i1, i2)
    pair = jnp.zeros_like(lo)
    for idx, (e_lo, e_hi) in enumerate(EXPERT_PAIRS):
        pair = jnp.where((lo == e_lo) & (hi == e_hi), idx, pair)
    cls = g_idx * len(EXPERT_PAIRS) + pair
    info = jnp.concatenate([comb, jnp.zeros((LANES - n_eg, tm), F32)], axis=0)
    hist = jnp.sum((cls == lax.broadcasted_iota(jnp.int32, (N_CLASS_ROWS, tm), 0)).astype(F32),
                   axis=1, keepdims=True)
    return info.T, cls, hist


def _dot_bf16x3(x, w_hilo):
    n = w_hilo.shape[1] // 2
    x_hi = x.astype(BF16)
    x_lo = (x - x_hi.astype(F32)).astype(BF16)
    both = jnp.dot(x_hi, w_hilo, preferred_element_type=F32)
    return both[:, :n] + both[:, n:] + jnp.dot(x_lo, w_hilo[:, :n], preferred_element_type=F32)


def _xattn_kernel(h_ref, g_ref, wqk_ref, wvo_ref, gffn_ref, wr_ref, br_ref, o_ref, cls_ref, hist_ref,
                  *, scale):
    h = h_ref[...]
    d = h.shape[1]
    m = wqk_ref.shape[2] // X_HEADS
    hn = _rms(h, g_ref[...]).astype(BF16)
    s_all = jnp.dot(hn, wqk_ref[0], preferred_element_type=F32) * scale
    probs = []
    for hh in range(X_HEADS):
        s = s_all[:, hh * m:(hh + 1) * m]
        s = s - jnp.max(s, axis=-1, keepdims=True)
        p = jnp.exp(s)
        probs.append((p / jnp.sum(p, axis=-1, keepdims=True)).astype(BF16))
    h_out = h + jnp.dot(jnp.concatenate(probs, axis=1), wvo_ref[0], preferred_element_type=F32)
    zn = _rms(h_out, gffn_ref[...])
    logits = _dot_bf16x3(zn, wr_ref[...]) + br_ref[...]
    route, cls, hist = _route(logits)
    o_ref[:, :d] = h_out
    o_ref[:, d:] = route
    cls_ref[0] = cls
    hist_ref[0] = jnp.broadcast_to(hist, hist_ref.shape[1:])


def _xattn(h2, seq, g, wqk, wvo, g_ffn, w_r, b_r, tm):
    t, d = h2.shape
    tiles_per_seq = seq // tm
    n_tiles = t // tm
    kern = functools.partial(_xattn_kernel, scale=(d // X_HEADS) ** -0.5)
    w_r_hi = w_r.astype(BF16)
    w_r_lo = (w_r - w_r_hi.astype(F32)).astype(BF16)
    w_r_hilo = jnp.concatenate([w_r_hi, w_r_lo], axis=1)
    hx, cls, hist = pl.pallas_call(
        kern,
        grid=(n_tiles,),
        in_specs=[pl.BlockSpec((tm, d), lambda i: (i, 0)), _const_spec((1, d)),
                  pl.BlockSpec((1,) + wqk.shape[1:], lambda i: (i // tiles_per_seq, 0, 0)),
                  pl.BlockSpec((1,) + wvo.shape[1:], lambda i: (i // tiles_per_seq, 0, 0)),
                  _const_spec((1, d)), _const_spec(w_r_hilo.shape), _const_spec(b_r.shape)],
        out_specs=[pl.BlockSpec((tm, d + LANES), lambda i: (i, 0)),
                   pl.BlockSpec((1, 1, tm), lambda i: (i, 0, 0)),
                   pl.BlockSpec((1, N_CLASS_ROWS, LANES), lambda i: (i, 0, 0))],
        out_shape=[jax.ShapeDtypeStruct((t, d + LANES), F32),
                   jax.ShapeDtypeStruct((n_tiles, 1, tm), jnp.int32),
                   jax.ShapeDtypeStruct((n_tiles, N_CLASS_ROWS, LANES), F32)],
        compiler_params=pltpu.CompilerParams(dimension_semantics=("arbitrary",),
                                             vmem_limit_bytes=VMEM_LIMIT),
        name="xattn",
    )(h2, g.reshape(1, d), wqk, wvo, g_ffn.reshape(1, d), w_r_hilo, b_r)
    return hx, cls.reshape(t), jnp.sum(hist[:, :, 0], axis=0).astype(jnp.int32)


def _dispatch_tables(cls, counts, tm):
    t = cls.shape[0]
    n_g, n_eg, n_pair = N_EXPERT_GROUPS, EXPERTS_PER_GROUP, len(EXPERT_PAIRS)
    n_tiles = t // tm + n_g
    i32 = jnp.int32
    _, tok_sorted = lax.sort((cls, jnp.arange(t, dtype=i32)), num_keys=1, is_stable=True)
    tok_sorted = jnp.concatenate([tok_sorted, jnp.zeros((tm,), i32)])
    counts = counts[:n_g * n_pair]
    c_start = jnp.cumsum(counts) - counts
    g_count = counts.reshape(n_g, n_pair).sum(axis=1)
    g_start = jnp.cumsum(g_count) - g_count
    tiles_g = (g_count + tm - 1) // tm
    tile_end = jnp.cumsum(tiles_g)
    tile_first = tile_end - tiles_g
    tile_ids = jnp.arange(n_tiles, dtype=i32)
    tile_group = jnp.minimum(jnp.sum((tile_ids[:, None] >= tile_end[None, :]).astype(i32), axis=1),
                             n_g - 1)
    k_in_group = tile_ids - tile_first[tile_group]
    tile_valid = jnp.clip(g_count[tile_group] - k_in_group * tm, 0, tm)
    tile_valid = jnp.where(tile_ids < tile_end[-1], tile_valid, 0)
    tile_start = jnp.where(tile_valid > 0, g_start[tile_group] + k_in_group * tm, 0)
    lo, hi = tile_start[:, None], (tile_start + tile_valid)[:, None]
    inter = (counts[None, :] > 0) & (c_start[None, :] < hi) & ((c_start + counts)[None, :] > lo)
    inter = inter.reshape(n_tiles, n_g, n_pair).any(axis=1)
    pair_has = jnp.array([[e in p for e in range(n_eg)] for p in EXPERT_PAIRS])
    need = (inter[:, :, None] & pair_has[None]).any(axis=1)
    tile_nexp = need.sum(axis=1)
    tile_exp = jnp.argsort(jnp.logical_not(need), axis=1, stable=True)
    return (tok_sorted, tile_start.astype(i32), tile_group.astype(i32), tile_valid.astype(i32),
            tile_nexp.astype(i32), tile_exp.reshape(-1).astype(i32))


def _moe_kernel(tok_ref, ts_ref, tg_ref, tv_ref, tn_ref, te_ref, hx_hbm, g_ref, w1_ref, w3_ref, w2_ref,
                gf_ref, out_hbm, xnext, osend, xcur, zn_ref, acc_ref, fin_ref, gsem, ssem):
    i = pl.program_id(0)
    n_blk, rows = xnext.shape[0] - 1, xnext.shape[1]
    tm = n_blk * rows
    d = osend.shape[2]
    n_eg = w1_ref.shape[0]

    def row_in(tok, blk, k):
        return pltpu.make_async_copy(hx_hbm.at[pl.ds(tok, 1), :], xnext.at[blk, pl.ds(k, 1), :], gsem)

    def row_out(tok, blk, k):
        return pltpu.make_async_copy(osend.at[blk, pl.ds(k, 1), :], out_hbm.at[pl.ds(tok, 1), :], ssem)

    def tile_in_wait():
        pltpu.make_async_copy(xnext.at[pl.ds(0, n_blk)], xnext.at[pl.ds(0, n_blk)], gsem).wait()

    def tile_out_wait():
        pltpu.make_async_copy(osend.at[pl.ds(0, n_blk)], osend.at[pl.ds(0, n_blk)], ssem).wait()

    def anchor(width):
        zero = xnext[n_blk, 0:1, 0:width]
        osend[n_blk, 0:1, 0:width] = zero
        return zero + osend[n_blk, 0:1, 0:width]

    def expert(zn, k, move_rows=None):
        e = te_ref[i * n_eg + k]
        if move_rows is not None:
            move_rows(0)
        a = jnp.dot(zn, w1_ref[e], preferred_element_type=F32)
        if move_rows is not None:
            a = a + anchor(a.shape[1])
            move_rows(1)
        b = jnp.dot(zn, w3_ref[e], preferred_element_type=F32)
        hid = (a * jax.nn.sigmoid(a) * b).astype(BF16)
        route = xcur[:, d:]
        lane = lax.broadcasted_iota(jnp.int32, route.shape, 1)
        comb = jnp.sum(jnp.where(lane == e, route, 0.0), axis=1, keepdims=True)
        out = comb * jnp.dot(hid, w2_ref[e], preferred_element_type=F32)
        if move_rows is not None:
            out = out + anchor(d)
        return out

    busy = tv_ref[i] > 0
    after_busy = tv_ref[jnp.maximum(i - 1, 0)] > 0
    drain = jnp.logical_and(jnp.logical_not(busy), jnp.logical_and(i >= 1, after_busy))

    @pl.when(i == 0)
    def _():
        xnext[n_blk] = jnp.zeros(xnext.shape[1:], F32)
        osend[...] = jnp.zeros(osend.shape, F32)
        base = ts_ref[0]

        def body(blk, c):
            for k in range(rows):
                row_in(tok_ref[base + blk * rows + k], blk, k).start()
            return c
        lax.fori_loop(0, n_blk, body, 0)

    @pl.when(jnp.logical_or(busy, drain))
    def _():
        tile_in_wait()
        xcur[...] = xnext[0:n_blk].reshape(tm, xnext.shape[2])

    def hand_over_results():
        @pl.when(i >= 1)
        def _():
            tile_out_wait()
            osend[0:n_blk] = fin_ref[...].reshape(n_blk, rows, d)

    n_always = 2

    @pl.when(busy)
    def _():
        h = xcur[:, :d]
        zn = _rms(h, g_ref[...]).astype(BF16)
        zn_ref[...] = zn
        nxt = ts_ref[i + 1]

        def gather_rows(part):
            for r in range(part * tm // 2, (part + 1) * tm // 2):
                row_in(tok_ref[nxt + r], r // rows, r % rows).start()
        acc_ref[...] = h + expert(zn, 0, gather_rows)

    @pl.when(busy)
    def _():
        hand_over_results()

    @pl.when(busy)
    def _():
        prv = ts_ref[jnp.maximum(i - 1, 0)]

        def scatter_rows(part):
            for r in range(part * tm // 2, (part + 1) * tm // 2):
                row_out(tok_ref[prv + r], r // rows, r % rows).start()
        acc_ref[...] += expert(zn_ref[...], 1, scatter_rows)

    for k in range(n_always, n_eg):
        @pl.when(tn_ref[i] > k)
        def _():
            acc_ref[...] += expert(zn_ref[...], k)

    @pl.when(busy)
    def _():
        fin_ref[...] = _rms(acc_ref[...], gf_ref[...])

    @pl.when(drain)
    def _():
        hand_over_results()
        base = ts_ref[i - 1]
        nv = tv_ref[i - 1]
        full = nv // rows

        def body(blk, c):
            for k in range(rows):
                row_out(tok_ref[base + blk * rows + k], blk, k).start()
            return c
        lax.fori_loop(0, full, body, 0)
        for k in range(rows - 1):

            @pl.when(full * rows + k < nv)
            def _():
                row_out(tok_ref[base + full * rows + k], full, k).start()

        @pl.when(full > 0)
        def _():
            pltpu.make_async_copy(osend.at[pl.ds(0, full)], osend.at[pl.ds(0, full)], ssem).wait()
        for k in range(rows - 1):

            @pl.when(full * rows + k < nv)
            def _():
                row_out(0, 0, 0).wait()


def _moe(hx, cls, counts, g, w1, w3, w2, g_final, tm):
    t, dx = hx.shape
    d = dx - LANES
    n_eg = EXPERTS_PER_GROUP
    de = w1.shape[2]
    tables = _dispatch_tables(cls, counts, tm)
    n_tiles = tables[2].shape[0]

    def wmap(i, tok, ts, tg, tv, tn, te):
        return (tg[i], 0, 0)

    def cmap(i, tok, ts, tg, tv, tn, te):
        return (0, 0)

    grid_spec = pltpu.PrefetchScalarGridSpec(
        num_scalar_prefetch=len(tables),
        grid=(n_tiles,),
        in_specs=[pl.BlockSpec(memory_space=pl.ANY),
                  pl.BlockSpec((1, d), cmap),
                  pl.BlockSpec((n_eg, d, de), wmap),
                  pl.BlockSpec((n_eg, d, de), wmap),
                  pl.BlockSpec((n_eg, de, d), wmap),
                  pl.BlockSpec((1, d), cmap)],
        out_specs=pl.BlockSpec(memory_space=pl.ANY),
        scratch_shapes=[pltpu.VMEM((tm // SUBLANES + 1, SUBLANES, dx), F32),
                        pltpu.VMEM((tm // SUBLANES + 1, SUBLANES, d), F32),
                        pltpu.VMEM((tm, dx), F32),
                        pltpu.VMEM((tm, d), BF16),
                        pltpu.VMEM((tm, d), F32),
                        pltpu.VMEM((tm, d), F32),
                        pltpu.SemaphoreType.DMA(()), pltpu.SemaphoreType.DMA(())])
    return pl.pallas_call(
        _moe_kernel,
        grid_spec=grid_spec,
        out_shape=jax.ShapeDtypeStruct((t, d), F32),
        compiler_params=pltpu.CompilerParams(dimension_semantics=("arbitrary",),
                                             vmem_limit_bytes=VMEM_LIMIT),
        name="moe",
    )(*tables, hx, g.reshape(1, d), w1.astype(BF16), w3.astype(BF16), w2.astype(BF16),
      g_final.reshape(1, d))


def _layer(h2, mem2, batch, seq, p, l, g_final, tm):
    d = h2.shape[1]
    tm_mix = min(MIXER_TILE, seq)
    n_chunks = tm_mix // SSM_CHUNK
    n_scan = int(math.log2(n_chunks))
    tables = _ssm_tables(p["ssm_a_re"][l], p["ssm_a_im"][l], p["ssm_log_dt"][l], p["ssm_b_re"][l],
                         p["ssm_b_im"][l], p["ssm_c_re"][l], p["ssm_c_im"][l], p["ssm_d"][l],
                         SSM_CHUNK, n_scan, n_chunks // SUBLANES)
    h2 = _mixer(h2, seq, p["norm_mix"][l], p["w_in"][l], p["pool_w"][l], p["pool_scale"][l], tables,
                p["glu_w"][l], p["glu_b"][l], p["br_pool"][l], p["br_ssm"][l], p["w_out"][l],
                tm_mix)

    wqk, wvo = _kv_fold(mem2, batch, p["norm_mem"][l], p["x_wkv"][l], p["x_wq"][l], p["x_wo"][l])

    n_g, n_eg = N_EXPERT_GROUPS, EXPERTS_PER_GROUP
    w_r = jnp.concatenate([p["router_g_w"][l], p["router_e_w"][l].reshape(d, n_g * n_eg)], axis=1)
    w_r = jnp.pad(w_r, ((0, 0), (0, LANES - w_r.shape[1])))
    b_r = jnp.concatenate([p["router_g_b"][l], p["router_e_b"][l].reshape(n_g * n_eg)])
    b_r = jnp.pad(b_r, (0, LANES - b_r.shape[0])).reshape(1, LANES)
    hx, cls, counts = _xattn(h2, seq, p["norm_x"][l], wqk, wvo, p["norm_ffn"][l], w_r, b_r,
                             min(XATTN_TILE, seq))
    return _moe(hx, cls, counts, p["norm_ffn"][l], p["exp_w1"][l], p["exp_w3"][l], p["exp_w2"][l], g_final, tm)


def kernel(x, mem, norm_mix, w_in, pool_w, pool_scale, ssm_a_re, ssm_a_im, ssm_log_dt, ssm_b_re,
           ssm_b_im, ssm_c_re, ssm_c_im, ssm_d, glu_w, glu_b, br_pool, br_ssm, w_out, norm_x,
           norm_mem, x_wq, x_wkv, x_wo, norm_ffn, router_g_w, router_g_b, router_e_w, router_e_b,
           exp_w1, exp_w3, exp_w2, norm_final):
    p = dict(norm_mix=norm_mix, w_in=w_in, pool_w=pool_w, pool_scale=pool_scale, ssm_a_re=ssm_a_re,
             ssm_a_im=ssm_a_im, ssm_log_dt=ssm_log_dt, ssm_b_re=ssm_b_re, ssm_b_im=ssm_b_im,
             ssm_c_re=ssm_c_re, ssm_c_im=ssm_c_im, ssm_d=ssm_d, glu_w=glu_w, glu_b=glu_b,
             br_pool=br_pool, br_ssm=br_ssm, w_out=w_out, norm_x=norm_x, norm_mem=norm_mem,
             x_wq=x_wq, x_wkv=x_wkv, x_wo=x_wo, norm_ffn=norm_ffn, router_g_w=router_g_w,
             router_g_b=router_g_b, router_e_w=router_e_w, router_e_b=router_e_b, exp_w1=exp_w1,
             exp_w3=exp_w3, exp_w2=exp_w2)
    batch, seq, d = x.shape
    assert norm_mix.shape[0] == 1, "final norm is fused into the (single) layer's last stage"
    tm = min(512, seq)
    h2 = x.reshape(batch * seq, d)
    mem2 = mem.reshape(-1, d)
    out = _layer(h2, mem2, batch, seq, p, 0, norm_final, tm)
    return out.reshape(batch, seq, d)
```

```python
import functools
import math

import jax
import jax.numpy as jnp
from jax import lax
from jax.experimental import pallas as pl
from jax.experimental.pallas import tpu as pltpu

F32 = jnp.float32
BF16 = jnp.bfloat16

RMS_EPS = 1e-6
POOL_WINDOWS = (2, 4, 8, 16)
POOL_HALO = 16
SSM_GROUP_DIM = 16
SSM_STATE = 64
LANES = 128
SUBLANES = 8
OCT = LANES // SSM_GROUP_DIM
SSM_CHUNK = 4
STRAND_PAD = 8
X_HEADS = 4
MIXER_TILE = 1024
XATTN_TILE = 1024
MOE_TILE = 512
N_EXPERT_GROUPS = 4
EXPERTS_PER_GROUP = 4
EXPERT_PAIRS = ((0, 1), (0, 2), (0, 3), (1, 3), (1, 2), (2, 3))
N_CLASS_ROWS = 32
VMEM_LIMIT = 56 * 1024 * 1024


def _rms(x, g):
    return x * lax.rsqrt(jnp.mean(x * x, axis=-1, keepdims=True) + RMS_EPS) * g


def _const_spec(shape):
    nd = len(shape)
    return pl.BlockSpec(shape, lambda *_: (0,) * nd, pipeline_mode=pl.Buffered(1))


def _ssm_tables(a_re, a_im, log_dt, b_re, b_im, c_re, c_im, d_skip, chunk, n_scan, n_pow):
    hi = lax.Precision.HIGHEST
    G, N = a_re.shape
    C = SSM_GROUP_DIM
    Q = G // OCT
    dt = jnp.exp(log_dt)[:, None]
    mag = jnp.exp(a_re * dt)
    lb_re, lb_im = mag * jnp.cos(a_im * dt), mag * jnp.sin(a_im * dt)
    den = a_re * a_re + a_im * a_im
    nr, ni = lb_re - 1.0, lb_im
    f_re = ((nr * a_re + ni * a_im) / den)[..., None]
    f_im = ((ni * a_re - nr * a_im) / den)[..., None]
    bb_re = f_re * b_re - f_im * b_im
    bb_im = f_re * b_im + f_im * b_re

    pw_re, pw_im = [jnp.ones_like(lb_re)], [jnp.zeros_like(lb_re)]
    for _ in range(chunk):
        r, i = pw_re[-1], pw_im[-1]
        pw_re.append(r * lb_re - i * lb_im)
        pw_im.append(r * lb_im + i * lb_re)
    pw_re, pw_im = jnp.stack(pw_re), jnp.stack(pw_im)

    lbb_re = pw_re[:chunk, :, :, None] * bb_re - pw_im[:chunk, :, :, None] * bb_im
    lbb_im = pw_re[:chunk, :, :, None] * bb_im + pw_im[:chunk, :, :, None] * bb_re
    kern = (jnp.einsum('gon,dgnc->dgoc', c_re, lbb_re, precision=hi)
            - jnp.einsum('gon,dgnc->dgoc', c_im, lbb_im, precision=hi))
    kern = kern.at[0].add(d_skip.reshape(G, C)[:, :, None] * jnp.eye(C, dtype=F32))

    lane_g = jnp.arange(LANES) // C
    state_g = jnp.arange(OCT * N) // N
    rep_c = (jnp.arange(C)[:, None] == (jnp.arange(LANES) % C)[None, :]).astype(F32)
    rep_n = (jnp.arange(N)[:, None] == (jnp.arange(OCT * N) % N)[None, :]).astype(F32)
    same_cc = (lane_g[:, None] == lane_g[None, :]).astype(F32)
    same_cn = (lane_g[:, None] == state_g[None, :]).astype(F32)

    kt = kern.transpose(0, 1, 3, 2).reshape(chunk, Q, LANES, C)
    blk = jnp.einsum('dqrc,cl->dqrl', kt, rep_c, precision=hi) * same_cc
    zero = jnp.zeros((Q, LANES, LANES), F32)
    m_intra = jnp.concatenate(
        [jnp.concatenate([blk[jp - j] if jp >= j else zero for jp in range(chunk)], axis=2)
         for j in range(chunk)], axis=1)

    def state_half(lbb):
        z = lbb[::-1].transpose(0, 1, 3, 2).reshape(chunk, Q, LANES, N)
        w = jnp.einsum('jqrn,nl->qjrl', z, rep_n, precision=hi) * same_cn
        return w.reshape(Q, chunk * LANES, OCT * N)
    w_state = jnp.concatenate([state_half(lbb_re), state_half(lbb_im)], axis=-1)

    p_re = c_re[None] * pw_re[1:, :, None, :] - c_im[None] * pw_im[1:, :, None, :]
    p_im = c_re[None] * pw_im[1:, :, None, :] + c_im[None] * pw_re[1:, :, None, :]
    def out_half(p):
        z = p.transpose(0, 1, 3, 2).reshape(chunk, Q, OCT * N, C)
        w = jnp.einsum('jqrc,cl->qrjl', z, rep_c, precision=hi)
        w = w * same_cn.T[None, :, None, :]
        return w.reshape(Q, OCT * N, chunk * LANES)
    w_out = jnp.concatenate([out_half(p_re), out_half(-p_im)], axis=1)

    lr, li = [pw_re[chunk]], [pw_im[chunk]]
    for _ in range(n_scan - 1):
        r, i = lr[-1], li[-1]
        lr.append(r * r - i * i)
        li.append(2.0 * r * i)
    qr, qi = [jnp.ones_like(lb_re)], [jnp.zeros_like(lb_re)]
    for _ in range(n_pow - 1):
        r, i = qr[-1], qi[-1]
        qr.append(r * pw_re[chunk] - i * pw_im[chunk])
        qi.append(r * pw_im[chunk] + i * pw_re[chunk])
    lam_re = jnp.stack(lr + qr).reshape(n_scan + n_pow, G * N)
    lam_im = jnp.stack(li + qi).reshape(n_scan + n_pow, G * N)
    return m_intra.astype(BF16), w_state.astype(BF16), w_out.astype(BF16), lam_re, lam_im


def _mixer_kernel(x_ref, nm_ref, win_ref, poolw_ref, pscale_ref, mi_ref, ws_ref, wh_ref,
                  lre_ref, lim_ref, gluw_ref, glub_ref, brp_ref, brs_ref, wout_ref,
                  o_ref, zext_ref, v_ref, y_ref, carry_ref, *, tiles_per_seq, chunk):
    tm, d_model = x_ref.shape
    pool_w = zext_ref.shape[1]
    n_oct = v_ref.shape[0]
    ssm_w = n_oct * LANES
    n_chunks = tm // chunk
    n_str = SUBLANES
    per = n_chunks // n_str
    span = per * chunk
    pitch = v_ref.shape[1] // n_str
    n_scan = lre_ref.shape[0] - per
    seq_tile = pl.program_id(0) % tiles_per_seq

    @pl.when(seq_tile == 0)
    def _():
        zext_ref[0:POOL_HALO, :] = jnp.zeros((POOL_HALO, pool_w), F32)
        carry_ref[...] = jnp.zeros_like(carry_ref)

    x = x_ref[...]
    u = _rms(x, nm_ref[...]).astype(BF16)
    proj = jnp.dot(u, win_ref[...], preferred_element_type=F32)
    z = proj[:, :pool_w]
    zext_ref[POOL_HALO:, :] = z
    for q in range(n_oct):
        for r in range(n_str):
            v_ref[q, r * pitch:r * pitch + span, :] = (
                proj[r * span:(r + 1) * span, pool_w + q * LANES:pool_w + (q + 1) * LANES])
    g_pool = jax.nn.sigmoid(proj[:, pool_w + ssm_w:pool_w + ssm_w + d_model])
    g_ssm = jax.nn.sigmoid(proj[:, pool_w + ssm_w + d_model:])

    pos = (seq_tile * tm + lax.broadcasted_iota(jnp.int32, (tm, 1), 0) + 1).astype(F32)
    gdim = pool_w // len(POOL_WINDOWS)
    ypool = []
    for gi, w in enumerate(POOL_WINDOWS):
        cols = slice(gi * gdim, (gi + 1) * gdim)
        acc = z[:, cols]
        for dlt in range(1, w):
            acc = acc + zext_ref[POOL_HALO - dlt:POOL_HALO - dlt + tm, cols]
        diff = acc / jnp.minimum(pos, float(w)) - z[:, cols]
        ypool.append(jnp.dot(diff.astype(BF16), poolw_ref[gi], preferred_element_type=F32))
    ypool = jnp.concatenate(ypool, axis=1) * pscale_ref[...]
    zext_ref[0:POOL_HALO, :] = zext_ref[tm:tm + POOL_HALO, :]
    p_br = jnp.dot(ypool.astype(BF16), brp_ref[...], preferred_element_type=F32)

    srow = lax.broadcasted_iota(jnp.int32, (n_str, 1), 0)
    half = OCT * SSM_STATE
    log_per = per.bit_length() - 1
    for q in range(n_oct):
        st = slice(q * half, (q + 1) * half)

        def lam(k):
            return lre_ref[k:k + 1, st], lim_ref[k:k + 1, st]

        vq = jnp.concatenate(
            [jnp.concatenate([v_ref[q, pl.ds(b * chunk + j, n_str, stride=pitch), :]
                              for j in range(chunk)], axis=1) for b in range(per)],
            axis=0).astype(BF16)
        s = jnp.dot(vq, ws_ref[q], preferred_element_type=F32)
        a_re, a_im = lam(0)
        loc_re, loc_im = [s[0:n_str, :half]], [s[0:n_str, half:]]
        for b in range(1, per):
            p_re, p_im = loc_re[-1], loc_im[-1]
            loc_re.append(s[b * n_str:(b + 1) * n_str, :half] + a_re * p_re - a_im * p_im)
            loc_im.append(s[b * n_str:(b + 1) * n_str, half:] + a_re * p_im + a_im * p_re)
        c_re, c_im = carry_ref[0:1, st], carry_ref[1:2, st]
        e_re, e_im = loc_re[-1], loc_im[-1]
        l_re, l_im = lam(log_per)
        e_re = e_re + jnp.where(srow == 0, l_re * c_re - l_im * c_im, 0.0)
        e_im = e_im + jnp.where(srow == 0, l_re * c_im + l_im * c_re, 0.0)
        for k in range(n_str.bit_length() - 1):
            sh = 1 << k
            r_re = jnp.where(srow >= sh, pltpu.roll(e_re, sh, axis=0), 0.0)
            r_im = jnp.where(srow >= sh, pltpu.roll(e_im, sh, axis=0), 0.0)
            l_re, l_im = lam(log_per + k)
            e_re, e_im = e_re + l_re * r_re - l_im * r_im, e_im + l_re * r_im + l_im * r_re
        carry_ref[0:1, st] = e_re[n_str - 1:n_str, :]
        carry_ref[1:2, st] = e_im[n_str - 1:n_str, :]
        i_re = jnp.where(srow == 0, c_re, pltpu.roll(e_re, 1, axis=0))
        i_im = jnp.where(srow == 0, c_im, pltpu.roll(e_im, 1, axis=0))
        h0_re, h0_im = [i_re], [i_im]
        for b in range(1, per):
            l_re, l_im = lam(n_scan + b)
            h0_re.append(loc_re[b - 1] + l_re * i_re - l_im * i_im)
            h0_im.append(loc_im[b - 1] + l_re * i_im + l_im * i_re)
        h0_re = jnp.concatenate(h0_re, axis=0).astype(BF16)
        h0_im = jnp.concatenate(h0_im, axis=0).astype(BF16)
        yq = (jnp.dot(vq, mi_ref[q], preferred_element_type=F32)
              + jnp.dot(h0_re, wh_ref[q, :half, :], preferred_element_type=F32)
              + jnp.dot(h0_im, wh_ref[q, half:, :], preferred_element_type=F32))
        for b in range(per):
            for j in range(chunk):
                y_ref[q, pl.ds(b * chunk + j, n_str, stride=pitch), :] = (
                    yq[b * n_str:(b + 1) * n_str, j * LANES:(j + 1) * LANES])

    y = jnp.concatenate(
        [jnp.concatenate([y_ref[q, r * pitch:r * pitch + span, :] for r in range(n_str)], axis=0)
         for q in range(n_oct)], axis=1)
    ys = jax.nn.gelu(y, approximate=True)
    glu = jnp.dot(ys.astype(BF16), gluw_ref[...], preferred_element_type=F32) + glub_ref[...]
    ys = ys * jax.nn.sigmoid(glu)
    s_br = jnp.dot(ys.astype(BF16), brs_ref[...], preferred_element_type=F32)

    merged = g_pool * p_br + g_ssm * s_br
    o_ref[...] = x + jnp.dot(merged.astype(BF16), wout_ref[...], preferred_element_type=F32)


def _mixer(x2, seq, nm, w_in, pool_w, pool_scale, tables, glu_w, glu_b, br_pool, br_ssm, w_out, tm):
    t, d = x2.shape
    mi, ws, wh, lre, lim = tables
    pool_width = pool_w.shape[0] * pool_w.shape[1]
    ssm_width = glu_w.shape[0]
    consts = [nm.reshape(1, d), w_in.astype(BF16), pool_w.astype(BF16),
              pool_scale.reshape(1, pool_width), mi, ws, wh, lre, lim,
              glu_w.astype(BF16), glu_b.reshape(1, ssm_width), br_pool.astype(BF16),
              br_ssm.astype(BF16), w_out.astype(BF16)]
    kern = functools.partial(_mixer_kernel, tiles_per_seq=seq // tm, chunk=SSM_CHUNK)
    return pl.pallas_call(
        kern,
        grid=(t // tm,),
        in_specs=[pl.BlockSpec((tm, d), lambda i: (i, 0))] + [_const_spec(c.shape) for c in consts],
        out_specs=pl.BlockSpec((tm, d), lambda i: (i, 0)),
        out_shape=jax.ShapeDtypeStruct((t, d), F32),
        scratch_shapes=[pltpu.VMEM((POOL_HALO + tm, pool_width), F32),
                        pltpu.VMEM((ssm_width // LANES, tm + SUBLANES * STRAND_PAD, LANES), F32),
                        pltpu.VMEM((ssm_width // LANES, tm + SUBLANES * STRAND_PAD, LANES), F32),
                        pltpu.VMEM((2, lre.shape[1]), F32)],
        compiler_params=pltpu.CompilerParams(dimension_semantics=("arbitrary",),
                                             vmem_limit_bytes=VMEM_LIMIT),
        name="mixer",
    )(x2, *consts)


def _kv_fold_kernel(mem_ref, g_ref, wkv_ref, wq_ref, wo_ref, wqk_ref, wvo_ref):
    m, d = mem_ref.shape
    hd = d // X_HEADS
    mn = _rms(mem_ref[...], g_ref[...]).astype(BF16)
    kv = jnp.dot(mn, wkv_ref[...], preferred_element_type=F32)
    for h in range(X_HEADS):
        k_h = kv[:, h * hd:(h + 1) * hd].astype(BF16)
        v_h = kv[:, d + h * hd:d + (h + 1) * hd].astype(BF16)
        qk = lax.dot_general(wq_ref[:, h * hd:(h + 1) * hd], k_h, (((1,), (1,)), ((), ())),
                             preferred_element_type=F32)
        wqk_ref[0, :, h * m:(h + 1) * m] = qk.astype(BF16)
        wvo_ref[0, h * m:(h + 1) * m, :] = jnp.dot(
            v_h, wo_ref[h * hd:(h + 1) * hd, :], preferred_element_type=F32).astype(BF16)


def _kv_fold(mem2, batch, g, w_kv, w_q, w_o):
    d = mem2.shape[1]
    m = mem2.shape[0] // batch
    hm = X_HEADS * m
    return pl.pallas_call(
        _kv_fold_kernel,
        grid=(batch,),
        in_specs=[pl.BlockSpec((m, d), lambda b: (b, 0)), _const_spec((1, d)),
                  _const_spec(w_kv.shape), _const_spec(w_q.shape), _const_spec(w_o.shape)],
        out_specs=[pl.BlockSpec((1, d, hm), lambda b: (b, 0, 0)),
                   pl.BlockSpec((1, hm, d), lambda b: (b, 0, 0))],
        out_shape=[jax.ShapeDtypeStruct((batch, d, hm), BF16),
                   jax.ShapeDtypeStruct((batch, hm, d), BF16)],
        compiler_params=pltpu.CompilerParams(dimension_semantics=("arbitrary",),
                                             vmem_limit_bytes=VMEM_LIMIT),
        name="kv_fold",
    )(mem2, g.reshape(1, d), w_kv.astype(BF16), w_q.astype(BF16), w_o.astype(BF16))


def _route(logits):
    n_g, n_eg = N_EXPERT_GROUPS, EXPERTS_PER_GROUP
    tm = logits.shape[0]
    lt = logits.T
    neg = jnp.float32(-jnp.inf)
    big = jnp.int32(1 << 20)

    def first_max(v):
        row = lax.broadcasted_iota(jnp.int32, v.shape, 0)
        m = jnp.max(v, axis=0, keepdims=True)
        return m, jnp.min(jnp.where(v == m, row, big), axis=0, keepdims=True)

    gl = lt[0:n_g]
    gmax, g_idx = first_max(gl)
    g_gate = 1.0 / jnp.sum(jnp.exp(gl - gmax), axis=0, keepdims=True)
    el = lt[n_g:n_g + n_eg]
    for g in range(1, n_g):
        el = jnp.where(g_idx == g, lt[n_g + g * n_eg:n_g + (g + 1) * n_eg], el)
    row = lax.broadcasted_iota(jnp.int32, el.shape, 0)
    t1, i1 = first_max(el)
    t2, i2 = first_max(jnp.where(row == i1, neg, el))
    e2 = jnp.exp(t2 - t1)
    w1 = 1.0 / (1.0 + e2)
    w2 = e2 / (1.0 + e2)
    comb = jnp.where(row == i1, w1, jnp.where(row == i2, w2, 0.0)) * g_gate
    lo, hi = jnp.minimum(i1, i2), jnp.maximum(i1, i2)
    pair = jnp.zeros_like(lo)
    for idx, (e_lo, e_hi) in enumerate(EXPERT_PAIRS):
        pair = jnp.where((lo == e_lo) & (hi == e_hi), idx, pair)
    cls = g_idx * len(EXPERT_PAIRS) + pair
    info = jnp.concatenate([comb, jnp.zeros((LANES - n_eg, tm), F32)], axis=0)
    hist = jnp.sum((cls == lax.broadcasted_iota(jnp.int32, (N_CLASS_ROWS, tm), 0)).astype(F32),
                   axis=1, keepdims=True)
    return info.T, cls, hist


def _dot_bf16x3(x, w_hilo):
    n = w_hilo.shape[1] // 2
    x_hi = x.astype(BF16)
    x_lo = (x - x_hi.astype(F32)).astype(BF16)
    both = jnp.dot(x_hi, w_hilo, preferred_element_type=F32)
    return both[:, :n] + both[:, n:] + jnp.dot(x_lo, w_hilo[:, :n], preferred_element_type=F32)


def _xattn_kernel(h_ref, g_ref, wqk_ref, wvo_ref, gffn_ref, wr_ref, br_ref, o_ref, cls_ref, hist_ref,
                  *, scale):
    h = h_ref[...]
    d = h.shape[1]
    m = wqk_ref.shape[2] // X_HEADS
    hn = _rms(h, g_ref[...]).astype(BF16)
    s_all = jnp.dot(hn, wqk_ref[0], preferred_element_type=F32) * scale
    probs = []
    for hh in range(X_HEADS):
        s = s_all[:, hh * m:(hh + 1) * m]
        s = s - jnp.max(s, axis=-1, keepdims=True)
        p = jnp.exp(s)
        probs.append((p / jnp.sum(p, axis=-1, keepdims=True)).astype(BF16))
    h_out = h + jnp.dot(jnp.concatenate(probs, axis=1), wvo_ref[0], preferred_element_type=F32)
    zn = _rms(h_out, gffn_ref[...])
    logits = _dot_bf16x3(zn, wr_ref[...]) + br_ref[...]
    route, cls, hist = _route(logits)
    o_ref[:, :d] = h_out
    o_ref[:, d:] = route
    cls_ref[0] = cls
    hist_ref[0] = jnp.broadcast_to(hist, hist_ref.shape[1:])


def _xattn(h2, seq, g, wqk, wvo, g_ffn, w_r, b_r, tm):
    t, d = h2.shape
    tiles_per_seq = seq // tm
    n_tiles = t // tm
    kern = functools.partial(_xattn_kernel, scale=(d // X_HEADS) ** -0.5)
    w_r_hi = w_r.astype(BF16)
    w_r_lo = (w_r - w_r_hi.astype(F32)).astype(BF16)
    w_r_hilo = jnp.concatenate([w_r_hi, w_r_lo], axis=1)
    hx, cls, hist = pl.pallas_call(
        kern,
        grid=(n_tiles,),
        in_specs=[pl.BlockSpec((tm, d), lambda i: (i, 0)), _const_spec((1, d)),
                  pl.BlockSpec((1,) + wqk.shape[1:], lambda i: (i // tiles_per_seq, 0, 0)),
                  pl.BlockSpec((1,) + wvo.shape[1:], lambda i: (i // tiles_per_seq, 0, 0)),
                  _const_spec((1, d)), _const_spec(w_r_hilo.shape), _const_spec(b_r.shape)],
        out_specs=[pl.BlockSpec((tm, d + LANES), lambda i: (i, 0)),
                   pl.BlockSpec((1, 1, tm), lambda i: (i, 0, 0)),
                   pl.BlockSpec((1, N_CLASS_ROWS, LANES), lambda i: (i, 0, 0))],
        out_shape=[jax.ShapeDtypeStruct((t, d + LANES), F32),
                   jax.ShapeDtypeStruct((n_tiles, 1, tm), jnp.int32),
                   jax.ShapeDtypeStruct((n_tiles, N_CLASS_ROWS, LANES), F32)],
        compiler_params=pltpu.CompilerParams(dimension_semantics=("arbitrary",),
                                             vmem_limit_bytes=VMEM_LIMIT),
        name="xattn",
    )(h2, g.reshape(1, d), wqk, wvo, g_ffn.reshape(1, d), w_r_hilo, b_r)
    return hx, cls.reshape(t), jnp.sum(hist[:, :, 0], axis=0).astype(jnp.int32)


def _dispatch_tables(cls, counts, tm):
    t = cls.shape[0]
    n_g, n_eg, n_pair = N_EXPERT_GROUPS, EXPERTS_PER_GROUP, len(EXPERT_PAIRS)
    n_tiles = t // tm + n_g
    i32 = jnp.int32
    _, tok_sorted = lax.sort((cls, jnp.arange(t, dtype=i32)), num_keys=1, is_stable=True)
    tok_sorted = jnp.concatenate([tok_sorted, jnp.zeros((tm,), i32)])
    counts = counts[:n_g * n_pair]
    c_start = jnp.cumsum(counts) - counts
    g_count = counts.reshape(n_g, n_pair).sum(axis=1)
    g_start = jnp.cumsum(g_count) - g_count
    tiles_g = (g_count + tm - 1) // tm
    tile_end = jnp.cumsum(tiles_g)
    tile_first = tile_end - tiles_g
    tile_ids = jnp.arange(n_tiles, dtype=i32)
    tile_group = jnp.minimum(jnp.sum((tile_ids[:, None] >= tile_end[None, :]).astype(i32), axis=1),
                             n_g - 1)
    k_in_group = tile_ids - tile_first[tile_group]
    tile_valid = jnp.clip(g_count[tile_group] - k_in_group * tm, 0, tm)
    tile_valid = jnp.where(tile_ids < tile_end[-1], tile_valid, 0)
    tile_start = jnp.where(tile_valid > 0, g_start[tile_group] + k_in_group * tm, 0)
    lo, hi = tile_start[:, None], (tile_start + tile_valid)[:, None]
    inter = (counts[None, :] > 0) & (c_start[None, :] < hi) & ((c_start + counts)[None, :] > lo)
    inter = inter.reshape(n_tiles, n_g, n_pair).any(axis=1)
    pair_has = jnp.array([[e in p for e in range(n_eg)] for p in EXPERT_PAIRS])
    need = (inter[:, :, None] & pair_has[None]).any(axis=1)
    tile_nexp = need.sum(axis=1)
    tile_exp = jnp.argsort(jnp.logical_not(need), axis=1, stable=True)
    return (tok_sorted, tile_start.astype(i32), tile_group.astype(i32), tile_valid.astype(i32),
            tile_nexp.astype(i32), tile_exp.reshape(-1).astype(i32))


def _moe_kernel(tok_ref, ts_ref, tg_ref, tv_ref, tn_ref, te_ref, hx_hbm, g_ref, w1_ref, w3_ref, w2_ref,
                gf_ref, out_hbm, xnext, osend, xcur, zn_ref, acc0_ref, acc_ref, gsem, ssem):
    i = pl.program_id(0)
    n_blk, rows = xnext.shape[0] - 1, xnext.shape[1]
    tm = n_blk * rows
    d = osend.shape[2]
    n_eg = w1_ref.shape[0]

    def row_in(tok, blk, k):
        return pltpu.make_async_copy(hx_hbm.at[pl.ds(tok, 1), :], xnext.at[blk, pl.ds(k, 1), :], gsem)

    def row_out(tok, blk, k):
        return pltpu.make_async_copy(osend.at[blk, pl.ds(k, 1), :], out_hbm.at[pl.ds(tok, 1), :], ssem)

    def tile_in_wait():
        pltpu.make_async_copy(xnext.at[pl.ds(0, n_blk)], xnext.at[pl.ds(0, n_blk)], gsem).wait()

    def tile_out_wait():
        pltpu.make_async_copy(osend.at[pl.ds(0, n_blk)], osend.at[pl.ds(0, n_blk)], ssem).wait()

    def anchor(width):
        zero = xnext[n_blk, 0:1, 0:width]
        osend[n_blk, 0:1, 0:width] = zero
        return zero + osend[n_blk, 0:1, 0:width]

    def expert(zn, k, move_rows=None):
        e = te_ref[i * n_eg + k]
        if move_rows is not None:
            move_rows(0)
        a = jnp.dot(zn, w1_ref[e], preferred_element_type=F32)
        if move_rows is not None:
            a = a + anchor(a.shape[1])
            move_rows(1)
        b = jnp.dot(zn, w3_ref[e], preferred_element_type=F32)
        hid = (a * jax.nn.sigmoid(a) * b).astype(BF16)
        route = xcur[:, d:]
        lane = lax.broadcasted_iota(jnp.int32, route.shape, 1)
        comb = jnp.sum(jnp.where(lane == e, route, 0.0), axis=1, keepdims=True)
        out = comb * jnp.dot(hid, w2_ref[e], preferred_element_type=F32)
        if move_rows is not None:
            out = out + anchor(d)
        return out

    busy = tv_ref[i] > 0
    after_busy = tv_ref[jnp.maximum(i - 1, 0)] > 0
    drain = jnp.logical_and(jnp.logical_not(busy), jnp.logical_and(i >= 1, after_busy))

    @pl.when(i == 0)
    def _():
        xnext[n_blk] = jnp.zeros(xnext.shape[1:], F32)
        osend[...] = jnp.zeros(osend.shape, F32)
        base = ts_ref[0]

        def body(blk, c):
            for k in range(rows):
                row_in(tok_ref[base + blk * rows + k], blk, k).start()
            return c
        lax.fori_loop(0, n_blk, body, 0)

    @pl.when(jnp.logical_or(busy, drain))
    def _():
        tile_in_wait()
        xcur[...] = xnext[0:n_blk].reshape(tm, xnext.shape[2])

    def hand_over_results():
        @pl.when(i >= 1)
        def _():
            tile_out_wait()
            osend[0:n_blk] = _rms(acc_ref[...], gf_ref[...]).reshape(n_blk, rows, d)

    n_always = 2

    @pl.when(busy)
    def _():
        h = xcur[:, :d]
        zn = _rms(h, g_ref[...]).astype(BF16)
        zn_ref[...] = zn
        nxt = ts_ref[i + 1]

        def gather_rows(part):
            for r in range(part * tm // 2, (part + 1) * tm // 2):
                row_in(tok_ref[nxt + r], r // rows, r % rows).start()
        acc0_ref[...] = h + expert(zn, 0, gather_rows)

    @pl.when(busy)
    def _():
        hand_over_results()

    @pl.when(busy)
    def _():
        prv = ts_ref[jnp.maximum(i - 1, 0)]

        def scatter_rows(part):
            for r in range(part * tm // 2, (part + 1) * tm // 2):
                row_out(tok_ref[prv + r], r // rows, r % rows).start()
        acc_ref[...] = acc0_ref[...] + expert(zn_ref[...], 1, scatter_rows)

    for k in range(n_always, n_eg):
        @pl.when(tn_ref[i] > k)
        def _():
            acc_ref[...] += expert(zn_ref[...], k)

    @pl.when(drain)
    def _():
        hand_over_results()
        base = ts_ref[i - 1]
        nv = tv_ref[i - 1]
        full = nv // rows

        def body(blk, c):
            for k in range(rows):
                row_out(tok_ref[base + blk * rows + k], blk, k).start()
            return c
        lax.fori_loop(0, full, body, 0)
        for k in range(rows - 1):

            @pl.when(full * rows + k < nv)
            def _():
                row_out(tok_ref[base + full * rows + k], full, k).start()

        @pl.when(full > 0)
        def _():
            pltpu.make_async_copy(osend.at[pl.ds(0, full)], osend.at[pl.ds(0, full)], ssem).wait()
        for k in range(rows - 1):

            @pl.when(full * rows + k < nv)
            def _():
                row_out(0, 0, 0).wait()


def _moe(hx, cls, counts, g, w1, w3, w2, g_final, tm):
    t, dx = hx.shape
    d = dx - LANES
    n_eg = EXPERTS_PER_GROUP
    de = w1.shape[2]
    tables = _dispatch_tables(cls, counts, tm)
    n_tiles = tables[2].shape[0]

    def wmap(i, tok, ts, tg, tv, tn, te):
        return (tg[i], 0, 0)

    def cmap(i, tok, ts, tg, tv, tn, te):
        return (0, 0)

    grid_spec = pltpu.PrefetchScalarGridSpec(
        num_scalar_prefetch=len(tables),
        grid=(n_tiles,),
        in_specs=[pl.BlockSpec(memory_space=pl.ANY),
                  pl.BlockSpec((1, d), cmap),
                  pl.BlockSpec((n_eg, d, de), wmap),
                  pl.BlockSpec((n_eg, d, de), wmap),
                  pl.BlockSpec((n_eg, de, d), wmap),
                  pl.BlockSpec((1, d), cmap)],
        out_specs=pl.BlockSpec(memory_space=pl.ANY),
        scratch_shapes=[pltpu.VMEM((tm // SUBLANES + 1, SUBLANES, dx), F32),
                        pltpu.VMEM((tm // SUBLANES + 1, SUBLANES, d), F32),
                        pltpu.VMEM((tm, dx), F32),
                        pltpu.VMEM((tm, d), BF16),
                        pltpu.VMEM((tm, d), F32),
                        pltpu.VMEM((tm, d), F32),
                        pltpu.SemaphoreType.DMA(()), pltpu.SemaphoreType.DMA(())])
    return pl.pallas_call(
        _moe_kernel,
        grid_spec=grid_spec,
        out_shape=jax.ShapeDtypeStruct((t, d), F32),
        compiler_params=pltpu.CompilerParams(dimension_semantics=("arbitrary",),
                                             vmem_limit_bytes=VMEM_LIMIT),
        name="moe",
    )(*tables, hx, g.reshape(1, d), w1.astype(BF16), w3.astype(BF16), w2.astype(BF16),
      g_final.reshape(1, d))


def _layer(h2, mem2, batch, seq, p, l, g_final, tm):
    d = h2.shape[1]
    tm_mix = min(MIXER_TILE, seq)
    n_chunks = tm_mix // SSM_CHUNK
    n_scan = int(math.log2(n_chunks))
    tables = _ssm_tables(p["ssm_a_re"][l], p["ssm_a_im"][l], p["ssm_log_dt"][l], p["ssm_b_re"][l],
                         p["ssm_b_im"][l], p["ssm_c_re"][l], p["ssm_c_im"][l], p["ssm_d"][l],
                         SSM_CHUNK, n_scan, n_chunks // SUBLANES)
    h2 = _mixer(h2, seq, p["norm_mix"][l], p["w_in"][l], p["pool_w"][l], p["pool_scale"][l], tables,
                p["glu_w"][l], p["glu_b"][l], p["br_pool"][l], p["br_ssm"][l], p["w_out"][l],
                tm_mix)

    wqk, wvo = _kv_fold(mem2, batch, p["norm_mem"][l], p["x_wkv"][l], p["x_wq"][l], p["x_wo"][l])

    n_g, n_eg = N_EXPERT_GROUPS, EXPERTS_PER_GROUP
    w_r = jnp.concatenate([p["router_g_w"][l], p["router_e_w"][l].reshape(d, n_g * n_eg)], axis=1)
    w_r = jnp.pad(w_r, ((0, 0), (0, LANES - w_r.shape[1])))
    b_r = jnp.concatenate([p["router_g_b"][l], p["router_e_b"][l].reshape(n_g * n_eg)])
    b_r = jnp.pad(b_r, (0, LANES - b_r.shape[0])).reshape(1, LANES)
    hx, cls, counts = _xattn(h2, seq, p["norm_x"][l], wqk, wvo, p["norm_ffn"][l], w_r, b_r,
                             min(XATTN_TILE, seq))
    return _moe(hx, cls, counts, p["norm_ffn"][l], p["exp_w1"][l], p["exp_w3"][l], p["exp_w2"][l], g_final, tm)


def kernel(x, mem, norm_mix, w_in, pool_w, pool_scale, ssm_a_re, ssm_a_im, ssm_log_dt, ssm_b_re,
           ssm_b_im, ssm_c_re, ssm_c_im, ssm_d, glu_w, glu_b, br_pool, br_ssm, w_out, norm_x,
           norm_mem, x_wq, x_wkv, x_wo, norm_ffn, router_g_w, router_g_b, router_e_w, router_e_b,
           exp_w1, exp_w3, exp_w2, norm_final):
    p = dict(norm_mix=norm_mix, w_in=w_in, pool_w=pool_w, pool_scale=pool_scale, ssm_a_re=ssm_a_re,
             ssm_a_im=ssm_a_im, ssm_log_dt=ssm_log_dt, ssm_b_re=ssm_b_re, ssm_b_im=ssm_b_im,
             ssm_c_re=ssm_c_re, ssm_c_im=ssm_c_im, ssm_d=ssm_d, glu_w=glu_w, glu_b=glu_b,
             br_pool=br_pool, br_ssm=br_ssm, w_out=w_out, norm_x=norm_x, norm_mem=norm_mem,
             x_wq=x_wq, x_wkv=x_wkv, x_wo=x_wo, norm_ffn=norm_ffn, router_g_w=router_g_w,
             router_g_b=router_g_b, router_e_w=router_e_w, router_e_b=router_e_b, exp_w1=exp_w1,
             exp_w3=exp_w3, exp_w2=exp_w2)
    batch, seq, d = x.shape
    assert norm_mix.shape[0] == 1, "final norm is fused into the (single) layer's last stage"
    tm = min(MOE_TILE, seq)
    assert seq % min(MIXER_TILE, seq) == 0 and seq % min(XATTN_TILE, seq) == 0 and (batch * seq) % tm == 0
    h2 = x.reshape(batch * seq, d)
    mem2 = mem.reshape(-1, d)
    out = _layer(h2, mem2, batch, seq, p, 0, norm_final, tm)
    return out.reshape(batch, seq, d)
```

```python
import functools
import math

import jax
import jax.numpy as jnp
from jax import lax
from jax.experimental import pallas as pl
from jax.experimental.pallas import tpu as pltpu

F32 = jnp.float32
BF16 = jnp.bfloat16

RMS_EPS = 1e-6
POOL_WINDOWS = (2, 4, 8, 16)
POOL_HALO = 16
SSM_GROUP_DIM = 16
SSM_STATE = 64
LANES = 128
SUBLANES = 8
OCT = LANES // SSM_GROUP_DIM
SSM_CHUNK = 4
STRAND_PAD = 8
X_HEADS = 4
MIXER_TILE = 1024
XATTN_TILE = 1024
MOE_TILE = 512
DMA_THREADS = 2
N_EXPERT_GROUPS = 4
EXPERTS_PER_GROUP = 4
EXPERT_PAIRS = ((0, 1), (0, 2), (0, 3), (1, 3), (1, 2), (2, 3))
N_CLASS_ROWS = 32
VMEM_LIMIT = 56 * 1024 * 1024


def _rms(x, g):
    return x * lax.rsqrt(jnp.mean(x * x, axis=-1, keepdims=True) + RMS_EPS) * g


def _const_spec(shape):
    nd = len(shape)
    return pl.BlockSpec(shape, lambda *_: (0,) * nd, pipeline_mode=pl.Buffered(1))


def _ssm_tables(a_re, a_im, log_dt, b_re, b_im, c_re, c_im, d_skip, chunk, n_scan, n_pow):
    hi = lax.Precision.HIGHEST
    G, N = a_re.shape
    C = SSM_GROUP_DIM
    Q = G // OCT
    dt = jnp.exp(log_dt)[:, None]
    mag = jnp.exp(a_re * dt)
    lb_re, lb_im = mag * jnp.cos(a_im * dt), mag * jnp.sin(a_im * dt)
    den = a_re * a_re + a_im * a_im
    nr, ni = lb_re - 1.0, lb_im
    f_re = ((nr * a_re + ni * a_im) / den)[..., None]
    f_im = ((ni * a_re - nr * a_im) / den)[..., None]
    bb_re = f_re * b_re - f_im * b_im
    bb_im = f_re * b_im + f_im * b_re

    pw_re, pw_im = [jnp.ones_like(lb_re)], [jnp.zeros_like(lb_re)]
    for _ in range(chunk):
        r, i = pw_re[-1], pw_im[-1]
        pw_re.append(r * lb_re - i * lb_im)
        pw_im.append(r * lb_im + i * lb_re)
    pw_re, pw_im = jnp.stack(pw_re), jnp.stack(pw_im)

    lbb_re = pw_re[:chunk, :, :, None] * bb_re - pw_im[:chunk, :, :, None] * bb_im
    lbb_im = pw_re[:chunk, :, :, None] * bb_im + pw_im[:chunk, :, :, None] * bb_re
    kern = (jnp.einsum('gon,dgnc->dgoc', c_re, lbb_re, precision=hi)
            - jnp.einsum('gon,dgnc->dgoc', c_im, lbb_im, precision=hi))
    kern = kern.at[0].add(d_skip.reshape(G, C)[:, :, None] * jnp.eye(C, dtype=F32))

    lane_g = jnp.arange(LANES) // C
    state_g = jnp.arange(OCT * N) // N
    rep_c = (jnp.arange(C)[:, None] == (jnp.arange(LANES) % C)[None, :]).astype(F32)
    rep_n = (jnp.arange(N)[:, None] == (jnp.arange(OCT * N) % N)[None, :]).astype(F32)
    same_cc = (lane_g[:, None] == lane_g[None, :]).astype(F32)
    same_cn = (lane_g[:, None] == state_g[None, :]).astype(F32)

    kt = kern.transpose(0, 1, 3, 2).reshape(chunk, Q, LANES, C)
    blk = jnp.einsum('dqrc,cl->dqrl', kt, rep_c, precision=hi) * same_cc
    zero = jnp.zeros((Q, LANES, LANES), F32)
    m_intra = jnp.concatenate(
        [jnp.concatenate([blk[jp - j] if jp >= j else zero for jp in range(chunk)], axis=2)
         for j in range(chunk)], axis=1)

    def state_half(lbb):
        z = lbb[::-1].transpose(0, 1, 3, 2).reshape(chunk, Q, LANES, N)
        w = jnp.einsum('jqrn,nl->qjrl', z, rep_n, precision=hi) * same_cn
        return w.reshape(Q, chunk * LANES, OCT * N)
    w_state = jnp.concatenate([state_half(lbb_re), state_half(lbb_im)], axis=-1)

    p_re = c_re[None] * pw_re[1:, :, None, :] - c_im[None] * pw_im[1:, :, None, :]
    p_im = c_re[None] * pw_im[1:, :, None, :] + c_im[None] * pw_re[1:, :, None, :]
    def out_half(p):
        z = p.transpose(0, 1, 3, 2).reshape(chunk, Q, OCT * N, C)
        w = jnp.einsum('jqrc,cl->qrjl', z, rep_c, precision=hi)
        w = w * same_cn.T[None, :, None, :]
        return w.reshape(Q, OCT * N, chunk * LANES)
    w_out = jnp.concatenate([out_half(p_re), out_half(-p_im)], axis=1)

    lr, li = [pw_re[chunk]], [pw_im[chunk]]
    for _ in range(n_scan - 1):
        r, i = lr[-1], li[-1]
        lr.append(r * r - i * i)
        li.append(2.0 * r * i)
    qr, qi = [jnp.ones_like(lb_re)], [jnp.zeros_like(lb_re)]
    for _ in range(n_pow - 1):
        r, i = qr[-1], qi[-1]
        qr.append(r * pw_re[chunk] - i * pw_im[chunk])
        qi.append(r * pw_im[chunk] + i * pw_re[chunk])
    lam_re = jnp.stack(lr + qr).reshape(n_scan + n_pow, G * N)
    lam_im = jnp.stack(li + qi).reshape(n_scan + n_pow, G * N)
    return m_intra.astype(BF16), w_state.astype(BF16), w_out.astype(BF16), lam_re, lam_im


def _mixer_kernel(x_ref, nm_ref, win_ref, poolw_ref, pscale_ref, mi_ref, ws_ref, wh_ref,
                  lre_ref, lim_ref, gluw_ref, glub_ref, brp_ref, brs_ref, wout_ref,
                  o_ref, zext_ref, v_ref, y_ref, carry_ref, *, tiles_per_seq, chunk):
    tm, d_model = x_ref.shape
    pool_w = zext_ref.shape[1]
    n_oct = v_ref.shape[0]
    ssm_w = n_oct * LANES
    n_chunks = tm // chunk
    n_str = SUBLANES
    per = n_chunks // n_str
    span = per * chunk
    pitch = v_ref.shape[1] // n_str
    n_scan = lre_ref.shape[0] - per
    seq_tile = pl.program_id(0) % tiles_per_seq

    @pl.when(seq_tile == 0)
    def _():
        zext_ref[0:POOL_HALO, :] = jnp.zeros((POOL_HALO, pool_w), F32)
        carry_ref[...] = jnp.zeros_like(carry_ref)

    x = x_ref[...]
    u = _rms(x, nm_ref[...]).astype(BF16)
    proj = jnp.dot(u, win_ref[...], preferred_element_type=F32)
    z = proj[:, :pool_w]
    zext_ref[POOL_HALO:, :] = z
    for q in range(n_oct):
        for r in range(n_str):
            v_ref[q, r * pitch:r * pitch + span, :] = (
                proj[r * span:(r + 1) * span, pool_w + q * LANES:pool_w + (q + 1) * LANES])
    g_pool = jax.nn.sigmoid(proj[:, pool_w + ssm_w:pool_w + ssm_w + d_model])
    g_ssm = jax.nn.sigmoid(proj[:, pool_w + ssm_w + d_model:])

    pos = (seq_tile * tm + lax.broadcasted_iota(jnp.int32, (tm, 1), 0) + 1).astype(F32)
    gdim = pool_w // len(POOL_WINDOWS)
    ypool = []
    for gi, w in enumerate(POOL_WINDOWS):
        cols = slice(gi * gdim, (gi + 1) * gdim)
        acc = z[:, cols]
        for dlt in range(1, w):
            acc = acc + zext_ref[POOL_HALO - dlt:POOL_HALO - dlt + tm, cols]
        diff = acc / jnp.minimum(pos, float(w)) - z[:, cols]
        ypool.append(jnp.dot(diff.astype(BF16), poolw_ref[gi], preferred_element_type=F32))
    ypool = jnp.concatenate(ypool, axis=1) * pscale_ref[...]
    zext_ref[0:POOL_HALO, :] = zext_ref[tm:tm + POOL_HALO, :]
    p_br = jnp.dot(ypool.astype(BF16), brp_ref[...], preferred_element_type=F32)

    srow = lax.broadcasted_iota(jnp.int32, (n_str, 1), 0)
    half = OCT * SSM_STATE
    log_per = per.bit_length() - 1
    for q in range(n_oct):
        st = slice(q * half, (q + 1) * half)

        def lam(k):
            return lre_ref[k:k + 1, st], lim_ref[k:k + 1, st]

        vq = jnp.concatenate(
            [jnp.concatenate([v_ref[q, pl.ds(b * chunk + j, n_str, stride=pitch), :]
                              for j in range(chunk)], axis=1) for b in range(per)],
            axis=0).astype(BF16)
        s = jnp.dot(vq, ws_ref[q], preferred_element_type=F32)
        a_re, a_im = lam(0)
        loc_re, loc_im = [s[0:n_str, :half]], [s[0:n_str, half:]]
        for b in range(1, per):
            p_re, p_im = loc_re[-1], loc_im[-1]
            loc_re.append(s[b * n_str:(b + 1) * n_str, :half] + a_re * p_re - a_im * p_im)
            loc_im.append(s[b * n_str:(b + 1) * n_str, half:] + a_re * p_im + a_im * p_re)
        c_re, c_im = carry_ref[0:1, st], carry_ref[1:2, st]
        e_re, e_im = loc_re[-1], loc_im[-1]
        l_re, l_im = lam(log_per)
        e_re = e_re + jnp.where(srow == 0, l_re * c_re - l_im * c_im, 0.0)
        e_im = e_im + jnp.where(srow == 0, l_re * c_im + l_im * c_re, 0.0)
        for k in range(n_str.bit_length() - 1):
            sh = 1 << k
            r_re = jnp.where(srow >= sh, pltpu.roll(e_re, sh, axis=0), 0.0)
            r_im = jnp.where(srow >= sh, pltpu.roll(e_im, sh, axis=0), 0.0)
            l_re, l_im = lam(log_per + k)
            e_re, e_im = e_re + l_re * r_re - l_im * r_im, e_im + l_re * r_im + l_im * r_re
        carry_ref[0:1, st] = e_re[n_str - 1:n_str, :]
        carry_ref[1:2, st] = e_im[n_str - 1:n_str, :]
        i_re = jnp.where(srow == 0, c_re, pltpu.roll(e_re, 1, axis=0))
        i_im = jnp.where(srow == 0, c_im, pltpu.roll(e_im, 1, axis=0))
        h0_re, h0_im = [i_re], [i_im]
        for b in range(1, per):
            l_re, l_im = lam(n_scan + b)
            h0_re.append(loc_re[b - 1] + l_re * i_re - l_im * i_im)
            h0_im.append(loc_im[b - 1] + l_re * i_im + l_im * i_re)
        h0_re = jnp.concatenate(h0_re, axis=0).astype(BF16)
        h0_im = jnp.concatenate(h0_im, axis=0).astype(BF16)
        yq = (jnp.dot(vq, mi_ref[q], preferred_element_type=F32)
              + jnp.dot(h0_re, wh_ref[q, :half, :], preferred_element_type=F32)
              + jnp.dot(h0_im, wh_ref[q, half:, :], preferred_element_type=F32))
        for b in range(per):
            for j in range(chunk):
                y_ref[q, pl.ds(b * chunk + j, n_str, stride=pitch), :] = (
                    yq[b * n_str:(b + 1) * n_str, j * LANES:(j + 1) * LANES])

    y = jnp.concatenate(
        [jnp.concatenate([y_ref[q, r * pitch:r * pitch + span, :] for r in range(n_str)], axis=0)
         for q in range(n_oct)], axis=1)
    ys = jax.nn.gelu(y, approximate=True)
    glu = jnp.dot(ys.astype(BF16), gluw_ref[...], preferred_element_type=F32) + glub_ref[...]
    ys = ys * jax.nn.sigmoid(glu)
    s_br = jnp.dot(ys.astype(BF16), brs_ref[...], preferred_element_type=F32)

    merged = g_pool * p_br + g_ssm * s_br
    o_ref[...] = x + jnp.dot(merged.astype(BF16), wout_ref[...], preferred_element_type=F32)


def _mixer(x2, seq, nm, w_in, pool_w, pool_scale, tables, glu_w, glu_b, br_pool, br_ssm, w_out, tm):
    t, d = x2.shape
    mi, ws, wh, lre, lim = tables
    pool_width = pool_w.shape[0] * pool_w.shape[1]
    ssm_width = glu_w.shape[0]
    consts = [nm.reshape(1, d), w_in.astype(BF16), pool_w.astype(BF16),
              pool_scale.reshape(1, pool_width), mi, ws, wh, lre, lim,
              glu_w.astype(BF16), glu_b.reshape(1, ssm_width), br_pool.astype(BF16),
              br_ssm.astype(BF16), w_out.astype(BF16)]
    kern = functools.partial(_mixer_kernel, tiles_per_seq=seq // tm, chunk=SSM_CHUNK)
    return pl.pallas_call(
        kern,
        grid=(t // tm,),
        in_specs=[pl.BlockSpec((tm, d), lambda i: (i, 0))] + [_const_spec(c.shape) for c in consts],
        out_specs=pl.BlockSpec((tm, d), lambda i: (i, 0)),
        out_shape=jax.ShapeDtypeStruct((t, d), F32),
        scratch_shapes=[pltpu.VMEM((POOL_HALO + tm, pool_width), F32),
                        pltpu.VMEM((ssm_width // LANES, tm + SUBLANES * STRAND_PAD, LANES), F32),
                        pltpu.VMEM((ssm_width // LANES, tm + SUBLANES * STRAND_PAD, LANES), F32),
                        pltpu.VMEM((2, lre.shape[1]), F32)],
        compiler_params=pltpu.CompilerParams(dimension_semantics=("arbitrary",),
                                             vmem_limit_bytes=VMEM_LIMIT),
        name="mixer",
    )(x2, *consts)


def _kv_fold_kernel(mem_ref, g_ref, wkv_ref, wq_ref, wo_ref, wqk_ref, wvo_ref):
    m, d = mem_ref.shape
    hd = d // X_HEADS
    mn = _rms(mem_ref[...], g_ref[...]).astype(BF16)
    kv = jnp.dot(mn, wkv_ref[...], preferred_element_type=F32)
    for h in range(X_HEADS):
        k_h = kv[:, h * hd:(h + 1) * hd].astype(BF16)
        v_h = kv[:, d + h * hd:d + (h + 1) * hd].astype(BF16)
        qk = lax.dot_general(wq_ref[:, h * hd:(h + 1) * hd], k_h, (((1,), (1,)), ((), ())),
                             preferred_element_type=F32)
        wqk_ref[0, :, h * m:(h + 1) * m] = qk.astype(BF16)
        wvo_ref[0, h * m:(h + 1) * m, :] = jnp.dot(
            v_h, wo_ref[h * hd:(h + 1) * hd, :], preferred_element_type=F32).astype(BF16)


def _kv_fold(mem2, batch, g, w_kv, w_q, w_o):
    d = mem2.shape[1]
    m = mem2.shape[0] // batch
    hm = X_HEADS * m
    return pl.pallas_call(
        _kv_fold_kernel,
        grid=(batch,),
        in_specs=[pl.BlockSpec((m, d), lambda b: (b, 0)), _const_spec((1, d)),
                  _const_spec(w_kv.shape), _const_spec(w_q.shape), _const_spec(w_o.shape)],
        out_specs=[pl.BlockSpec((1, d, hm), lambda b: (b, 0, 0)),
                   pl.BlockSpec((1, hm, d), lambda b: (b, 0, 0))],
        out_shape=[jax.ShapeDtypeStruct((batch, d, hm), BF16),
                   jax.ShapeDtypeStruct((batch, hm, d), BF16)],
        compiler_params=pltpu.CompilerParams(dimension_semantics=("arbitrary",),
                                             vmem_limit_bytes=VMEM_LIMIT),
        name="kv_fold",
    )(mem2, g.reshape(1, d), w_kv.astype(BF16), w_q.astype(BF16), w_o.astype(BF16))


def _route(logits):
    n_g, n_eg = N_EXPERT_GROUPS, EXPERTS_PER_GROUP
    tm = logits.shape[0]
    lt = logits.T
    neg = jnp.float32(-jnp.inf)
    big = jnp.int32(1 << 20)

    def first_max(v):
        row = lax.broadcasted_iota(jnp.int32, v.shape, 0)
        m = jnp.max(v, axis=0, keepdims=True)
        return m, jnp.min(jnp.where(v == m, row, big), axis=0, keepdims=True)

    gl = lt[0:n_g]
    gmax, g_idx = first_max(gl)
    g_gate = 1.0 / jnp.sum(jnp.exp(gl - gmax), axis=0, keepdims=True)
    el = lt[n_g:n_g + n_eg]
    for g in range(1, n_g):
        el = jnp.where(g_idx == g, lt[n_g + g * n_eg:n_g + (g + 1) * n_eg], el)
    row = lax.broadcasted_iota(jnp.int32, el.shape, 0)
    t1, i1 = first_max(el)
    t2, i2 = first_max(jnp.where(row == i1, neg, el))
    e2 = jnp.exp(t2 - t1)
    w1 = 1.0 / (1.0 + e2)
    w2 = e2 / (1.0 + e2)
    comb = jnp.where(row == i1, w1, jnp.where(row == i2, w2, 0.0)) * g_gate
    lo, hi = jnp.minimum(i1, i2), jnp.maximum(i1, i2)
    pair = jnp.zeros_like(lo)
    for idx, (e_lo, e_hi) in enumerate(EXPERT_PAIRS):
        pair = jnp.where((lo == e_lo) & (hi == e_hi), idx, pair)
    cls = g_idx * len(EXPERT_PAIRS) + pair
    info = jnp.concatenate([comb, jnp.zeros((LANES - n_eg, tm), F32)], axis=0)
    hist = jnp.sum((cls == lax.broadcasted_iota(jnp.int32, (N_CLASS_ROWS, tm), 0)).astype(F32),
                   axis=1, keepdims=True)
    return info.T, cls, hist


def _dot_bf16x3(x, w_hilo):
    n = w_hilo.shape[1] // 2
    x_hi = x.astype(BF16)
    x_lo = (x - x_hi.astype(F32)).astype(BF16)
    both = jnp.dot(x_hi, w_hilo, preferred_element_type=F32)
    return both[:, :n] + both[:, n:] + jnp.dot(x_lo, w_hilo[:, :n], preferred_element_type=F32)


def _xattn_kernel(h_ref, g_ref, wqk_ref, wvo_ref, gffn_ref, wr_ref, br_ref, o_ref, cls_ref, hist_ref,
                  *, scale):
    h = h_ref[...]
    d = h.shape[1]
    m = wqk_ref.shape[2] // X_HEADS
    hn = _rms(h, g_ref[...]).astype(BF16)
    s_all = jnp.dot(hn, wqk_ref[0], preferred_element_type=F32) * scale
    probs = []
    for hh in range(X_HEADS):
        s = s_all[:, hh * m:(hh + 1) * m]
        s = s - jnp.max(s, axis=-1, keepdims=True)
        p = jnp.exp(s)
        probs.append((p / jnp.sum(p, axis=-1, keepdims=True)).astype(BF16))
    h_out = h + jnp.dot(jnp.concatenate(probs, axis=1), wvo_ref[0], preferred_element_type=F32)
    zn = _rms(h_out, gffn_ref[...])
    logits = _dot_bf16x3(zn, wr_ref[...]) + br_ref[...]
    route, cls, hist = _route(logits)
    o_ref[:, :d] = h_out
    o_ref[:, d:] = route
    cls_ref[0] = cls
    hist_ref[0] = jnp.broadcast_to(hist, hist_ref.shape[1:])


def _xattn(h2, seq, g, wqk, wvo, g_ffn, w_r, b_r, tm):
    t, d = h2.shape
    tiles_per_seq = seq // tm
    n_tiles = t // tm
    kern = functools.partial(_xattn_kernel, scale=(d // X_HEADS) ** -0.5)
    w_r_hi = w_r.astype(BF16)
    w_r_lo = (w_r - w_r_hi.astype(F32)).astype(BF16)
    w_r_hilo = jnp.concatenate([w_r_hi, w_r_lo], axis=1)
    hx, cls, hist = pl.pallas_call(
        kern,
        grid=(n_tiles,),
        in_specs=[pl.BlockSpec((tm, d), lambda i: (i, 0)), _const_spec((1, d)),
                  pl.BlockSpec((1,) + wqk.shape[1:], lambda i: (i // tiles_per_seq, 0, 0)),
                  pl.BlockSpec((1,) + wvo.shape[1:], lambda i: (i // tiles_per_seq, 0, 0)),
                  _const_spec((1, d)), _const_spec(w_r_hilo.shape), _const_spec(b_r.shape)],
        out_specs=[pl.BlockSpec((tm, d + LANES), lambda i: (i, 0)),
                   pl.BlockSpec((1, 1, tm), lambda i: (i, 0, 0)),
                   pl.BlockSpec((1, N_CLASS_ROWS, LANES), lambda i: (i, 0, 0))],
        out_shape=[jax.ShapeDtypeStruct((t, d + LANES), F32),
                   jax.ShapeDtypeStruct((n_tiles, 1, tm), jnp.int32),
                   jax.ShapeDtypeStruct((n_tiles, N_CLASS_ROWS, LANES), F32)],
        compiler_params=pltpu.CompilerParams(dimension_semantics=("arbitrary",),
                                             vmem_limit_bytes=VMEM_LIMIT),
        name="xattn",
    )(h2, g.reshape(1, d), wqk, wvo, g_ffn.reshape(1, d), w_r_hilo, b_r)
    return hx, cls.reshape(t), jnp.sum(hist[:, :, 0], axis=0).astype(jnp.int32)


def _dispatch_tables(cls, counts, tm):
    t = cls.shape[0]
    n_g, n_eg, n_pair = N_EXPERT_GROUPS, EXPERTS_PER_GROUP, len(EXPERT_PAIRS)
    n_tiles = t // tm + n_g
    i32 = jnp.int32
    _, tok_sorted = lax.sort((cls, jnp.arange(t, dtype=i32)), num_keys=1, is_stable=True)
    tok_sorted = jnp.concatenate([tok_sorted, jnp.zeros((tm,), i32)])
    counts = counts[:n_g * n_pair]
    c_start = jnp.cumsum(counts) - counts
    g_count = counts.reshape(n_g, n_pair).sum(axis=1)
    g_start = jnp.cumsum(g_count) - g_count
    tiles_g = (g_count + tm - 1) // tm
    tile_end = jnp.cumsum(tiles_g)
    tile_first = tile_end - tiles_g
    tile_ids = jnp.arange(n_tiles, dtype=i32)
    tile_group = jnp.minimum(jnp.sum((tile_ids[:, None] >= tile_end[None, :]).astype(i32), axis=1),
                             n_g - 1)
    k_in_group = tile_ids - tile_first[tile_group]
    tile_valid = jnp.clip(g_count[tile_group] - k_in_group * tm, 0, tm)
    tile_valid = jnp.where(tile_ids < tile_end[-1], tile_valid, 0)
    tile_start = jnp.where(tile_valid > 0, g_start[tile_group] + k_in_group * tm, 0)
    lo, hi = tile_start[:, None], (tile_start + tile_valid)[:, None]
    inter = (counts[None, :] > 0) & (c_start[None, :] < hi) & ((c_start + counts)[None, :] > lo)
    inter = inter.reshape(n_tiles, n_g, n_pair).any(axis=1)
    pair_has = jnp.array([[e in p for e in range(n_eg)] for p in EXPERT_PAIRS])
    need = (inter[:, :, None] & pair_has[None]).any(axis=1)
    tile_nexp = need.sum(axis=1)
    tile_exp = jnp.argsort(jnp.logical_not(need), axis=1, stable=True)
    return (tok_sorted, tile_start.astype(i32), tile_group.astype(i32), tile_valid.astype(i32),
            tile_nexp.astype(i32), tile_exp.reshape(-1).astype(i32))


def _moe_kernel(tok_ref, ts_ref, tg_ref, tv_ref, tn_ref, te_ref, hx_hbm, g_ref, w1_ref, w3_ref, w2_ref,
                gf_ref, out_hbm, xnext, osend, xcur, zn_ref, acc0_ref, acc_ref, gsem, ssem):
    i = pl.program_id(0)
    n_blk, rows = xnext.shape[0] - 1, xnext.shape[1]
    tm = n_blk * rows
    d = osend.shape[2]
    n_eg = w1_ref.shape[0]

    def row_in(tok, blk, k):
        return pltpu.make_async_copy(hx_hbm.at[pl.ds(tok, 1), :], xnext.at[blk, pl.ds(k, 1), :], gsem)

    def row_out(tok, blk, k):
        return pltpu.make_async_copy(osend.at[blk, pl.ds(k, 1), :], out_hbm.at[pl.ds(tok, 1), :], ssem)

    def tile_in_wait():
        pltpu.make_async_copy(xnext.at[pl.ds(0, n_blk)], xnext.at[pl.ds(0, n_blk)], gsem).wait()

    def tile_out_wait():
        pltpu.make_async_copy(osend.at[pl.ds(0, n_blk)], osend.at[pl.ds(0, n_blk)], ssem).wait()

    def anchor(width):
        zero = xnext[n_blk, 0:1, 0:width]
        osend[n_blk, 0:1, 0:width] = zero
        return zero + osend[n_blk, 0:1, 0:width]

    def expert(zn, k, move_rows=None):
        e = te_ref[i * n_eg + k]
        if move_rows is not None:
            move_rows(0)
        a = jnp.dot(zn, w1_ref[e], preferred_element_type=F32)
        if move_rows is not None:
            a = a + anchor(a.shape[1])
            move_rows(1)
        b = jnp.dot(zn, w3_ref[e], preferred_element_type=F32)
        hid = (a * jax.nn.sigmoid(a) * b).astype(BF16)
        route = xcur[:, d:]
        lane = lax.broadcasted_iota(jnp.int32, route.shape, 1)
        comb = jnp.sum(jnp.where(lane == e, route, 0.0), axis=1, keepdims=True)
        out = comb * jnp.dot(hid, w2_ref[e], preferred_element_type=F32)
        if move_rows is not None:
            out = out + anchor(d)
        return out

    busy = tv_ref[i] > 0
    after_busy = tv_ref[jnp.maximum(i - 1, 0)] > 0
    drain = jnp.logical_and(jnp.logical_not(busy), jnp.logical_and(i >= 1, after_busy))

    @pl.when(i == 0)
    def _():
        xnext[n_blk] = jnp.zeros(xnext.shape[1:], F32)
        osend[...] = jnp.zeros(osend.shape, F32)
        base = ts_ref[0]

        def body(blk, c):
            for k in range(rows):
                row_in(tok_ref[base + blk * rows + k], blk, k).start(priority=k % DMA_THREADS)
            return c
        lax.fori_loop(0, n_blk, body, 0)

    @pl.when(jnp.logical_or(busy, drain))
    def _():
        tile_in_wait()
        xcur[...] = xnext[0:n_blk].reshape(tm, xnext.shape[2])

    def hand_over_results():
        @pl.when(i >= 1)
        def _():
            tile_out_wait()
            osend[0:n_blk] = _rms(acc_ref[...], gf_ref[...]).reshape(n_blk, rows, d)

    n_always = 2

    @pl.when(busy)
    def _():
        h = xcur[:, :d]
        zn = _rms(h, g_ref[...]).astype(BF16)
        zn_ref[...] = zn
        nxt = ts_ref[i + 1]

        def gather_rows(part):
            for r in range(part * tm // 2, (part + 1) * tm // 2):
                row_in(tok_ref[nxt + r], r // rows, r % rows).start(priority=r % DMA_THREADS)
        acc0_ref[...] = h + expert(zn, 0, gather_rows)

    @pl.when(busy)
    def _():
        hand_over_results()

    @pl.when(busy)
    def _():
        prv = ts_ref[jnp.maximum(i - 1, 0)]

        def scatter_rows(part):
            for r in range(part * tm // 2, (part + 1) * tm // 2):
                row_out(tok_ref[prv + r], r // rows, r % rows).start(priority=r % DMA_THREADS)
        acc_ref[...] = acc0_ref[...] + expert(zn_ref[...], 1, scatter_rows)

    for k in range(n_always, n_eg):
        @pl.when(tn_ref[i] > k)
        def _():
            acc_ref[...] += expert(zn_ref[...], k)

    @pl.when(drain)
    def _():
        hand_over_results()
        base = ts_ref[i - 1]
        nv = tv_ref[i - 1]
        full = nv // rows

        def body(blk, c):
            for k in range(rows):
                row_out(tok_ref[base + blk * rows + k], blk, k).start(priority=k % DMA_THREADS)
            return c
        lax.fori_loop(0, full, body, 0)
        for k in range(rows - 1):

            @pl.when(full * rows + k < nv)
            def _():
                row_out(tok_ref[base + full * rows + k], full, k).start(priority=k % DMA_THREADS)

        @pl.when(full > 0)
        def _():
            pltpu.make_async_copy(osend.at[pl.ds(0, full)], osend.at[pl.ds(0, full)], ssem).wait()
        for k in range(rows - 1):

            @pl.when(full * rows + k < nv)
            def _():
                row_out(0, 0, 0).wait()


def _moe(hx, cls, counts, g, w1, w3, w2, g_final, tm):
    t, dx = hx.shape
    d = dx - LANES
    n_eg = EXPERTS_PER_GROUP
    de = w1.shape[2]
    tables = _dispatch_tables(cls, counts, tm)
    n_tiles = tables[2].shape[0]

    def wmap(i, tok, ts, tg, tv, tn, te):
        return (tg[i], 0, 0)

    def cmap(i, tok, ts, tg, tv, tn, te):
        return (0, 0)

    grid_spec = pltpu.PrefetchScalarGridSpec(
        num_scalar_prefetch=len(tables),
        grid=(n_tiles,),
        in_specs=[pl.BlockSpec(memory_space=pl.ANY),
                  pl.BlockSpec((1, d), cmap),
                  pl.BlockSpec((n_eg, d, de), wmap),
                  pl.BlockSpec((n_eg, d, de), wmap),
                  pl.BlockSpec((n_eg, de, d), wmap),
                  pl.BlockSpec((1, d), cmap)],
        out_specs=pl.BlockSpec(memory_space=pl.ANY),
        scratch_shapes=[pltpu.VMEM((tm // SUBLANES + 1, SUBLANES, dx), F32),
                        pltpu.VMEM((tm // SUBLANES + 1, SUBLANES, d), F32),
                        pltpu.VMEM((tm, dx), F32),
                        pltpu.VMEM((tm, d), BF16),
                        pltpu.VMEM((tm, d), F32),
                        pltpu.VMEM((tm, d), F32),
                        pltpu.SemaphoreType.DMA(()), pltpu.SemaphoreType.DMA(())])
    return pl.pallas_call(
        _moe_kernel,
        grid_spec=grid_spec,
        out_shape=jax.ShapeDtypeStruct((t, d), F32),
        compiler_params=pltpu.CompilerParams(dimension_semantics=("arbitrary",),
                                             vmem_limit_bytes=VMEM_LIMIT),
        name="moe",
    )(*tables, hx, g.reshape(1, d), w1.astype(BF16), w3.astype(BF16), w2.astype(BF16),
      g_final.reshape(1, d))


def _layer(h2, mem2, batch, seq, p, l, g_final, tm):
    d = h2.shape[1]
    tm_mix = min(MIXER_TILE, seq)
    n_chunks = tm_mix // SSM_CHUNK
    n_scan = int(math.log2(n_chunks))
    tables = _ssm_tables(p["ssm_a_re"][l], p["ssm_a_im"][l], p["ssm_log_dt"][l], p["ssm_b_re"][l],
                         p["ssm_b_im"][l], p["ssm_c_re"][l], p["ssm_c_im"][l], p["ssm_d"][l],
                         SSM_CHUNK, n_scan, n_chunks // SUBLANES)
    h2 = _mixer(h2, seq, p["norm_mix"][l], p["w_in"][l], p["pool_w"][l], p["pool_scale"][l], tables,
                p["glu_w"][l], p["glu_b"][l], p["br_pool"][l], p["br_ssm"][l], p["w_out"][l],
                tm_mix)

    wqk, wvo = _kv_fold(mem2, batch, p["norm_mem"][l], p["x_wkv"][l], p["x_wq"][l], p["x_wo"][l])

    n_g, n_eg = N_EXPERT_GROUPS, EXPERTS_PER_GROUP
    w_r = jnp.concatenate([p["router_g_w"][l], p["router_e_w"][l].reshape(d, n_g * n_eg)], axis=1)
    w_r = jnp.pad(w_r, ((0, 0), (0, LANES - w_r.shape[1])))
    b_r = jnp.concatenate([p["router_g_b"][l], p["router_e_b"][l].reshape(n_g * n_eg)])
    b_r = jnp.pad(b_r, (0, LANES - b_r.shape[0])).reshape(1, LANES)
    hx, cls, counts = _xattn(h2, seq, p["norm_x"][l], wqk, wvo, p["norm_ffn"][l], w_r, b_r,
                             min(XATTN_TILE, seq))
    return _moe(hx, cls, counts, p["norm_ffn"][l], p["exp_w1"][l], p["exp_w3"][l], p["exp_w2"][l], g_final, tm)


def kernel(x, mem, norm_mix, w_in, pool_w, pool_scale, ssm_a_re, ssm_a_im, ssm_log_dt, ssm_b_re,
           ssm_b_im, ssm_c_re, ssm_c_im, ssm_d, glu_w, glu_b, br_pool, br_ssm, w_out, norm_x,
           norm_mem, x_wq, x_wkv, x_wo, norm_ffn, router_g_w, router_g_b, router_e_w, router_e_b,
           exp_w1, exp_w3, exp_w2, norm_final):
    p = dict(norm_mix=norm_mix, w_in=w_in, pool_w=pool_w, pool_scale=pool_scale, ssm_a_re=ssm_a_re,
             ssm_a_im=ssm_a_im, ssm_log_dt=ssm_log_dt, ssm_b_re=ssm_b_re, ssm_b_im=ssm_b_im,
             ssm_c_re=ssm_c_re, ssm_c_im=ssm_c_im, ssm_d=ssm_d, glu_w=glu_w, glu_b=glu_b,
             br_pool=br_pool, br_ssm=br_ssm, w_out=w_out, norm_x=norm_x, norm_mem=norm_mem,
             x_wq=x_wq, x_wkv=x_wkv, x_wo=x_wo, norm_ffn=norm_ffn, router_g_w=router_g_w,
             router_g_b=router_g_b, router_e_w=router_e_w, router_e_b=router_e_b, exp_w1=exp_w1,
             exp_w3=exp_w3, exp_w2=exp_w2)
    batch, seq, d = x.shape
    assert norm_mix.shape[0] == 1, "final norm is fused into the (single) layer's last stage"
    tm = min(MOE_TILE, seq)
    assert seq % min(MIXER_TILE, seq) == 0 and seq % min(XATTN_TILE, seq) == 0 and (batch * seq) % tm == 0
    h2 = x.reshape(batch * seq, d)
    mem2 = mem.reshape(-1, d)
    out = _layer(h2, mem2, batch, seq, p, 0, norm_final, tm)
    return out.reshape(batch, seq, d)
```

```python
import functools
import math

import jax
import jax.numpy as jnp
from jax import lax
from jax.experimental import pallas as pl
from jax.experimental.pallas import tpu as pltpu

F32 = jnp.float32
BF16 = jnp.bfloat16

RMS_EPS = 1e-6
POOL_WINDOWS = (2, 4, 8, 16)
POOL_HALO = 16
SSM_GROUP_DIM = 16
SSM_STATE = 64
LANES = 128
SUBLANES = 8
OCT = LANES // SSM_GROUP_DIM
SSM_CHUNK = 4
STRAND_PAD = 8
X_HEADS = 4
MIXER_TILE = 1024
XATTN_TILE = 1024
MOE_TILE = 512
N_EXPERT_GROUPS = 4
EXPERTS_PER_GROUP = 4
EXPERT_PAIRS = ((0, 1), (0, 2), (0, 3), (1, 3), (1, 2), (2, 3))
N_CLASS_ROWS = 32
VMEM_LIMIT = 56 * 1024 * 1024


def _rms(x, g):
    return x * lax.rsqrt(jnp.mean(x * x, axis=-1, keepdims=True) + RMS_EPS) * g


def _const_spec(shape):
    nd = len(shape)
    return pl.BlockSpec(shape, lambda *_: (0,) * nd, pipeline_mode=pl.Buffered(1))


def _ssm_tables(a_re, a_im, log_dt, b_re, b_im, c_re, c_im, d_skip, chunk, n_scan, n_pow):
    hi = lax.Precision.HIGHEST
    G, N = a_re.shape
    C = SSM_GROUP_DIM
    Q = G // OCT
    dt = jnp.exp(log_dt)[:, None]
    mag = jnp.exp(a_re * dt)
    lb_re, lb_im = mag * jnp.cos(a_im * dt), mag * jnp.sin(a_im * dt)
    den = a_re * a_re + a_im * a_im
    nr, ni = lb_re - 1.0, lb_im
    f_re = ((nr * a_re + ni * a_im) / den)[..., None]
    f_im = ((ni * a_re - nr * a_im) / den)[..., None]
    bb_re = f_re * b_re - f_im * b_im
    bb_im = f_re * b_im + f_im * b_re

    pw_re, pw_im = [jnp.ones_like(lb_re)], [jnp.zeros_like(lb_re)]
    for _ in range(chunk):
        r, i = pw_re[-1], pw_im[-1]
        pw_re.append(r * lb_re - i * lb_im)
        pw_im.append(r * lb_im + i * lb_re)
    pw_re, pw_im = jnp.stack(pw_re), jnp.stack(pw_im)

    lbb_re = pw_re[:chunk, :, :, None] * bb_re - pw_im[:chunk, :, :, None] * bb_im
    lbb_im = pw_re[:chunk, :, :, None] * bb_im + pw_im[:chunk, :, :, None] * bb_re
    kern = (jnp.einsum('gon,dgnc->dgoc', c_re, lbb_re, precision=hi)
            - jnp.einsum('gon,dgnc->dgoc', c_im, lbb_im, precision=hi))
    kern = kern.at[0].add(d_skip.reshape(G, C)[:, :, None] * jnp.eye(C, dtype=F32))

    lane_g = jnp.arange(LANES) // C
    state_g = jnp.arange(OCT * N) // N
    rep_c = (jnp.arange(C)[:, None] == (jnp.arange(LANES) % C)[None, :]).astype(F32)
    rep_n = (jnp.arange(N)[:, None] == (jnp.arange(OCT * N) % N)[None, :]).astype(F32)
    same_cc = (lane_g[:, None] == lane_g[None, :]).astype(F32)
    same_cn = (lane_g[:, None] == state_g[None, :]).astype(F32)

    kt = kern.transpose(0, 1, 3, 2).reshape(chunk, Q, LANES, C)
    blk = jnp.einsum('dqrc,cl->dqrl', kt, rep_c, precision=hi) * same_cc
    zero = jnp.zeros((Q, LANES, LANES), F32)
    m_intra = jnp.concatenate(
        [jnp.concatenate([blk[jp - j] if jp >= j else zero for jp in range(chunk)], axis=2)
         for j in range(chunk)], axis=1)

    def state_half(lbb):
        z = lbb[::-1].transpose(0, 1, 3, 2).reshape(chunk, Q, LANES, N)
        w = jnp.einsum('jqrn,nl->qjrl', z, rep_n, precision=hi) * same_cn
        return w.reshape(Q, chunk * LANES, OCT * N)
    w_state = jnp.concatenate([state_half(lbb_re), state_half(lbb_im)], axis=-1)

    p_re = c_re[None] * pw_re[1:, :, None, :] - c_im[None] * pw_im[1:, :, None, :]
    p_im = c_re[None] * pw_im[1:, :, None, :] + c_im[None] * pw_re[1:, :, None, :]
    def out_half(p):
        z = p.transpose(0, 1, 3, 2).reshape(chunk, Q, OCT * N, C)
        w = jnp.einsum('jqrc,cl->qrjl', z, rep_c, precision=hi)
        w = w * same_cn.T[None, :, None, :]
        return w.reshape(Q, OCT * N, chunk * LANES)
    w_out = jnp.concatenate([out_half(p_re), out_half(-p_im)], axis=1)

    lr, li = [pw_re[chunk]], [pw_im[chunk]]
    for _ in range(n_scan - 1):
        r, i = lr[-1], li[-1]
        lr.append(r * r - i * i)
        li.append(2.0 * r * i)
    qr, qi = [jnp.ones_like(lb_re)], [jnp.zeros_like(lb_re)]
    for _ in range(n_pow - 1):
        r, i = qr[-1], qi[-1]
        qr.append(r * pw_re[chunk] - i * pw_im[chunk])
        qi.append(r * pw_im[chunk] + i * pw_re[chunk])
    lam_re = jnp.stack(lr + qr).reshape(n_scan + n_pow, G * N)
    lam_im = jnp.stack(li + qi).reshape(n_scan + n_pow, G * N)
    return m_intra.astype(BF16), w_state.astype(BF16), w_out.astype(BF16), lam_re, lam_im


def _mixer_kernel(x_ref, nm_ref, win_ref, poolw_ref, pscale_ref, mi_ref, ws_ref, wh_ref,
                  lre_ref, lim_ref, gluw_ref, glub_ref, brp_ref, brs_ref, wout_ref,
                  o_ref, zext_ref, v_ref, y_ref, carry_ref, *, tiles_per_seq, chunk):
    tm, d_model = x_ref.shape
    pool_w = zext_ref.shape[1]
    n_oct = v_ref.shape[0]
    ssm_w = n_oct * LANES
    n_chunks = tm // chunk
    n_str = SUBLANES
    per = n_chunks // n_str
    span = per * chunk
    pitch = v_ref.shape[1] // n_str
    n_scan = lre_ref.shape[0] - per
    seq_tile = pl.program_id(0) % tiles_per_seq

    @pl.when(seq_tile == 0)
    def _():
        zext_ref[0:POOL_HALO, :] = jnp.zeros((POOL_HALO, pool_w), F32)
        carry_ref[...] = jnp.zeros_like(carry_ref)

    x = x_ref[...]
    u = _rms(x, nm_ref[...]).astype(BF16)
    proj = jnp.dot(u, win_ref[...], preferred_element_type=F32)
    z = proj[:, :pool_w]
    zext_ref[POOL_HALO:, :] = z
    for q in range(n_oct):
        for r in range(n_str):
            v_ref[q, r * pitch:r * pitch + span, :] = (
                proj[r * span:(r + 1) * span, pool_w + q * LANES:pool_w + (q + 1) * LANES])
    g_pool = jax.nn.sigmoid(proj[:, pool_w + ssm_w:pool_w + ssm_w + d_model])
    g_ssm = jax.nn.sigmoid(proj[:, pool_w + ssm_w + d_model:])

    pos = (seq_tile * tm + lax.broadcasted_iota(jnp.int32, (tm, 1), 0) + 1).astype(F32)
    gdim = pool_w // len(POOL_WINDOWS)
    ypool = []
    for gi, w in enumerate(POOL_WINDOWS):
        cols = slice(gi * gdim, (gi + 1) * gdim)
        acc = z[:, cols]
        for dlt in range(1, w):
            acc = acc + zext_ref[POOL_HALO - dlt:POOL_HALO - dlt + tm, cols]
        diff = acc / jnp.minimum(pos, float(w)) - z[:, cols]
        ypool.append(jnp.dot(diff.astype(BF16), poolw_ref[gi], preferred_element_type=F32))
    ypool = jnp.concatenate(ypool, axis=1) * pscale_ref[...]
    zext_ref[0:POOL_HALO, :] = zext_ref[tm:tm + POOL_HALO, :]
    p_br = jnp.dot(ypool.astype(BF16), brp_ref[...], preferred_element_type=F32)

    srow = lax.broadcasted_iota(jnp.int32, (n_str, 1), 0)
    half = OCT * SSM_STATE
    log_per = per.bit_length() - 1
    for q in range(n_oct):
        st = slice(q * half, (q + 1) * half)

        def lam(k):
            return lre_ref[k:k + 1, st], lim_ref[k:k + 1, st]

        vq = jnp.concatenate(
            [jnp.concatenate([v_ref[q, pl.ds(b * chunk + j, n_str, stride=pitch), :]
                              for j in range(chunk)], axis=1) for b in range(per)],
            axis=0).astype(BF16)
        s = jnp.dot(vq, ws_ref[q], preferred_element_type=F32)
        a_re, a_im = lam(0)
        loc_re, loc_im = [s[0:n_str, :half]], [s[0:n_str, half:]]
        for b in range(1, per):
            p_re, p_im = loc_re[-1], loc_im[-1]
            loc_re.append(s[b * n_str:(b + 1) * n_str, :half] + a_re * p_re - a_im * p_im)
            loc_im.append(s[b * n_str:(b + 1) * n_str, half:] + a_re * p_im + a_im * p_re)
        c_re, c_im = carry_ref[0:1, st], carry_ref[1:2, st]
        e_re, e_im = loc_re[-1], loc_im[-1]
        l_re, l_im = lam(log_per)
        e_re = e_re + jnp.where(srow == 0, l_re * c_re - l_im * c_im, 0.0)
        e_im = e_im + jnp.where(srow == 0, l_re * c_im + l_im * c_re, 0.0)
        for k in range(n_str.bit_length() - 1):
            sh = 1 << k
            r_re = jnp.where(srow >= sh, pltpu.roll(e_re, sh, axis=0), 0.0)
            r_im = jnp.where(srow >= sh, pltpu.roll(e_im, sh, axis=0), 0.0)
            l_re, l_im = lam(log_per + k)
            e_re, e_im = e_re + l_re * r_re - l_im * r_im, e_im + l_re * r_im + l_im * r_re
        carry_ref[0:1, st] = e_re[n_str - 1:n_str, :]
        carry_ref[1:2, st] = e_im[n_str - 1:n_str, :]
        i_re = jnp.where(srow == 0, c_re, pltpu.roll(e_re, 1, axis=0))
        i_im = jnp.where(srow == 0, c_im, pltpu.roll(e_im, 1, axis=0))
        h0_re, h0_im = [i_re], [i_im]
        for b in range(1, per):
            l_re, l_im = lam(n_scan + b)
            h0_re.append(loc_re[b - 1] + l_re * i_re - l_im * i_im)
            h0_im.append(loc_im[b - 1] + l_re * i_im + l_im * i_re)
        h0_re = jnp.concatenate(h0_re, axis=0).astype(BF16)
        h0_im = jnp.concatenate(h0_im, axis=0).astype(BF16)
        yq = (jnp.dot(vq, mi_ref[q], preferred_element_type=F32)
              + jnp.dot(h0_re, wh_ref[q, :half, :], preferred_element_type=F32)
              + jnp.dot(h0_im, wh_ref[q, half:, :], preferred_element_type=F32))
        for b in range(per):
            for j in range(chunk):
                y_ref[q, pl.ds(b * chunk + j, n_str, stride=pitch), :] = (
                    yq[b * n_str:(b + 1) * n_str, j * LANES:(j + 1) * LANES])

    y = jnp.concatenate(
        [jnp.concatenate([y_ref[q, r * pitch:r * pitch + span, :] for r in range(n_str)], axis=0)
         for q in range(n_oct)], axis=1)
    ys = jax.nn.gelu(y, approximate=True)
    glu = jnp.dot(ys.astype(BF16), gluw_ref[...], preferred_element_type=F32) + glub_ref[...]
    ys = ys * jax.nn.sigmoid(glu)
    s_br = jnp.dot(ys.astype(BF16), brs_ref[...], preferred_element_type=F32)

    merged = g_pool * p_br + g_ssm * s_br
    o_ref[...] = x + jnp.dot(merged.astype(BF16), wout_ref[...], preferred_element_type=F32)


def _mixer(x2, seq, nm, w_in, pool_w, pool_scale, tables, glu_w, glu_b, br_pool, br_ssm, w_out, tm):
    t, d = x2.shape
    mi, ws, wh, lre, lim = tables
    pool_width = pool_w.shape[0] * pool_w.shape[1]
    ssm_width = glu_w.shape[0]
    consts = [nm.reshape(1, d), w_in.astype(BF16), pool_w.astype(BF16),
              pool_scale.reshape(1, pool_width), mi, ws, wh, lre, lim,
              glu_w.astype(BF16), glu_b.reshape(1, ssm_width), br_pool.astype(BF16),
              br_ssm.astype(BF16), w_out.astype(BF16)]
    kern = functools.partial(_mixer_kernel, tiles_per_seq=seq // tm, chunk=SSM_CHUNK)
    return pl.pallas_call(
        kern,
        grid=(t // tm,),
        in_specs=[pl.BlockSpec((tm, d), lambda i: (i, 0))] + [_const_spec(c.shape) for c in consts],
        out_specs=pl.BlockSpec((tm, d), lambda i: (i, 0)),
        out_shape=jax.ShapeDtypeStruct((t, d), F32),
        scratch_shapes=[pltpu.VMEM((POOL_HALO + tm, pool_width), F32),
                        pltpu.VMEM((ssm_width // LANES, tm + SUBLANES * STRAND_PAD, LANES), F32),
                        pltpu.VMEM((ssm_width // LANES, tm + SUBLANES * STRAND_PAD, LANES), F32),
                        pltpu.VMEM((2, lre.shape[1]), F32)],
        compiler_params=pltpu.CompilerParams(dimension_semantics=("arbitrary",),
                                             vmem_limit_bytes=VMEM_LIMIT),
        name="mixer",
    )(x2, *consts)


def _kv_fold_kernel(mem_ref, g_ref, wkv_ref, wq_ref, wo_ref, wqk_ref, wvo_ref):
    m, d = mem_ref.shape
    hd = d // X_HEADS
    mn = _rms(mem_ref[...], g_ref[...]).astype(BF16)
    kv = jnp.dot(mn, wkv_ref[...], preferred_element_type=F32)
    for h in range(X_HEADS):
        k_h = kv[:, h * hd:(h + 1) * hd].astype(BF16)
        v_h = kv[:, d + h * hd:d + (h + 1) * hd].astype(BF16)
        qk = lax.dot_general(wq_ref[:, h * hd:(h + 1) * hd], k_h, (((1,), (1,)), ((), ())),
                             preferred_element_type=F32)
        wqk_ref[0, :, h * m:(h + 1) * m] = qk.astype(BF16)
        wvo_ref[0, h * m:(h + 1) * m, :] = jnp.dot(
            v_h, wo_ref[h * hd:(h + 1) * hd, :], preferred_element_type=F32).astype(BF16)


def _kv_fold(mem2, batch, g, w_kv, w_q, w_o):
    d = mem2.shape[1]
    m = mem2.shape[0] // batch
    hm = X_HEADS * m
    return pl.pallas_call(
        _kv_fold_kernel,
        grid=(batch,),
        in_specs=[pl.BlockSpec((m, d), lambda b: (b, 0)), _const_spec((1, d)),
                  _const_spec(w_kv.shape), _const_spec(w_q.shape), _const_spec(w_o.shape)],
        out_specs=[pl.BlockSpec((1, d, hm), lambda b: (b, 0, 0)),
                   pl.BlockSpec((1, hm, d), lambda b: (b, 0, 0))],
        out_shape=[jax.ShapeDtypeStruct((batch, d, hm), BF16),
                   jax.ShapeDtypeStruct((batch, hm, d), BF16)],
        compiler_params=pltpu.CompilerParams(dimension_semantics=("arbitrary",),
                                             vmem_limit_bytes=VMEM_LIMIT),
        name="kv_fold",
    )(mem2, g.reshape(1, d), w_kv.astype(BF16), w_q.astype(BF16), w_o.astype(BF16))


def _route(logits):
    n_g, n_eg = N_EXPERT_GROUPS, EXPERTS_PER_GROUP
    tm = logits.shape[0]
    lt = logits.T
    neg = jnp.float32(-jnp.inf)
    big = jnp.int32(1 << 20)

    def first_max(v):
        row = lax.broadcasted_iota(jnp.int32, v.shape, 0)
        m = jnp.max(v, axis=0, keepdims=True)
        return m, jnp.min(jnp.where(v == m, row, big), axis=0, keepdims=True)

    gl = lt[0:n_g]
    gmax, g_idx = first_max(gl)
    g_gate = 1.0 / jnp.sum(jnp.exp(gl - gmax), axis=0, keepdims=True)
    el = lt[n_g:n_g + n_eg]
    for g in range(1, n_g):
        el = jnp.where(g_idx == g, lt[n_g + g * n_eg:n_g + (g + 1) * n_eg], el)
    row = lax.broadcasted_iota(jnp.int32, el.shape, 0)
    t1, i1 = first_max(el)
    t2, i2 = first_max(jnp.where(row == i1, neg, el))
    e2 = jnp.exp(t2 - t1)
    w1 = 1.0 / (1.0 + e2)
    w2 = e2 / (1.0 + e2)
    comb = jnp.where(row == i1, w1, jnp.where(row == i2, w2, 0.0)) * g_gate
    lo, hi = jnp.minimum(i1, i2), jnp.maximum(i1, i2)
    pair = jnp.zeros_like(lo)
    for idx, (e_lo, e_hi) in enumerate(EXPERT_PAIRS):
        pair = jnp.where((lo == e_lo) & (hi == e_hi), idx, pair)
    cls = g_idx * len(EXPERT_PAIRS) + pair
    info = jnp.concatenate([comb, jnp.zeros((LANES - n_eg, tm), F32)], axis=0)
    hist = jnp.sum((cls == lax.broadcasted_iota(jnp.int32, (N_CLASS_ROWS, tm), 0)).astype(F32),
                   axis=1, keepdims=True)
    return info.T, cls, hist


def _dot_bf16x3(x, w_hilo):
    n = w_hilo.shape[1] // 2
    x_hi = x.astype(BF16)
    x_lo = (x - x_hi.astype(F32)).astype(BF16)
    both = jnp.dot(x_hi, w_hilo, preferred_element_type=F32)
    return both[:, :n] + both[:, n:] + jnp.dot(x_lo, w_hilo[:, :n], preferred_element_type=F32)


def _xattn_kernel(h_ref, g_ref, wqk_ref, wvo_ref, gffn_ref, wr_ref, br_ref, w1f_ref, w3f_ref, w2f_ref,
                  o_ref, cls_ref, hist_ref, w1b_ref, w3b_ref, w2b_ref, *, scale, n_experts):
    i = pl.program_id(0)

    @pl.when(i < n_experts)
    def _():
        w1b_ref[...] = w1f_ref[...].astype(BF16)
        w3b_ref[...] = w3f_ref[...].astype(BF16)

    @pl.when(jnp.logical_and(i >= n_experts, i < 2 * n_experts))
    def _():
        w2b_ref[...] = w2f_ref[...].astype(BF16)

    h = h_ref[...]
    d = h.shape[1]
    m = wqk_ref.shape[2] // X_HEADS
    hn = _rms(h, g_ref[...]).astype(BF16)
    s_all = jnp.dot(hn, wqk_ref[0], preferred_element_type=F32) * scale
    probs = []
    for hh in range(X_HEADS):
        s = s_all[:, hh * m:(hh + 1) * m]
        s = s - jnp.max(s, axis=-1, keepdims=True)
        p = jnp.exp(s)
        probs.append((p / jnp.sum(p, axis=-1, keepdims=True)).astype(BF16))
    h_out = h + jnp.dot(jnp.concatenate(probs, axis=1), wvo_ref[0], preferred_element_type=F32)
    zn = _rms(h_out, gffn_ref[...])
    logits = _dot_bf16x3(zn, wr_ref[...]) + br_ref[...]
    route, cls, hist = _route(logits)
    o_ref[:, :d] = h_out
    o_ref[:, d:] = route
    cls_ref[0] = cls
    hist_ref[0] = jnp.broadcast_to(hist, hist_ref.shape[1:])


def _xattn(h2, seq, g, wqk, wvo, g_ffn, w_r, b_r, w1, w3, w2, tm):
    t, d = h2.shape
    tiles_per_seq = seq // tm
    n_tiles = t // tm
    n_e = w1.shape[0]
    assert n_tiles >= 2 * n_e, "the expert-weight cast needs two grid steps per expert"
    kern = functools.partial(_xattn_kernel, scale=(d // X_HEADS) ** -0.5, n_experts=n_e)

    def first(i):
        return (jnp.minimum(i, n_e - 1), 0, 0)

    def second(i):
        return (jnp.clip(i - n_e, 0, n_e - 1), 0, 0)
    w_r_hi = w_r.astype(BF16)
    w_r_lo = (w_r - w_r_hi.astype(F32)).astype(BF16)
    w_r_hilo = jnp.concatenate([w_r_hi, w_r_lo], axis=1)
    hx, cls, hist, w1b, w3b, w2b = pl.pallas_call(
        kern,
        grid=(n_tiles,),
        in_specs=[pl.BlockSpec((tm, d), lambda i: (i, 0)), _const_spec((1, d)),
                  pl.BlockSpec((1,) + wqk.shape[1:], lambda i: (i // tiles_per_seq, 0, 0)),
                  pl.BlockSpec((1,) + wvo.shape[1:], lambda i: (i // tiles_per_seq, 0, 0)),
                  _const_spec((1, d)), _const_spec(w_r_hilo.shape), _const_spec(b_r.shape),
                  pl.BlockSpec((1,) + w1.shape[1:], first), pl.BlockSpec((1,) + w3.shape[1:], first),
                  pl.BlockSpec((1,) + w2.shape[1:], second)],
        out_specs=[pl.BlockSpec((tm, d + LANES), lambda i: (i, 0)),
                   pl.BlockSpec((1, 1, tm), lambda i: (i, 0, 0)),
                   pl.BlockSpec((1, N_CLASS_ROWS, LANES), lambda i: (i, 0, 0)),
                   pl.BlockSpec((1,) + w1.shape[1:], first), pl.BlockSpec((1,) + w3.shape[1:], first),
                   pl.BlockSpec((1,) + w2.shape[1:], second)],
        out_shape=[jax.ShapeDtypeStruct((t, d + LANES), F32),
                   jax.ShapeDtypeStruct((n_tiles, 1, tm), jnp.int32),
                   jax.ShapeDtypeStruct((n_tiles, N_CLASS_ROWS, LANES), F32),
                   jax.ShapeDtypeStruct(w1.shape, BF16), jax.ShapeDtypeStruct(w3.shape, BF16),
                   jax.ShapeDtypeStruct(w2.shape, BF16)],
        compiler_params=pltpu.CompilerParams(dimension_semantics=("arbitrary",),
                                             vmem_limit_bytes=VMEM_LIMIT),
        name="xattn",
    )(h2, g.reshape(1, d), wqk, wvo, g_ffn.reshape(1, d), w_r_hilo, b_r, w1, w3, w2)
    return hx, cls.reshape(t), jnp.sum(hist[:, :, 0], axis=0).astype(jnp.int32), (w1b, w3b, w2b)


def _dispatch_tables(cls, counts, tm):
    t = cls.shape[0]
    n_g, n_eg, n_pair = N_EXPERT_GROUPS, EXPERTS_PER_GROUP, len(EXPERT_PAIRS)
    n_tiles = t // tm + n_g
    i32 = jnp.int32
    _, tok_sorted = lax.sort((cls, jnp.arange(t, dtype=i32)), num_keys=1, is_stable=True)
    tok_sorted = jnp.concatenate([tok_sorted, jnp.zeros((tm,), i32)])
    counts = counts[:n_g * n_pair]
    c_start = jnp.cumsum(counts) - counts
    g_count = counts.reshape(n_g, n_pair).sum(axis=1)
    g_start = jnp.cumsum(g_count) - g_count
    tiles_g = (g_count + tm - 1) // tm
    tile_end = jnp.cumsum(tiles_g)
    tile_first = tile_end - tiles_g
    tile_ids = jnp.arange(n_tiles, dtype=i32)
    tile_group = jnp.minimum(jnp.sum((tile_ids[:, None] >= tile_end[None, :]).astype(i32), axis=1),
                             n_g - 1)
    k_in_group = tile_ids - tile_first[tile_group]
    tile_valid = jnp.clip(g_count[tile_group] - k_in_group * tm, 0, tm)
    tile_valid = jnp.where(tile_ids < tile_end[-1], tile_valid, 0)
    tile_start = jnp.where(tile_valid > 0, g_start[tile_group] + k_in_group * tm, 0)
    lo, hi = tile_start[:, None], (tile_start + tile_valid)[:, None]
    inter = (counts[None, :] > 0) & (c_start[None, :] < hi) & ((c_start + counts)[None, :] > lo)
    inter = inter.reshape(n_tiles, n_g, n_pair).any(axis=1)
    pair_has = jnp.array([[e in p for e in range(n_eg)] for p in EXPERT_PAIRS])
    need = (inter[:, :, None] & pair_has[None]).any(axis=1)
    tile_nexp = need.sum(axis=1)
    tile_exp = jnp.argsort(jnp.logical_not(need), axis=1, stable=True)
    return (tok_sorted, tile_start.astype(i32), tile_group.astype(i32), tile_valid.astype(i32),
            tile_nexp.astype(i32), tile_exp.reshape(-1).astype(i32))


def _moe_kernel(tok_ref, ts_ref, tg_ref, tv_ref, tn_ref, te_ref, hx_hbm, g_ref, w1_ref, w3_ref, w2_ref,
                gf_ref, out_hbm, xnext, osend, xcur, zn_ref, acc0_ref, acc_ref, gsem, ssem):
    i = pl.program_id(0)
    n_blk, rows = xnext.shape[0] - 1, xnext.shape[1]
    tm = n_blk * rows
    d = osend.shape[2]
    n_eg = w1_ref.shape[0]

    def row_in(tok, blk, k):
        return pltpu.make_async_copy(hx_hbm.at[pl.ds(tok, 1), :], xnext.at[blk, pl.ds(k, 1), :], gsem)

    def row_out(tok, blk, k):
        return pltpu.make_async_copy(osend.at[blk, pl.ds(k, 1), :], out_hbm.at[pl.ds(tok, 1), :], ssem)

    def tile_in_wait():
        pltpu.make_async_copy(xnext.at[pl.ds(0, n_blk)], xnext.at[pl.ds(0, n_blk)], gsem).wait()

    def tile_out_wait():
        pltpu.make_async_copy(osend.at[pl.ds(0, n_blk)], osend.at[pl.ds(0, n_blk)], ssem).wait()

    def anchor(width):
        zero = xnext[n_blk, 0:1, 0:width]
        osend[n_blk, 0:1, 0:width] = zero
        return zero + osend[n_blk, 0:1, 0:width]

    def expert(zn, k, move_rows=None):
        e = te_ref[i * n_eg + k]
        if move_rows is not None:
            move_rows(0)
        a = jnp.dot(zn, w1_ref[e], preferred_element_type=F32)
        if move_rows is not None:
            a = a + anchor(a.shape[1])
            move_rows(1)
        b = jnp.dot(zn, w3_ref[e], preferred_element_type=F32)
        hid = (a * jax.nn.sigmoid(a) * b).astype(BF16)
        route = xcur[:, d:]
        lane = lax.broadcasted_iota(jnp.int32, route.shape, 1)
        comb = jnp.sum(jnp.where(lane == e, route, 0.0), axis=1, keepdims=True)
        out = comb * jnp.dot(hid, w2_ref[e], preferred_element_type=F32)
        if move_rows is not None:
            out = out + anchor(d)
        return out

    busy = tv_ref[i] > 0
    after_busy = tv_ref[jnp.maximum(i - 1, 0)] > 0
    drain = jnp.logical_and(jnp.logical_not(busy), jnp.logical_and(i >= 1, after_busy))

    @pl.when(i == 0)
    def _():
        xnext[n_blk] = jnp.zeros(xnext.shape[1:], F32)
        osend[...] = jnp.zeros(osend.shape, F32)
        base = ts_ref[0]

        def body(blk, c):
            for k in range(rows):
                row_in(tok_ref[base + blk * rows + k], blk, k).start()
            return c
        lax.fori_loop(0, n_blk, body, 0)

    @pl.when(jnp.logical_or(busy, drain))
    def _():
        tile_in_wait()
        xcur[...] = xnext[0:n_blk].reshape(tm, xnext.shape[2])

    def hand_over_results():
        @pl.when(i >= 1)
        def _():
            tile_out_wait()
            osend[0:n_blk] = _rms(acc_ref[...], gf_ref[...]).reshape(n_blk, rows, d)

    n_always = 2

    @pl.when(busy)
    def _():
        h = xcur[:, :d]
        zn = _rms(h, g_ref[...]).astype(BF16)
        zn_ref[...] = zn
        nxt = ts_ref[i + 1]

        def gather_rows(part):
            for r in range(part * tm // 2, (part + 1) * tm // 2):
                row_in(tok_ref[nxt + r], r // rows, r % rows).start()
        acc0_ref[...] = h + expert(zn, 0, gather_rows)

    @pl.when(busy)
    def _():
        hand_over_results()

    @pl.when(busy)
    def _():
        prv = ts_ref[jnp.maximum(i - 1, 0)]

        def scatter_rows(part):
            for r in range(part * tm // 2, (part + 1) * tm // 2):
                row_out(tok_ref[prv + r], r // rows, r % rows).start()
        acc_ref[...] = acc0_ref[...] + expert(zn_ref[...], 1, scatter_rows)

    for k in range(n_always, n_eg):
        @pl.when(tn_ref[i] > k)
        def _():
            acc_ref[...] += expert(zn_ref[...], k)

    @pl.when(drain)
    def _():
        hand_over_results()
        base = ts_ref[i - 1]
        nv = tv_ref[i - 1]
        full = nv // rows

        def body(blk, c):
            for k in range(rows):
                row_out(tok_ref[base + blk * rows + k], blk, k).start()
            return c
        lax.fori_loop(0, full, body, 0)
        for k in range(rows - 1):

            @pl.when(full * rows + k < nv)
            def _():
                row_out(tok_ref[base + full * rows + k], full, k).start()

        @pl.when(full > 0)
        def _():
            pltpu.make_async_copy(osend.at[pl.ds(0, full)], osend.at[pl.ds(0, full)], ssem).wait()
        for k in range(rows - 1):

            @pl.when(full * rows + k < nv)
            def _():
                row_out(0, 0, 0).wait()


def _moe(hx, cls, counts, g, w1, w3, w2, g_final, tm):
    t, dx = hx.shape
    d = dx - LANES
    n_eg = EXPERTS_PER_GROUP
    de = w1.shape[2]
    tables = _dispatch_tables(cls, counts, tm)
    n_tiles = tables[2].shape[0]

    def wmap(i, tok, ts, tg, tv, tn, te):
        return (tg[i], 0, 0)

    def cmap(i, tok, ts, tg, tv, tn, te):
        return (0, 0)

    grid_spec = pltpu.PrefetchScalarGridSpec(
        num_scalar_prefetch=len(tables),
        grid=(n_tiles,),
        in_specs=[pl.BlockSpec(memory_space=pl.ANY),
                  pl.BlockSpec((1, d), cmap),
                  pl.BlockSpec((n_eg, d, de), wmap),
                  pl.BlockSpec((n_eg, d, de), wmap),
                  pl.BlockSpec((n_eg, de, d), wmap),
                  pl.BlockSpec((1, d), cmap)],
        out_specs=pl.BlockSpec(memory_space=pl.ANY),
        scratch_shapes=[pltpu.VMEM((tm // SUBLANES + 1, SUBLANES, dx), F32),
                        pltpu.VMEM((tm // SUBLANES + 1, SUBLANES, d), F32),
                        pltpu.VMEM((tm, dx), F32),
                        pltpu.VMEM((tm, d), BF16),
                        pltpu.VMEM((tm, d), F32),
                        pltpu.VMEM((tm, d), F32),
                        pltpu.SemaphoreType.DMA(()), pltpu.SemaphoreType.DMA(())])
    return pl.pallas_call(
        _moe_kernel,
        grid_spec=grid_spec,
        out_shape=jax.ShapeDtypeStruct((t, d), F32),
        compiler_params=pltpu.CompilerParams(dimension_semantics=("arbitrary",),
                                             vmem_limit_bytes=VMEM_LIMIT),
        name="moe",
    )(*tables, hx, g.reshape(1, d), w1, w3, w2,
      g_final.reshape(1, d))


def _layer(h2, mem2, batch, seq, p, l, g_final, tm):
    d = h2.shape[1]
    tm_mix = min(MIXER_TILE, seq)
    n_chunks = tm_mix // SSM_CHUNK
    n_scan = int(math.log2(n_chunks))
    tables = _ssm_tables(p["ssm_a_re"][l], p["ssm_a_im"][l], p["ssm_log_dt"][l], p["ssm_b_re"][l],
                         p["ssm_b_im"][l], p["ssm_c_re"][l], p["ssm_c_im"][l], p["ssm_d"][l],
                         SSM_CHUNK, n_scan, n_chunks // SUBLANES)
    h2 = _mixer(h2, seq, p["norm_mix"][l], p["w_in"][l], p["pool_w"][l], p["pool_scale"][l], tables,
                p["glu_w"][l], p["glu_b"][l], p["br_pool"][l], p["br_ssm"][l], p["w_out"][l],
                tm_mix)

    wqk, wvo = _kv_fold(mem2, batch, p["norm_mem"][l], p["x_wkv"][l], p["x_wq"][l], p["x_wo"][l])

    n_g, n_eg = N_EXPERT_GROUPS, EXPERTS_PER_GROUP
    w_r = jnp.concatenate([p["router_g_w"][l], p["router_e_w"][l].reshape(d, n_g * n_eg)], axis=1)
    w_r = jnp.pad(w_r, ((0, 0), (0, LANES - w_r.shape[1])))
    b_r = jnp.concatenate([p["router_g_b"][l], p["router_e_b"][l].reshape(n_g * n_eg)])
    b_r = jnp.pad(b_r, (0, LANES - b_r.shape[0])).reshape(1, LANES)
    hx, cls, counts, (w1b, w3b, w2b) = _xattn(
        h2, seq, p["norm_x"][l], wqk, wvo, p["norm_ffn"][l], w_r, b_r,
        p["exp_w1"][l], p["exp_w3"][l], p["exp_w2"][l], min(XATTN_TILE, seq))
    return _moe(hx, cls, counts, p["norm_ffn"][l], w1b, w3b, w2b, g_final, tm)


def kernel(x, mem, norm_mix, w_in, pool_w, pool_scale, ssm_a_re, ssm_a_im, ssm_log_dt, ssm_b_re,
           ssm_b_im, ssm_c_re, ssm_c_im, ssm_d, glu_w, glu_b, br_pool, br_ssm, w_out, norm_x,
           norm_mem, x_wq, x_wkv, x_wo, norm_ffn, router_g_w, router_g_b, router_e_w, router_e_b,
           exp_w1, exp_w3, exp_w2, norm_final):
    p = dict(norm_mix=norm_mix, w_in=w_in, pool_w=pool_w, pool_scale=pool_scale, ssm_a_re=ssm_a_re,
             ssm_a_im=ssm_a_im, ssm_log_dt=ssm_log_dt, ssm_b_re=ssm_b_re, ssm_b_im=ssm_b_im,
             ssm_c_re=ssm_c_re, ssm_c_im=ssm_c_im, ssm_d=ssm_d, glu_w=glu_w, glu_b=glu_b,
             br_pool=br_pool, br_ssm=br_ssm, w_out=w_out, norm_x=norm_x, norm_mem=norm_mem,
             x_wq=x_wq, x_wkv=x_wkv, x_wo=x_wo, norm_ffn=norm_ffn, router_g_w=router_g_w,
             router_g_b=router_g_b, router_e_w=router_e_w, router_e_b=router_e_b, exp_w1=exp_w1,
             exp_w3=exp_w3, exp_w2=exp_w2)
    batch, seq, d = x.shape
    assert norm_mix.shape[0] == 1, "final norm is fused into the (single) layer's last stage"
    tm = min(MOE_TILE, seq)
    assert seq % min(MIXER_TILE, seq) == 0 and seq % min(XATTN_TILE, seq) == 0 and (batch * seq) % tm == 0
    h2 = x.reshape(batch * seq, d)
    mem2 = mem.reshape(-1, d)
    out = _layer(h2, mem2, batch, seq, p, 0, norm_final, tm)
    return out.reshape(batch, seq, d)
```

```python
import functools
import math

import jax
import jax.numpy as jnp
from jax import lax
from jax.experimental import pallas as pl
from jax.experimental.pallas import tpu as pltpu

F32 = jnp.float32
BF16 = jnp.bfloat16

RMS_EPS = 1e-6
POOL_WINDOWS = (2, 4, 8, 16)
POOL_HALO = 16
SSM_GROUP_DIM = 16
SSM_STATE = 64
LANES = 128
SUBLANES = 8
OCT = LANES // SSM_GROUP_DIM
SSM_CHUNK = 4
STRAND_PAD = 8
X_HEADS = 4
MIXER_TILE = 1024
XATTN_TILE = 1024
MOE_TILE = 512
N_EXPERT_GROUPS = 4
EXPERTS_PER_GROUP = 4
EXPERT_PAIRS = ((0, 1), (0, 2), (0, 3), (1, 3), (1, 2), (2, 3))
N_CLASS_ROWS = 32
VMEM_LIMIT = 56 * 1024 * 1024


def _rms(x, g):
    return x * lax.rsqrt(jnp.mean(x * x, axis=-1, keepdims=True) + RMS_EPS) * g


def _const_spec(shape):
    nd = len(shape)
    return pl.BlockSpec(shape, lambda *_: (0,) * nd, pipeline_mode=pl.Buffered(1))


def _ssm_tables(a_re, a_im, log_dt, b_re, b_im, c_re, c_im, d_skip, chunk, n_scan, n_pow):
    hi = lax.Precision.HIGHEST
    G, N = a_re.shape
    C = SSM_GROUP_DIM
    Q = G // OCT
    dt = jnp.exp(log_dt)[:, None]
    mag = jnp.exp(a_re * dt)
    lb_re, lb_im = mag * jnp.cos(a_im * dt), mag * jnp.sin(a_im * dt)
    den = a_re * a_re + a_im * a_im
    nr, ni = lb_re - 1.0, lb_im
    f_re = ((nr * a_re + ni * a_im) / den)[..., None]
    f_im = ((ni * a_re - nr * a_im) / den)[..., None]
    bb_re = f_re * b_re - f_im * b_im
    bb_im = f_re * b_im + f_im * b_re

    pw_re, pw_im = [jnp.ones_like(lb_re)], [jnp.zeros_like(lb_re)]
    for _ in range(chunk):
        r, i = pw_re[-1], pw_im[-1]
        pw_re.append(r * lb_re - i * lb_im)
        pw_im.append(r * lb_im + i * lb_re)
    pw_re, pw_im = jnp.stack(pw_re), jnp.stack(pw_im)

    lbb_re = pw_re[:chunk, :, :, None] * bb_re - pw_im[:chunk, :, :, None] * bb_im
    lbb_im = pw_re[:chunk, :, :, None] * bb_im + pw_im[:chunk, :, :, None] * bb_re
    kern = (jnp.einsum('gon,dgnc->dgoc', c_re, lbb_re, precision=hi)
            - jnp.einsum('gon,dgnc->dgoc', c_im, lbb_im, precision=hi))
    kern = kern.at[0].add(d_skip.reshape(G, C)[:, :, None] * jnp.eye(C, dtype=F32))

    lane_g = jnp.arange(LANES) // C
    state_g = jnp.arange(OCT * N) // N
    rep_c = (jnp.arange(C)[:, None] == (jnp.arange(LANES) % C)[None, :]).astype(F32)
    rep_n = (jnp.arange(N)[:, None] == (jnp.arange(OCT * N) % N)[None, :]).astype(F32)
    same_cc = (lane_g[:, None] == lane_g[None, :]).astype(F32)
    same_cn = (lane_g[:, None] == state_g[None, :]).astype(F32)

    kt = kern.transpose(0, 1, 3, 2).reshape(chunk, Q, LANES, C)
    blk = (jnp.einsum('dqrc,cl->dqrl', kt, rep_c, precision=hi) * same_cc).astype(BF16)
    zero = jnp.zeros((Q, LANES, LANES), BF16)
    m_intra = jnp.concatenate(
        [jnp.concatenate([blk[jp - j] if jp >= j else zero for jp in range(chunk)], axis=2)
         for j in range(chunk)], axis=1)

    def state_half(lbb):
        z = lbb[::-1].transpose(0, 1, 3, 2).reshape(chunk, Q, LANES, N)
        w = jnp.einsum('jqrn,nl->qjrl', z, rep_n, precision=hi) * same_cn
        return w.reshape(Q, chunk * LANES, OCT * N).astype(BF16)
    w_state = jnp.concatenate([state_half(lbb_re), state_half(lbb_im)], axis=-1)

    p_re = c_re[None] * pw_re[1:, :, None, :] - c_im[None] * pw_im[1:, :, None, :]
    p_im = c_re[None] * pw_im[1:, :, None, :] + c_im[None] * pw_re[1:, :, None, :]
    def out_half(p):
        z = p.transpose(0, 1, 3, 2).reshape(chunk, Q, OCT * N, C)
        w = jnp.einsum('jqrc,cl->qrjl', z, rep_c, precision=hi)
        w = w * same_cn.T[None, :, None, :]
        return w.reshape(Q, OCT * N, chunk * LANES).astype(BF16)
    w_out = jnp.concatenate([out_half(p_re), out_half(-p_im)], axis=1)

    lr, li = [pw_re[chunk]], [pw_im[chunk]]
    for _ in range(n_scan - 1):
        r, i = lr[-1], li[-1]
        lr.append(r * r - i * i)
        li.append(2.0 * r * i)
    qr, qi = [jnp.ones_like(lb_re)], [jnp.zeros_like(lb_re)]
    for _ in range(n_pow - 1):
        r, i = qr[-1], qi[-1]
        qr.append(r * pw_re[chunk] - i * pw_im[chunk])
        qi.append(r * pw_im[chunk] + i * pw_re[chunk])
    lam_re = jnp.stack(lr + qr).reshape(n_scan + n_pow, G * N)
    lam_im = jnp.stack(li + qi).reshape(n_scan + n_pow, G * N)
    return m_intra, w_state, w_out, lam_re, lam_im


def _mixer_kernel(x_ref, nm_ref, win_ref, poolw_ref, pscale_ref, mi_ref, ws_ref, wh_ref,
                  lre_ref, lim_ref, gluw_ref, glub_ref, brp_ref, brs_ref, wout_ref,
                  o_ref, zext_ref, v_ref, y_ref, carry_ref, *, tiles_per_seq, chunk):
    tm, d_model = x_ref.shape
    pool_w = zext_ref.shape[1]
    n_oct = v_ref.shape[0]
    ssm_w = n_oct * LANES
    n_chunks = tm // chunk
    n_str = SUBLANES
    per = n_chunks // n_str
    span = per * chunk
    pitch = v_ref.shape[1] // n_str
    n_scan = lre_ref.shape[0] - per
    seq_tile = pl.program_id(0) % tiles_per_seq

    @pl.when(seq_tile == 0)
    def _():
        zext_ref[0:POOL_HALO, :] = jnp.zeros((POOL_HALO, pool_w), F32)
        carry_ref[...] = jnp.zeros_like(carry_ref)

    x = x_ref[...]
    u = _rms(x, nm_ref[...]).astype(BF16)
    proj = jnp.dot(u, win_ref[...], preferred_element_type=F32)
    z = proj[:, :pool_w]
    zext_ref[POOL_HALO:, :] = z
    for q in range(n_oct):
        for r in range(n_str):
            v_ref[q, r * pitch:r * pitch + span, :] = (
                proj[r * span:(r + 1) * span, pool_w + q * LANES:pool_w + (q + 1) * LANES])
    g_pool = jax.nn.sigmoid(proj[:, pool_w + ssm_w:pool_w + ssm_w + d_model])
    g_ssm = jax.nn.sigmoid(proj[:, pool_w + ssm_w + d_model:])

    pos = (seq_tile * tm + lax.broadcasted_iota(jnp.int32, (tm, 1), 0) + 1).astype(F32)
    gdim = pool_w // len(POOL_WINDOWS)
    ypool = []
    for gi, w in enumerate(POOL_WINDOWS):
        cols = slice(gi * gdim, (gi + 1) * gdim)
        acc = z[:, cols]
        for dlt in range(1, w):
            acc = acc + zext_ref[POOL_HALO - dlt:POOL_HALO - dlt + tm, cols]
        diff = acc / jnp.minimum(pos, float(w)) - z[:, cols]
        ypool.append(jnp.dot(diff.astype(BF16), poolw_ref[gi], preferred_element_type=F32))
    ypool = jnp.concatenate(ypool, axis=1) * pscale_ref[...]
    zext_ref[0:POOL_HALO, :] = zext_ref[tm:tm + POOL_HALO, :]
    p_br = jnp.dot(ypool.astype(BF16), brp_ref[...], preferred_element_type=F32)

    srow = lax.broadcasted_iota(jnp.int32, (n_str, 1), 0)
    half = OCT * SSM_STATE
    log_per = per.bit_length() - 1
    for q in range(n_oct):
        st = slice(q * half, (q + 1) * half)

        def lam(k):
            return lre_ref[k:k + 1, st], lim_ref[k:k + 1, st]

        vq = jnp.concatenate(
            [jnp.concatenate([v_ref[q, pl.ds(b * chunk + j, n_str, stride=pitch), :]
                              for j in range(chunk)], axis=1) for b in range(per)],
            axis=0).astype(BF16)
        s = jnp.dot(vq, ws_ref[q], preferred_element_type=F32)
        a_re, a_im = lam(0)
        loc_re, loc_im = [s[0:n_str, :half]], [s[0:n_str, half:]]
        for b in range(1, per):
            p_re, p_im = loc_re[-1], loc_im[-1]
            loc_re.append(s[b * n_str:(b + 1) * n_str, :half] + a_re * p_re - a_im * p_im)
            loc_im.append(s[b * n_str:(b + 1) * n_str, half:] + a_re * p_im + a_im * p_re)
        c_re, c_im = carry_ref[0:1, st], carry_ref[1:2, st]
        e_re, e_im = loc_re[-1], loc_im[-1]
        l_re, l_im = lam(log_per)
        e_re = e_re + jnp.where(srow == 0, l_re * c_re - l_im * c_im, 0.0)
        e_im = e_im + jnp.where(srow == 0, l_re * c_im + l_im * c_re, 0.0)
        for k in range(n_str.bit_length() - 1):
            sh = 1 << k
            r_re = jnp.where(srow >= sh, pltpu.roll(e_re, sh, axis=0), 0.0)
            r_im = jnp.where(srow >= sh, pltpu.roll(e_im, sh, axis=0), 0.0)
            l_re, l_im = lam(log_per + k)
            e_re, e_im = e_re + l_re * r_re - l_im * r_im, e_im + l_re * r_im + l_im * r_re
        carry_ref[0:1, st] = e_re[n_str - 1:n_str, :]
        carry_ref[1:2, st] = e_im[n_str - 1:n_str, :]
        i_re = jnp.where(srow == 0, c_re, pltpu.roll(e_re, 1, axis=0))
        i_im = jnp.where(srow == 0, c_im, pltpu.roll(e_im, 1, axis=0))
        h0_re, h0_im = [i_re], [i_im]
        for b in range(1, per):
            l_re, l_im = lam(n_scan + b)
            h0_re.append(loc_re[b - 1] + l_re * i_re - l_im * i_im)
            h0_im.append(loc_im[b - 1] + l_re * i_im + l_im * i_re)
        h0_re = jnp.concatenate(h0_re, axis=0).astype(BF16)
        h0_im = jnp.concatenate(h0_im, axis=0).astype(BF16)
        yq = (jnp.dot(vq, mi_ref[q], preferred_element_type=F32)
              + jnp.dot(h0_re, wh_ref[q, :half, :], preferred_element_type=F32)
              + jnp.dot(h0_im, wh_ref[q, half:, :], preferred_element_type=F32))
        for b in range(per):
            for j in range(chunk):
                y_ref[q, pl.ds(b * chunk + j, n_str, stride=pitch), :] = (
                    yq[b * n_str:(b + 1) * n_str, j * LANES:(j + 1) * LANES])

    y = jnp.concatenate(
        [jnp.concatenate([y_ref[q, r * pitch:r * pitch + span, :] for r in range(n_str)], axis=0)
         for q in range(n_oct)], axis=1)
    ys = jax.nn.gelu(y, approximate=True)
    glu = jnp.dot(ys.astype(BF16), gluw_ref[...], preferred_element_type=F32) + glub_ref[...]
    ys = ys * jax.nn.sigmoid(glu)
    s_br = jnp.dot(ys.astype(BF16), brs_ref[...], preferred_element_type=F32)

    merged = g_pool * p_br + g_ssm * s_br
    o_ref[...] = x + jnp.dot(merged.astype(BF16), wout_ref[...], preferred_element_type=F32)


def _mixer(x2, seq, nm, w_in, pool_w, pool_scale, tables, glu_w, glu_b, br_pool, br_ssm, w_out, tm):
    t, d = x2.shape
    mi, ws, wh, lre, lim = tables
    pool_width = pool_w.shape[0] * pool_w.shape[1]
    ssm_width = glu_w.shape[0]
    consts = [nm.reshape(1, d), w_in.astype(BF16), pool_w.astype(BF16),
              pool_scale.reshape(1, pool_width), mi, ws, wh, lre, lim,
              glu_w.astype(BF16), glu_b.reshape(1, ssm_width), br_pool.astype(BF16),
              br_ssm.astype(BF16), w_out.astype(BF16)]
    kern = functools.partial(_mixer_kernel, tiles_per_seq=seq // tm, chunk=SSM_CHUNK)
    return pl.pallas_call(
        kern,
        grid=(t // tm,),
        in_specs=[pl.BlockSpec((tm, d), lambda i: (i, 0))] + [_const_spec(c.shape) for c in consts],
        out_specs=pl.BlockSpec((tm, d), lambda i: (i, 0)),
        out_shape=jax.ShapeDtypeStruct((t, d), F32),
        scratch_shapes=[pltpu.VMEM((POOL_HALO + tm, pool_width), F32),
                        pltpu.VMEM((ssm_width // LANES, tm + SUBLANES * STRAND_PAD, LANES), F32),
                        pltpu.VMEM((ssm_width // LANES, tm + SUBLANES * STRAND_PAD, LANES), F32),
                        pltpu.VMEM((2, lre.shape[1]), F32)],
        compiler_params=pltpu.CompilerParams(dimension_semantics=("arbitrary",),
                                             vmem_limit_bytes=VMEM_LIMIT),
        name="mixer",
    )(x2, *consts)


def _kv_fold_kernel(mem_ref, g_ref, wkv_ref, wq_ref, wo_ref, wqk_ref, wvo_ref):
    m, d = mem_ref.shape
    hd = d // X_HEADS
    mn = _rms(mem_ref[...], g_ref[...]).astype(BF16)
    kv = jnp.dot(mn, wkv_ref[...], preferred_element_type=F32)
    for h in range(X_HEADS):
        k_h = kv[:, h * hd:(h + 1) * hd].astype(BF16)
        v_h = kv[:, d + h * hd:d + (h + 1) * hd].astype(BF16)
        qk = lax.dot_general(wq_ref[:, h * hd:(h + 1) * hd], k_h, (((1,), (1,)), ((), ())),
                             preferred_element_type=F32)
        wqk_ref[0, :, h * m:(h + 1) * m] = qk.astype(BF16)
        wvo_ref[0, h * m:(h + 1) * m, :] = jnp.dot(
            v_h, wo_ref[h * hd:(h + 1) * hd, :], preferred_element_type=F32).astype(BF16)


def _kv_fold(mem2, batch, g, w_kv, w_q, w_o):
    d = mem2.shape[1]
    m = mem2.shape[0] // batch
    hm = X_HEADS * m
    return pl.pallas_call(
        _kv_fold_kernel,
        grid=(batch,),
        in_specs=[pl.BlockSpec((m, d), lambda b: (b, 0)), _const_spec((1, d)),
                  _const_spec(w_kv.shape), _const_spec(w_q.shape), _const_spec(w_o.shape)],
        out_specs=[pl.BlockSpec((1, d, hm), lambda b: (b, 0, 0)),
                   pl.BlockSpec((1, hm, d), lambda b: (b, 0, 0))],
        out_shape=[jax.ShapeDtypeStruct((batch, d, hm), BF16),
                   jax.ShapeDtypeStruct((batch, hm, d), BF16)],
        compiler_params=pltpu.CompilerParams(dimension_semantics=("arbitrary",),
                                             vmem_limit_bytes=VMEM_LIMIT),
        name="kv_fold",
    )(mem2, g.reshape(1, d), w_kv.astype(BF16), w_q.astype(BF16), w_o.astype(BF16))


def _route(logits):
    n_g, n_eg = N_EXPERT_GROUPS, EXPERTS_PER_GROUP
    tm = logits.shape[0]
    lt = logits.T
    neg = jnp.float32(-jnp.inf)
    big = jnp.int32(1 << 20)

    def first_max(v):
        row = lax.broadcasted_iota(jnp.int32, v.shape, 0)
        m = jnp.max(v, axis=0, keepdims=True)
        return m, jnp.min(jnp.where(v == m, row, big), axis=0, keepdims=True)

    gl = lt[0:n_g]
    gmax, g_idx = first_max(gl)
    g_gate = 1.0 / jnp.sum(jnp.exp(gl - gmax), axis=0, keepdims=True)
    el = lt[n_g:n_g + n_eg]
    for g in range(1, n_g):
        el = jnp.where(g_idx == g, lt[n_g + g * n_eg:n_g + (g + 1) * n_eg], el)
    row = lax.broadcasted_iota(jnp.int32, el.shape, 0)
    t1, i1 = first_max(el)
    t2, i2 = first_max(jnp.where(row == i1, neg, el))
    e2 = jnp.exp(t2 - t1)
    w1 = 1.0 / (1.0 + e2)
    w2 = e2 / (1.0 + e2)
    comb = jnp.where(row == i1, w1, jnp.where(row == i2, w2, 0.0)) * g_gate
    lo, hi = jnp.minimum(i1, i2), jnp.maximum(i1, i2)
    pair = jnp.zeros_like(lo)
    for idx, (e_lo, e_hi) in enumerate(EXPERT_PAIRS):
        pair = jnp.where((lo == e_lo) & (hi == e_hi), idx, pair)
    cls = g_idx * len(EXPERT_PAIRS) + pair
    info = jnp.concatenate([comb, jnp.zeros((LANES - n_eg, tm), F32)], axis=0)
    hist = jnp.sum((cls == lax.broadcasted_iota(jnp.int32, (N_CLASS_ROWS, tm), 0)).astype(F32),
                   axis=1, keepdims=True)
    return info.T, cls, hist


def _dot_bf16x3(x, w_hilo):
    n = w_hilo.shape[1] // 2
    x_hi = x.astype(BF16)
    x_lo = (x - x_hi.astype(F32)).astype(BF16)
    both = jnp.dot(x_hi, w_hilo, preferred_element_type=F32)
    return both[:, :n] + both[:, n:] + jnp.dot(x_lo, w_hilo[:, :n], preferred_element_type=F32)


def _xattn_kernel(h_ref, g_ref, wqk_ref, wvo_ref, gffn_ref, wr_ref, br_ref, w1f_ref, w3f_ref, w2f_ref,
                  o_ref, cls_ref, hist_ref, w1b_ref, w3b_ref, w2b_ref, *, scale):
    w1b_ref[...] = w1f_ref[...].astype(BF16)
    w3b_ref[...] = w3f_ref[...].astype(BF16)
    w2b_ref[...] = w2f_ref[...].astype(BF16)

    h = h_ref[...]
    d = h.shape[1]
    m = wqk_ref.shape[2] // X_HEADS
    hn = _rms(h, g_ref[...]).astype(BF16)
    s_all = jnp.dot(hn, wqk_ref[0], preferred_element_type=F32) * scale
    probs = []
    for hh in range(X_HEADS):
        s = s_all[:, hh * m:(hh + 1) * m]
        s = s - jnp.max(s, axis=-1, keepdims=True)
        p = jnp.exp(s)
        probs.append((p / jnp.sum(p, axis=-1, keepdims=True)).astype(BF16))
    h_out = h + jnp.dot(jnp.concatenate(probs, axis=1), wvo_ref[0], preferred_element_type=F32)
    zn = _rms(h_out, gffn_ref[...])
    logits = _dot_bf16x3(zn, wr_ref[...]) + br_ref[...]
    route, cls, hist = _route(logits)
    o_ref[:, :d] = h_out
    o_ref[:, d:] = route
    cls_ref[0] = cls
    hist_ref[0] = jnp.broadcast_to(hist, hist_ref.shape[1:])


def _xattn(h2, seq, g, wqk, wvo, g_ffn, w_r, b_r, w1, w3, w2, tm):
    t, d = h2.shape
    tiles_per_seq = seq // tm
    n_tiles = t // tm
    n_e = w1.shape[0]
    assert n_tiles >= 2 * n_e, "the expert-weight cast needs two grid steps per expert"
    kern = functools.partial(_xattn_kernel, scale=(d // X_HEADS) ** -0.5)

    def half_expert(i):
        j = jnp.minimum(i, 2 * n_e - 1)
        return (j // 2, j % 2, 0)

    def half_block(w):
        return pl.BlockSpec((1, w.shape[1] // 2, w.shape[2]), half_expert)
    w_r_hi = w_r.astype(BF16)
    w_r_lo = (w_r - w_r_hi.astype(F32)).astype(BF16)
    w_r_hilo = jnp.concatenate([w_r_hi, w_r_lo], axis=1)
    hx, cls, hist, w1b, w3b, w2b = pl.pallas_call(
        kern,
        grid=(n_tiles,),
        in_specs=[pl.BlockSpec((tm, d), lambda i: (i, 0)), _const_spec((1, d)),
                  pl.BlockSpec((1,) + wqk.shape[1:], lambda i: (i // tiles_per_seq, 0, 0)),
                  pl.BlockSpec((1,) + wvo.shape[1:], lambda i: (i // tiles_per_seq, 0, 0)),
                  _const_spec((1, d)), _const_spec(w_r_hilo.shape), _const_spec(b_r.shape),
                  half_block(w1), half_block(w3), half_block(w2)],
        out_specs=[pl.BlockSpec((tm, d + LANES), lambda i: (i, 0)),
                   pl.BlockSpec((1, 1, tm), lambda i: (i, 0, 0)),
                   pl.BlockSpec((1, N_CLASS_ROWS, LANES), lambda i: (i, 0, 0)),
                   half_block(w1), half_block(w3), half_block(w2)],
        out_shape=[jax.ShapeDtypeStruct((t, d + LANES), F32),
                   jax.ShapeDtypeStruct((n_tiles, 1, tm), jnp.int32),
                   jax.ShapeDtypeStruct((n_tiles, N_CLASS_ROWS, LANES), F32),
                   jax.ShapeDtypeStruct(w1.shape, BF16), jax.ShapeDtypeStruct(w3.shape, BF16),
                   jax.ShapeDtypeStruct(w2.shape, BF16)],
        compiler_params=pltpu.CompilerParams(dimension_semantics=("arbitrary",),
                                             vmem_limit_bytes=VMEM_LIMIT),
        name="xattn",
    )(h2, g.reshape(1, d), wqk, wvo, g_ffn.reshape(1, d), w_r_hilo, b_r, w1, w3, w2)
    return hx, cls.reshape(t), jnp.sum(hist[:, :, 0], axis=0).astype(jnp.int32), (w1b, w3b, w2b)


def _dispatch_tables(cls, counts, tm):
    t = cls.shape[0]
    n_g, n_eg, n_pair = N_EXPERT_GROUPS, EXPERTS_PER_GROUP, len(EXPERT_PAIRS)
    n_tiles = t // tm + n_g
    i32 = jnp.int32
    _, tok_sorted = lax.sort((cls, jnp.arange(t, dtype=i32)), num_keys=1, is_stable=True)
    tok_sorted = jnp.concatenate([tok_sorted, jnp.zeros((tm,), i32)])
    counts = counts[:n_g * n_pair]
    c_start = jnp.cumsum(counts) - counts
    g_count = counts.reshape(n_g, n_pair).sum(axis=1)
    g_start = jnp.cumsum(g_count) - g_count
    tiles_g = (g_count + tm - 1) // tm
    tile_end = jnp.cumsum(tiles_g)
    tile_first = tile_end - tiles_g
    tile_ids = jnp.arange(n_tiles, dtype=i32)
    tile_group = jnp.minimum(jnp.sum((tile_ids[:, None] >= tile_end[None, :]).astype(i32), axis=1),
                             n_g - 1)
    k_in_group = tile_ids - tile_first[tile_group]
    tile_valid = jnp.clip(g_count[tile_group] - k_in_group * tm, 0, tm)
    tile_valid = jnp.where(tile_ids < tile_end[-1], tile_valid, 0)
    tile_start = jnp.where(tile_valid > 0, g_start[tile_group] + k_in_group * tm, 0)
    lo, hi = tile_start[:, None], (tile_start + tile_valid)[:, None]
    inter = (counts[None, :] > 0) & (c_start[None, :] < hi) & ((c_start + counts)[None, :] > lo)
    inter = inter.reshape(n_tiles, n_g, n_pair).any(axis=1)
    pair_has = jnp.array([[e in p for e in range(n_eg)] for p in EXPERT_PAIRS])
    need = (inter[:, :, None] & pair_has[None]).any(axis=1)
    tile_nexp = need.sum(axis=1)
    tile_exp = jnp.argsort(jnp.logical_not(need), axis=1, stable=True)
    return (tok_sorted, tile_start.astype(i32), tile_group.astype(i32), tile_valid.astype(i32),
            tile_nexp.astype(i32), tile_exp.reshape(-1).astype(i32))


def _moe_kernel(tok_ref, ts_ref, tg_ref, tv_ref, tn_ref, te_ref, hx_hbm, g_ref, w1_ref, w3_ref, w2_ref,
                gf_ref, out_hbm, xnext, osend, xcur, zn_ref, acc0_ref, acc_ref, gsem, ssem):
    i = pl.program_id(0)
    n_blk, rows = xnext.shape[0] - 1, xnext.shape[1]
    tm = n_blk * rows
    d = osend.shape[2]
    n_eg = w1_ref.shape[0]

    def row_in(tok, blk, k):
        return pltpu.make_async_copy(hx_hbm.at[pl.ds(tok, 1), :], xnext.at[blk, pl.ds(k, 1), :], gsem)

    def row_out(tok, blk, k):
        return pltpu.make_async_copy(osend.at[blk, pl.ds(k, 1), :], out_hbm.at[pl.ds(tok, 1), :], ssem)

    def tile_in_wait():
        pltpu.make_async_copy(xnext.at[pl.ds(0, n_blk)], xnext.at[pl.ds(0, n_blk)], gsem).wait()

    def tile_out_wait():
        pltpu.make_async_copy(osend.at[pl.ds(0, n_blk)], osend.at[pl.ds(0, n_blk)], ssem).wait()

    def anchor(width):
        zero = xnext[n_blk, 0:1, 0:width]
        osend[n_blk, 0:1, 0:width] = zero
        return zero + osend[n_blk, 0:1, 0:width]

    def expert(zn, k, move_rows=None):
        e = te_ref[i * n_eg + k]
        if move_rows is not None:
            move_rows(0)
        a = jnp.dot(zn, w1_ref[e], preferred_element_type=F32)
        if move_rows is not None:
            a = a + anchor(a.shape[1])
            move_rows(1)
        b = jnp.dot(zn, w3_ref[e], preferred_element_type=F32)
        hid = (a * jax.nn.sigmoid(a) * b).astype(BF16)
        route = xcur[:, d:]
        lane = lax.broadcasted_iota(jnp.int32, route.shape, 1)
        comb = jnp.sum(jnp.where(lane == e, route, 0.0), axis=1, keepdims=True)
        out = comb * jnp.dot(hid, w2_ref[e], preferred_element_type=F32)
        if move_rows is not None:
            out = out + anchor(d)
        return out

    busy = tv_ref[i] > 0
    after_busy = tv_ref[jnp.maximum(i - 1, 0)] > 0
    drain = jnp.logical_and(jnp.logical_not(busy), jnp.logical_and(i >= 1, after_busy))

    @pl.when(i == 0)
    def _():
        xnext[n_blk] = jnp.zeros(xnext.shape[1:], F32)
        osend[...] = jnp.zeros(osend.shape, F32)
        base = ts_ref[0]

        def body(blk, c):
            for k in range(rows):
                row_in(tok_ref[base + blk * rows + k], blk, k).start()
            return c
        lax.fori_loop(0, n_blk, body, 0)

    @pl.when(jnp.logical_or(busy, drain))
    def _():
        tile_in_wait()
        xcur[...] = xnext[0:n_blk].reshape(tm, xnext.shape[2])

    def hand_over_results():
        @pl.when(i >= 1)
        def _():
            tile_out_wait()
            osend[0:n_blk] = _rms(acc_ref[...], gf_ref[...]).reshape(n_blk, rows, d)

    n_always = 2

    @pl.when(busy)
    def _():
        h = xcur[:, :d]
        zn = _rms(h, g_ref[...]).astype(BF16)
        zn_ref[...] = zn
        nxt = ts_ref[i + 1]

        def gather_rows(part):
            for r in range(part * tm // 2, (part + 1) * tm // 2):
                row_in(tok_ref[nxt + r], r // rows, r % rows).start()
        acc0_ref[...] = h + expert(zn, 0, gather_rows)

    @pl.when(busy)
    def _():
        hand_over_results()

    @pl.when(busy)
    def _():
        prv = ts_ref[jnp.maximum(i - 1, 0)]

        def scatter_rows(part):
            for r in range(part * tm // 2, (part + 1) * tm // 2):
                row_out(tok_ref[prv + r], r // rows, r % rows).start()
        acc_ref[...] = acc0_ref[...] + expert(zn_ref[...], 1, scatter_rows)

    for k in range(n_always, n_eg):
        @pl.when(tn_ref[i] > k)
        def _():
            acc_ref[...] += expert(zn_ref[...], k)

    @pl.when(drain)
    def _():
        hand_over_results()
        base = ts_ref[i - 1]
        nv = tv_ref[i - 1]
        full = nv // rows

        def body(blk, c):
            for k in range(rows):
                row_out(tok_ref[base + blk * rows + k], blk, k).start()
            return c
        lax.fori_loop(0, full, body, 0)
        for k in range(rows - 1):

            @pl.when(full * rows + k < nv)
            def _():
                row_out(tok_ref[base + full * rows + k], full, k).start()

        @pl.when(full > 0)
        def _():
            pltpu.make_async_copy(osend.at[pl.ds(0, full)], osend.at[pl.ds(0, full)], ssem).wait()
        for k in range(rows - 1):

            @pl.when(full * rows + k < nv)
            def _():
                row_out(0, 0, 0).wait()


def _moe(hx, cls, counts, g, w1, w3, w2, g_final, tm):
    t, dx = hx.shape
    d = dx - LANES
    n_eg = EXPERTS_PER_GROUP
    de = w1.shape[2]
    tables = _dispatch_tables(cls, counts, tm)
    n_tiles = tables[2].shape[0]

    def wmap(i, tok, ts, tg, tv, tn, te):
        return (tg[i], 0, 0)

    def cmap(i, tok, ts, tg, tv, tn, te):
        return (0, 0)

    grid_spec = pltpu.PrefetchScalarGridSpec(
        num_scalar_prefetch=len(tables),
        grid=(n_tiles,),
        in_specs=[pl.BlockSpec(memory_space=pl.ANY),
                  pl.BlockSpec((1, d), cmap),
                  pl.BlockSpec((n_eg, d, de), wmap),
                  pl.BlockSpec((n_eg, d, de), wmap),
                  pl.BlockSpec((n_eg, de, d), wmap),
                  pl.BlockSpec((1, d), cmap)],
        out_specs=pl.BlockSpec(memory_space=pl.ANY),
        scratch_shapes=[pltpu.VMEM((tm // SUBLANES + 1, SUBLANES, dx), F32),
                        pltpu.VMEM((tm // SUBLANES + 1, SUBLANES, d), F32),
                        pltpu.VMEM((tm, dx), F32),
                        pltpu.VMEM((tm, d), BF16),
                        pltpu.VMEM((tm, d), F32),
                        pltpu.VMEM((tm, d), F32),
                        pltpu.SemaphoreType.DMA(()), pltpu.SemaphoreType.DMA(())])
    return pl.pallas_call(
        _moe_kernel,
        grid_spec=grid_spec,
        out_shape=jax.ShapeDtypeStruct((t, d), F32),
        compiler_params=pltpu.CompilerParams(dimension_semantics=("arbitrary",),
                                             vmem_limit_bytes=VMEM_LIMIT),
        name="moe",
    )(*tables, hx, g.reshape(1, d), w1, w3, w2,
      g_final.reshape(1, d))


def _layer(h2, mem2, batch, seq, p, l, g_final, tm):
    d = h2.shape[1]
    tm_mix = min(MIXER_TILE, seq)
    n_chunks = tm_mix // SSM_CHUNK
    n_scan = int(math.log2(n_chunks))
    tables = _ssm_tables(p["ssm_a_re"][l], p["ssm_a_im"][l], p["ssm_log_dt"][l], p["ssm_b_re"][l],
                         p["ssm_b_im"][l], p["ssm_c_re"][l], p["ssm_c_im"][l], p["ssm_d"][l],
                         SSM_CHUNK, n_scan, n_chunks // SUBLANES)
    h2 = _mixer(h2, seq, p["norm_mix"][l], p["w_in"][l], p["pool_w"][l], p["pool_scale"][l], tables,
                p["glu_w"][l], p["glu_b"][l], p["br_pool"][l], p["br_ssm"][l], p["w_out"][l],
                tm_mix)

    wqk, wvo = _kv_fold(mem2, batch, p["norm_mem"][l], p["x_wkv"][l], p["x_wq"][l], p["x_wo"][l])

    n_g, n_eg = N_EXPERT_GROUPS, EXPERTS_PER_GROUP
    w_r = jnp.concatenate([p["router_g_w"][l], p["router_e_w"][l].reshape(d, n_g * n_eg)], axis=1)
    w_r = jnp.pad(w_r, ((0, 0), (0, LANES - w_r.shape[1])))
    b_r = jnp.concatenate([p["router_g_b"][l], p["router_e_b"][l].reshape(n_g * n_eg)])
    b_r = jnp.pad(b_r, (0, LANES - b_r.shape[0])).reshape(1, LANES)
    hx, cls, counts, (w1b, w3b, w2b) = _xattn(
        h2, seq, p["norm_x"][l], wqk, wvo, p["norm_ffn"][l], w_r, b_r,
        p["exp_w1"][l], p["exp_w3"][l], p["exp_w2"][l], min(XATTN_TILE, seq))
    return _moe(hx, cls, counts, p["norm_ffn"][l], w1b, w3b, w2b, g_final, tm)


def kernel(x, mem, norm_mix, w_in, pool_w, pool_scale, ssm_a_re, ssm_a_im, ssm_log_dt, ssm_b_re,
           ssm_b_im, ssm_c_re, ssm_c_im, ssm_d, glu_w, glu_b, br_pool, br_ssm, w_out, norm_x,
           norm_mem, x_wq, x_wkv, x_wo, norm_ffn, router_g_w, router_g_b, router_e_w, router_e_b,
           exp_w1, exp_w3, exp_w2, norm_final):
    p = dict(norm_mix=norm_mix, w_in=w_in, pool_w=pool_w, pool_scale=pool_scale, ssm_a_re=ssm_a_re,
             ssm_a_im=ssm_a_im, ssm_log_dt=ssm_log_dt, ssm_b_re=ssm_b_re, ssm_b_im=ssm_b_im,
             ssm_c_re=ssm_c_re, ssm_c_im=ssm_c_im, ssm_d=ssm_d, glu_w=glu_w, glu_b=glu_b,
             br_pool=br_pool, br_ssm=br_ssm, w_out=w_out, norm_x=norm_x, norm_mem=norm_mem,
             x_wq=x_wq, x_wkv=x_wkv, x_wo=x_wo, norm_ffn=norm_ffn, router_g_w=router_g_w,
             router_g_b=router_g_b, router_e_w=router_e_w, router_e_b=router_e_b, exp_w1=exp_w1,
             exp_w3=exp_w3, exp_w2=exp_w2)
    batch, seq, d = x.shape
    assert norm_mix.shape[0] == 1, "final norm is fused into the (single) layer's last stage"
    tm = min(MOE_TILE, seq)
    assert seq % min(MIXER_TILE, seq) == 0 and seq % min(XATTN_TILE, seq) == 0 and (batch * seq) % tm == 0
    h2 = x.reshape(batch * seq, d)
    mem2 = mem.reshape(-1, d)
    out = _layer(h2, mem2, batch, seq, p, 0, norm_final, tm)
    return out.reshape(batch, seq, d)
```

```python
import functools
import math

import jax
import jax.numpy as jnp
from jax import lax
from jax.experimental import pallas as pl
from jax.experimental.pallas import tpu as pltpu

F32 = jnp.float32
BF16 = jnp.bfloat16

RMS_EPS = 1e-6
POOL_WINDOWS = (2, 4, 8, 16)
POOL_HALO = 16
SSM_GROUP_DIM = 16
SSM_STATE = 64
LANES = 128
SUBLANES = 8
OCT = LANES // SSM_GROUP_DIM
SSM_CHUNK = 4
STRAND_PAD = 8
X_HEADS = 4
MIXER_TILE = 1024
XATTN_TILE = 1024
MOE_TILE = 512
N_EXPERT_GROUPS = 4
EXPERTS_PER_GROUP = 4
EXPERT_PAIRS = ((0, 1), (0, 2), (0, 3), (1, 3), (1, 2), (2, 3))
N_CLASS_ROWS = 32
VMEM_LIMIT = 56 * 1024 * 1024


def _rms(x, g):
    return x * lax.rsqrt(jnp.mean(x * x, axis=-1, keepdims=True) + RMS_EPS) * g


def _const_spec(shape):
    nd = len(shape)
    return pl.BlockSpec(shape, lambda *_: (0,) * nd, pipeline_mode=pl.Buffered(1))


def _ssm_tables(a_re, a_im, log_dt, b_re, b_im, c_re, c_im, d_skip, chunk, n_scan, n_pow):
    hi = lax.Precision.HIGHEST
    G, N = a_re.shape
    C = SSM_GROUP_DIM
    Q = G // OCT
    dt = jnp.exp(log_dt)[:, None]
    mag = jnp.exp(a_re * dt)
    lb_re, lb_im = mag * jnp.cos(a_im * dt), mag * jnp.sin(a_im * dt)
    den = a_re * a_re + a_im * a_im
    nr, ni = lb_re - 1.0, lb_im
    f_re = ((nr * a_re + ni * a_im) / den)[..., None]
    f_im = ((ni * a_re - nr * a_im) / den)[..., None]
    bb_re = f_re * b_re - f_im * b_im
    bb_im = f_re * b_im + f_im * b_re

    pw_re, pw_im = [jnp.ones_like(lb_re)], [jnp.zeros_like(lb_re)]
    for _ in range(chunk):
        r, i = pw_re[-1], pw_im[-1]
        pw_re.append(r * lb_re - i * lb_im)
        pw_im.append(r * lb_im + i * lb_re)
    pw_re, pw_im = jnp.stack(pw_re), jnp.stack(pw_im)

    lbb_re = pw_re[:chunk, :, :, None] * bb_re - pw_im[:chunk, :, :, None] * bb_im
    lbb_im = pw_re[:chunk, :, :, None] * bb_im + pw_im[:chunk, :, :, None] * bb_re
    kern = (jnp.einsum('gon,dgnc->dgoc', c_re, lbb_re, precision=hi)
            - jnp.einsum('gon,dgnc->dgoc', c_im, lbb_im, precision=hi))
    kern = kern.at[0].add(d_skip.reshape(G, C)[:, :, None] * jnp.eye(C, dtype=F32))

    lane_g = jnp.arange(LANES) // C
    state_g = jnp.arange(OCT * N) // N
    rep_c = (jnp.arange(C)[:, None] == (jnp.arange(LANES) % C)[None, :]).astype(F32)
    rep_n = (jnp.arange(N)[:, None] == (jnp.arange(OCT * N) % N)[None, :]).astype(F32)
    same_cc = (lane_g[:, None] == lane_g[None, :]).astype(F32)
    same_cn = (lane_g[:, None] == state_g[None, :]).astype(F32)

    kt = kern.transpose(0, 1, 3, 2).reshape(chunk, Q, LANES, C)
    blk = (jnp.einsum('dqrc,cl->dqrl', kt, rep_c, precision=hi) * same_cc).astype(BF16)
    zero = jnp.zeros((Q, LANES, LANES), BF16)
    m_intra = jnp.concatenate(
        [jnp.concatenate([blk[jp - j] if jp >= j else zero for jp in range(chunk)], axis=2)
         for j in range(chunk)], axis=1)

    def state_half(lbb):
        z = lbb[::-1].transpose(0, 1, 3, 2).reshape(chunk, Q, LANES, N)
        w = jnp.einsum('jqrn,nl->qjrl', z, rep_n, precision=hi) * same_cn
        return w.reshape(Q, chunk * LANES, OCT * N).astype(BF16)
    w_state = jnp.concatenate([state_half(lbb_re), state_half(lbb_im)], axis=-1)

    p_re = c_re[None] * pw_re[1:, :, None, :] - c_im[None] * pw_im[1:, :, None, :]
    p_im = c_re[None] * pw_im[1:, :, None, :] + c_im[None] * pw_re[1:, :, None, :]
    def out_half(p):
        z = p.transpose(0, 1, 3, 2).reshape(chunk, Q, OCT * N, C)
        w = jnp.einsum('jqrc,cl->qrjl', z, rep_c, precision=hi)
        w = w * same_cn.T[None, :, None, :]
        return w.reshape(Q, OCT * N, chunk * LANES).astype(BF16)
    w_out = jnp.concatenate([out_half(p_re), out_half(-p_im)], axis=1)

    lr, li = [pw_re[chunk]], [pw_im[chunk]]
    for _ in range(n_scan - 1):
        r, i = lr[-1], li[-1]
        lr.append(r * r - i * i)
        li.append(2.0 * r * i)
    qr, qi = [jnp.ones_like(lb_re)], [jnp.zeros_like(lb_re)]
    for _ in range(n_pow - 1):
        r, i = qr[-1], qi[-1]
        qr.append(r * pw_re[chunk] - i * pw_im[chunk])
        qi.append(r * pw_im[chunk] + i * pw_re[chunk])
    lam_re = jnp.stack(lr + qr).reshape(n_scan + n_pow, G * N)
    lam_im = jnp.stack(li + qi).reshape(n_scan + n_pow, G * N)
    return m_intra, w_state, w_out, lam_re, lam_im


def _mixer_kernel(x_ref, nm_ref, win_ref, poolw_ref, pscale_ref, mi_ref, ws_ref, wh_ref,
                  lre_ref, lim_ref, gluw_ref, glub_ref, brp_ref, brs_ref, wout_ref,
                  o_ref, zext_ref, v_ref, y_ref, carry_ref, *, tiles_per_seq, chunk):
    tm, d_model = x_ref.shape
    pool_w = zext_ref.shape[1]
    n_oct = v_ref.shape[0]
    ssm_w = n_oct * LANES
    n_chunks = tm // chunk
    n_str = SUBLANES
    per = n_chunks // n_str
    span = per * chunk
    pitch = v_ref.shape[1] // n_str
    n_scan = lre_ref.shape[0] - per
    seq_tile = pl.program_id(0) % tiles_per_seq

    @pl.when(seq_tile == 0)
    def _():
        zext_ref[0:POOL_HALO, :] = jnp.zeros((POOL_HALO, pool_w), F32)
        carry_ref[...] = jnp.zeros_like(carry_ref)

    x = x_ref[...]
    u = _rms(x, nm_ref[...]).astype(BF16)
    proj = jnp.dot(u, win_ref[...], preferred_element_type=F32)
    z = proj[:, :pool_w]
    zext_ref[POOL_HALO:, :] = z
    for q in range(n_oct):
        for r in range(n_str):
            v_ref[q, r * pitch:r * pitch + span, :] = (
                proj[r * span:(r + 1) * span, pool_w + q * LANES:pool_w + (q + 1) * LANES])
    g_pool = jax.nn.sigmoid(proj[:, pool_w + ssm_w:pool_w + ssm_w + d_model])
    g_ssm = jax.nn.sigmoid(proj[:, pool_w + ssm_w + d_model:])

    pos = (seq_tile * tm + lax.broadcasted_iota(jnp.int32, (tm, 1), 0) + 1).astype(F32)
    gdim = pool_w // len(POOL_WINDOWS)
    ypool = []
    for gi, w in enumerate(POOL_WINDOWS):
        cols = slice(gi * gdim, (gi + 1) * gdim)
        acc = z[:, cols]
        for dlt in range(1, w):
            acc = acc + zext_ref[POOL_HALO - dlt:POOL_HALO - dlt + tm, cols]
        diff = acc / jnp.minimum(pos, float(w)) - z[:, cols]
        ypool.append(jnp.dot(diff.astype(BF16), poolw_ref[gi], preferred_element_type=F32))
    ypool = jnp.concatenate(ypool, axis=1) * pscale_ref[...]
    zext_ref[0:POOL_HALO, :] = zext_ref[tm:tm + POOL_HALO, :]
    p_br = jnp.dot(ypool.astype(BF16), brp_ref[...], preferred_element_type=F32)

    srow = lax.broadcasted_iota(jnp.int32, (n_str, 1), 0)
    half = OCT * SSM_STATE
    log_per = per.bit_length() - 1
    for q in range(n_oct):
        st = slice(q * half, (q + 1) * half)

        def lam(k):
            return lre_ref[k:k + 1, st], lim_ref[k:k + 1, st]

        vq = jnp.concatenate(
            [jnp.concatenate([v_ref[q, pl.ds(b * chunk + j, n_str, stride=pitch), :]
                              for j in range(chunk)], axis=1) for b in range(per)],
            axis=0).astype(BF16)
        s = jnp.dot(vq, ws_ref[q], preferred_element_type=F32)
        a_re, a_im = lam(0)
        loc_re, loc_im = [s[0:n_str, :half]], [s[0:n_str, half:]]
        for b in range(1, per):
            p_re, p_im = loc_re[-1], loc_im[-1]
            loc_re.append(s[b * n_str:(b + 1) * n_str, :half] + a_re * p_re - a_im * p_im)
            loc_im.append(s[b * n_str:(b + 1) * n_str, half:] + a_re * p_im + a_im * p_re)
        c_re, c_im = carry_ref[0:1, st], carry_ref[1:2, st]
        e_re, e_im = loc_re[-1], loc_im[-1]
        l_re, l_im = lam(log_per)
        e_re = e_re + jnp.where(srow == 0, l_re * c_re - l_im * c_im, 0.0)
        e_im = e_im + jnp.where(srow == 0, l_re * c_im + l_im * c_re, 0.0)
        for k in range(n_str.bit_length() - 1):
            sh = 1 << k
            r_re = jnp.where(srow >= sh, pltpu.roll(e_re, sh, axis=0), 0.0)
            r_im = jnp.where(srow >= sh, pltpu.roll(e_im, sh, axis=0), 0.0)
            l_re, l_im = lam(log_per + k)
            e_re, e_im = e_re + l_re * r_re - l_im * r_im, e_im + l_re * r_im + l_im * r_re
        carry_ref[0:1, st] = e_re[n_str - 1:n_str, :]
        carry_ref[1:2, st] = e_im[n_str - 1:n_str, :]
        i_re = jnp.where(srow == 0, c_re, pltpu.roll(e_re, 1, axis=0))
        i_im = jnp.where(srow == 0, c_im, pltpu.roll(e_im, 1, axis=0))
        h0_re, h0_im = [i_re], [i_im]
        for b in range(1, per):
            l_re, l_im = lam(n_scan + b)
            h0_re.append(loc_re[b - 1] + l_re * i_re - l_im * i_im)
            h0_im.append(loc_im[b - 1] + l_re * i_im + l_im * i_re)
        h0_re = jnp.concatenate(h0_re, axis=0).astype(BF16)
        h0_im = jnp.concatenate(h0_im, axis=0).astype(BF16)
        yq = (jnp.dot(vq, mi_ref[q], preferred_element_type=F32)
              + jnp.dot(h0_re, wh_ref[q, :half, :], preferred_element_type=F32)
              + jnp.dot(h0_im, wh_ref[q, half:, :], preferred_element_type=F32))
        for b in range(per):
            for j in range(chunk):
                y_ref[q, pl.ds(b * chunk + j, n_str, stride=pitch), :] = (
                    yq[b * n_str:(b + 1) * n_str, j * LANES:(j + 1) * LANES])

    y = jnp.concatenate(
        [jnp.concatenate([y_ref[q, r * pitch:r * pitch + span, :] for r in range(n_str)], axis=0)
         for q in range(n_oct)], axis=1)
    ys = jax.nn.gelu(y, approximate=True)
    glu = jnp.dot(ys.astype(BF16), gluw_ref[...], preferred_element_type=F32) + glub_ref[...]
    ys = ys * jax.nn.sigmoid(glu)
    s_br = jnp.dot(ys.astype(BF16), brs_ref[...], preferred_element_type=F32)

    merged = g_pool * p_br + g_ssm * s_br
    o_ref[...] = x + jnp.dot(merged.astype(BF16), wout_ref[...], preferred_element_type=F32)


def _mixer(x2, seq, nm, w_in, pool_w, pool_scale, tables, glu_w, glu_b, br_pool, br_ssm, w_out, tm):
    t, d = x2.shape
    mi, ws, wh, lre, lim = tables
    pool_width = pool_w.shape[0] * pool_w.shape[1]
    ssm_width = glu_w.shape[0]
    consts = [nm.reshape(1, d), w_in.astype(BF16), pool_w.astype(BF16),
              pool_scale.reshape(1, pool_width), mi, ws, wh, lre, lim,
              glu_w.astype(BF16), glu_b.reshape(1, ssm_width), br_pool.astype(BF16),
              br_ssm.astype(BF16), w_out.astype(BF16)]
    kern = functools.partial(_mixer_kernel, tiles_per_seq=seq // tm, chunk=SSM_CHUNK)
    return pl.pallas_call(
        kern,
        grid=(t // tm,),
        in_specs=[pl.BlockSpec((tm, d), lambda i: (i, 0))] + [_const_spec(c.shape) for c in consts],
        out_specs=pl.BlockSpec((tm, d), lambda i: (i, 0)),
        out_shape=jax.ShapeDtypeStruct((t, d), F32),
        scratch_shapes=[pltpu.VMEM((POOL_HALO + tm, pool_width), F32),
                        pltpu.VMEM((ssm_width // LANES, tm + SUBLANES * STRAND_PAD, LANES), F32),
                        pltpu.VMEM((ssm_width // LANES, tm + SUBLANES * STRAND_PAD, LANES), F32),
                        pltpu.VMEM((2, lre.shape[1]), F32)],
        compiler_params=pltpu.CompilerParams(dimension_semantics=("arbitrary",),
                                             vmem_limit_bytes=VMEM_LIMIT),
        name="mixer",
    )(x2, *consts)


def _kv_fold_kernel(mem_ref, g_ref, wkv32_ref, wq32_ref, wo32_ref, wqk_ref, wvo_ref,
                    wkv_ref, wq_ref, wo_ref):
    m, d = mem_ref.shape
    hd = d // X_HEADS

    @pl.when(pl.program_id(0) == 0)
    def _():
        wkv_ref[...] = wkv32_ref[...].astype(BF16)
        wq_ref[...] = wq32_ref[...].astype(BF16)
        wo_ref[...] = wo32_ref[...].astype(BF16)

    mn = _rms(mem_ref[...], g_ref[...]).astype(BF16)
    kv = jnp.dot(mn, wkv_ref[...], preferred_element_type=F32)
    for h in range(X_HEADS):
        k_h = kv[:, h * hd:(h + 1) * hd].astype(BF16)
        v_h = kv[:, d + h * hd:d + (h + 1) * hd].astype(BF16)
        qk = lax.dot_general(wq_ref[:, h * hd:(h + 1) * hd], k_h, (((1,), (1,)), ((), ())),
                             preferred_element_type=F32)
        wqk_ref[0, :, h * m:(h + 1) * m] = qk.astype(BF16)
        wvo_ref[0, h * m:(h + 1) * m, :] = jnp.dot(
            v_h, wo_ref[h * hd:(h + 1) * hd, :], preferred_element_type=F32).astype(BF16)


def _kv_fold(mem2, batch, g, w_kv, w_q, w_o):
    d = mem2.shape[1]
    m = mem2.shape[0] // batch
    hm = X_HEADS * m
    return pl.pallas_call(
        _kv_fold_kernel,
        grid=(batch,),
        in_specs=[pl.BlockSpec((m, d), lambda b: (b, 0)), _const_spec((1, d)),
                  _const_spec(w_kv.shape), _const_spec(w_q.shape), _const_spec(w_o.shape)],
        out_specs=[pl.BlockSpec((1, d, hm), lambda b: (b, 0, 0)),
                   pl.BlockSpec((1, hm, d), lambda b: (b, 0, 0))],
        out_shape=[jax.ShapeDtypeStruct((batch, d, hm), BF16),
                   jax.ShapeDtypeStruct((batch, hm, d), BF16)],
        scratch_shapes=[pltpu.VMEM(w_kv.shape, BF16), pltpu.VMEM(w_q.shape, BF16),
                        pltpu.VMEM(w_o.shape, BF16)],
        compiler_params=pltpu.CompilerParams(dimension_semantics=("arbitrary",),
                                             vmem_limit_bytes=VMEM_LIMIT),
        name="kv_fold",
    )(mem2, g.reshape(1, d), w_kv, w_q, w_o)


def _route(logits):
    n_g, n_eg = N_EXPERT_GROUPS, EXPERTS_PER_GROUP
    tm = logits.shape[0]
    lt = logits.T
    neg = jnp.float32(-jnp.inf)
    big = jnp.int32(1 << 20)

    def first_max(v):
        row = lax.broadcasted_iota(jnp.int32, v.shape, 0)
        m = jnp.max(v, axis=0, keepdims=True)
        return m, jnp.min(jnp.where(v == m, row, big), axis=0, keepdims=True)

    gl = lt[0:n_g]
    gmax, g_idx = first_max(gl)
    g_gate = 1.0 / jnp.sum(jnp.exp(gl - gmax), axis=0, keepdims=True)
    el = lt[n_g:n_g + n_eg]
    for g in range(1, n_g):
        el = jnp.where(g_idx == g, lt[n_g + g * n_eg:n_g + (g + 1) * n_eg], el)
    row = lax.broadcasted_iota(jnp.int32, el.shape, 0)
    t1, i1 = first_max(el)
    t2, i2 = first_max(jnp.where(row == i1, neg, el))
    e2 = jnp.exp(t2 - t1)
    w1 = 1.0 / (1.0 + e2)
    w2 = e2 / (1.0 + e2)
    comb = jnp.where(row == i1, w1, jnp.where(row == i2, w2, 0.0)) * g_gate
    lo, hi = jnp.minimum(i1, i2), jnp.maximum(i1, i2)
    pair = jnp.zeros_like(lo)
    for idx, (e_lo, e_hi) in enumerate(EXPERT_PAIRS):
        pair = jnp.where((lo == e_lo) & (hi == e_hi), idx, pair)
    cls = g_idx * len(EXPERT_PAIRS) + pair
    info = jnp.concatenate([comb, jnp.zeros((LANES - n_eg, tm), F32)], axis=0)
    hist = jnp.sum((cls == lax.broadcasted_iota(jnp.int32, (N_CLASS_ROWS, tm), 0)).astype(F32),
                   axis=1, keepdims=True)
    return info.T, cls, hist


def _dot_bf16x3(x, w_hilo):
    n = w_hilo.shape[1] // 2
    x_hi = x.astype(BF16)
    x_lo = (x - x_hi.astype(F32)).astype(BF16)
    both = jnp.dot(x_hi, w_hilo, preferred_element_type=F32)
    return both[:, :n] + both[:, n:] + jnp.dot(x_lo, w_hilo[:, :n], preferred_element_type=F32)


def _xattn_kernel(h_ref, g_ref, wqk_ref, wvo_ref, gffn_ref, wr_ref, br_ref, w1f_ref, w3f_ref, w2f_ref,
                  o_ref, cls_ref, hist_ref, w1b_ref, w3b_ref, w2b_ref, *, scale):
    w1b_ref[...] = w1f_ref[...].astype(BF16)
    w3b_ref[...] = w3f_ref[...].astype(BF16)
    w2b_ref[...] = w2f_ref[...].astype(BF16)

    h = h_ref[...]
    d = h.shape[1]
    m = wqk_ref.shape[2] // X_HEADS
    hn = _rms(h, g_ref[...]).astype(BF16)
    s_all = jnp.dot(hn, wqk_ref[0], preferred_element_type=F32) * scale
    probs = []
    for hh in range(X_HEADS):
        s = s_all[:, hh * m:(hh + 1) * m]
        s = s - jnp.max(s, axis=-1, keepdims=True)
        p = jnp.exp(s)
        probs.append((p / jnp.sum(p, axis=-1, keepdims=True)).astype(BF16))
    h_out = h + jnp.dot(jnp.concatenate(probs, axis=1), wvo_ref[0], preferred_element_type=F32)
    zn = _rms(h_out, gffn_ref[...])
    logits = _dot_bf16x3(zn, wr_ref[...]) + br_ref[...]
    route, cls, hist = _route(logits)
    o_ref[:, :d] = h_out
    o_ref[:, d:] = route
    cls_ref[0] = cls
    hist_ref[0] = jnp.broadcast_to(hist, hist_ref.shape[1:])


def _xattn(h2, seq, g, wqk, wvo, g_ffn, w_r, b_r, w1, w3, w2, tm):
    t, d = h2.shape
    tiles_per_seq = seq // tm
    n_tiles = t // tm
    n_e = w1.shape[0]
    assert n_tiles >= 2 * n_e, "the expert-weight cast needs two grid steps per expert"
    kern = functools.partial(_xattn_kernel, scale=(d // X_HEADS) ** -0.5)

    def half_expert(i):
        j = jnp.minimum(i, 2 * n_e - 1)
        return (j // 2, j % 2, 0)

    def half_block(w):
        return pl.BlockSpec((1, w.shape[1] // 2, w.shape[2]), half_expert)
    w_r_hi = w_r.astype(BF16)
    w_r_lo = (w_r - w_r_hi.astype(F32)).astype(BF16)
    w_r_hilo = jnp.concatenate([w_r_hi, w_r_lo], axis=1)
    hx, cls, hist, w1b, w3b, w2b = pl.pallas_call(
        kern,
        grid=(n_tiles,),
        in_specs=[pl.BlockSpec((tm, d), lambda i: (i, 0)), _const_spec((1, d)),
                  pl.BlockSpec((1,) + wqk.shape[1:], lambda i: (i // tiles_per_seq, 0, 0)),
                  pl.BlockSpec((1,) + wvo.shape[1:], lambda i: (i // tiles_per_seq, 0, 0)),
                  _const_spec((1, d)), _const_spec(w_r_hilo.shape), _const_spec(b_r.shape),
                  half_block(w1), half_block(w3), half_block(w2)],
        out_specs=[pl.BlockSpec((tm, d + LANES), lambda i: (i, 0)),
                   pl.BlockSpec((1, 1, tm), lambda i: (i, 0, 0)),
                   pl.BlockSpec((1, N_CLASS_ROWS, LANES), lambda i: (i, 0, 0)),
                   half_block(w1), half_block(w3), half_block(w2)],
        out_shape=[jax.ShapeDtypeStruct((t, d + LANES), F32),
                   jax.ShapeDtypeStruct((n_tiles, 1, tm), jnp.int32),
                   jax.ShapeDtypeStruct((n_tiles, N_CLASS_ROWS, LANES), F32),
                   jax.ShapeDtypeStruct(w1.shape, BF16), jax.ShapeDtypeStruct(w3.shape, BF16),
                   jax.ShapeDtypeStruct(w2.shape, BF16)],
        compiler_params=pltpu.CompilerParams(dimension_semantics=("arbitrary",),
                                             vmem_limit_bytes=VMEM_LIMIT),
        name="xattn",
    )(h2, g.reshape(1, d), wqk, wvo, g_ffn.reshape(1, d), w_r_hilo, b_r, w1, w3, w2)
    return hx, cls.reshape(t), jnp.sum(hist[:, :, 0], axis=0).astype(jnp.int32), (w1b, w3b, w2b)


def _dispatch_tables(cls, counts, tm):
    t = cls.shape[0]
    n_g, n_eg, n_pair = N_EXPERT_GROUPS, EXPERTS_PER_GROUP, len(EXPERT_PAIRS)
    n_tiles = t // tm + n_g
    i32 = jnp.int32
    assert n_g * n_pair * t < 2 ** 31
    tok_sorted = lax.sort(cls * t + jnp.arange(t, dtype=i32)) % t
    tok_sorted = jnp.concatenate([tok_sorted, jnp.zeros((tm,), i32)])
    counts = counts[:n_g * n_pair]
    c_start = jnp.cumsum(counts) - counts
    g_count = counts.reshape(n_g, n_pair).sum(axis=1)
    g_start = jnp.cumsum(g_count) - g_count
    tiles_g = (g_count + tm - 1) // tm
    tile_end = jnp.cumsum(tiles_g)
    tile_first = tile_end - tiles_g
    tile_ids = jnp.arange(n_tiles, dtype=i32)
    tile_group = jnp.minimum(jnp.sum((tile_ids[:, None] >= tile_end[None, :]).astype(i32), axis=1),
                             n_g - 1)
    k_in_group = tile_ids - tile_first[tile_group]
    tile_valid = jnp.clip(g_count[tile_group] - k_in_group * tm, 0, tm)
    tile_valid = jnp.where(tile_ids < tile_end[-1], tile_valid, 0)
    tile_start = jnp.where(tile_valid > 0, g_start[tile_group] + k_in_group * tm, 0)
    lo, hi = tile_start[:, None], (tile_start + tile_valid)[:, None]
    inter = (counts[None, :] > 0) & (c_start[None, :] < hi) & ((c_start + counts)[None, :] > lo)
    inter = inter.reshape(n_tiles, n_g, n_pair).any(axis=1)
    pair_has = jnp.array([[e in p for e in range(n_eg)] for p in EXPERT_PAIRS])
    need = (inter[:, :, None] & pair_has[None]).any(axis=1)
    tile_nexp = need.sum(axis=1)
    tile_exp = jnp.argsort(jnp.logical_not(need), axis=1, stable=True)
    return (tok_sorted, tile_start.astype(i32), tile_group.astype(i32), tile_valid.astype(i32),
            tile_nexp.astype(i32), tile_exp.reshape(-1).astype(i32))


def _moe_kernel(tok_ref, ts_ref, tg_ref, tv_ref, tn_ref, te_ref, hx_hbm, g_ref, w1_ref, w3_ref, w2_ref,
                gf_ref, out_hbm, xnext, osend, xcur, zn_ref, acc0_ref, acc_ref, gsem, ssem):
    i = pl.program_id(0)
    n_blk, rows = xnext.shape[0] - 1, xnext.shape[1]
    tm = n_blk * rows
    d = osend.shape[2]
    n_eg = w1_ref.shape[0]

    def row_in(tok, blk, k):
        return pltpu.make_async_copy(hx_hbm.at[pl.ds(tok, 1), :], xnext.at[blk, pl.ds(k, 1), :], gsem)

    def row_out(tok, blk, k):
        return pltpu.make_async_copy(osend.at[blk, pl.ds(k, 1), :], out_hbm.at[pl.ds(tok, 1), :], ssem)

    def tile_in_wait():
        pltpu.make_async_copy(xnext.at[pl.ds(0, n_blk)], xnext.at[pl.ds(0, n_blk)], gsem).wait()

    def tile_out_wait():
        pltpu.make_async_copy(osend.at[pl.ds(0, n_blk)], osend.at[pl.ds(0, n_blk)], ssem).wait()

    def anchor(width):
        zero = xnext[n_blk, 0:1, 0:width]
        osend[n_blk, 0:1, 0:width] = zero
        return zero + osend[n_blk, 0:1, 0:width]

    def expert(zn, k, move_rows=None):
        e = te_ref[i * n_eg + k]
        if move_rows is not None:
            move_rows(0)
        a = jnp.dot(zn, w1_ref[e], preferred_element_type=F32)
        if move_rows is not None:
            a = a + anchor(a.shape[1])
            move_rows(1)
        b = jnp.dot(zn, w3_ref[e], preferred_element_type=F32)
        hid = (a * jax.nn.sigmoid(a) * b).astype(BF16)
        route = xcur[:, d:]
        lane = lax.broadcasted_iota(jnp.int32, route.shape, 1)
        comb = jnp.sum(jnp.where(lane == e, route, 0.0), axis=1, keepdims=True)
        out = comb * jnp.dot(hid, w2_ref[e], preferred_element_type=F32)
        if move_rows is not None:
            out = out + anchor(d)
        return out

    busy = tv_ref[i] > 0
    after_busy = tv_ref[jnp.maximum(i - 1, 0)] > 0
    drain = jnp.logical_and(jnp.logical_not(busy), jnp.logical_and(i >= 1, after_busy))

    @pl.when(i == 0)
    def _():
        xnext[n_blk] = jnp.zeros(xnext.shape[1:], F32)
        osend[...] = jnp.zeros(osend.shape, F32)
        base = ts_ref[0]

        def body(blk, c):
            for k in range(rows):
                row_in(tok_ref[base + blk * rows + k], blk, k).start()
            return c
        lax.fori_loop(0, n_blk, body, 0)

    @pl.when(jnp.logical_or(busy, drain))
    def _():
        tile_in_wait()
        xcur[...] = xnext[0:n_blk].reshape(tm, xnext.shape[2])

    def hand_over_results():
        @pl.when(i >= 1)
        def _():
            tile_out_wait()
            osend[0:n_blk] = _rms(acc_ref[...], gf_ref[...]).reshape(n_blk, rows, d)

    n_always = 2

    @pl.when(busy)
    def _():
        h = xcur[:, :d]
        zn = _rms(h, g_ref[...]).astype(BF16)
        zn_ref[...] = zn
        nxt = ts_ref[i + 1]

        def gather_rows(part):
            for r in range(part * tm // 2, (part + 1) * tm // 2):
                row_in(tok_ref[nxt + r], r // rows, r % rows).start()
        acc0_ref[...] = h + expert(zn, 0, gather_rows)

    @pl.when(busy)
    def _():
        hand_over_results()

    @pl.when(busy)
    def _():
        prv = ts_ref[jnp.maximum(i - 1, 0)]

        def scatter_rows(part):
            for r in range(part * tm // 2, (part + 1) * tm // 2):
                row_out(tok_ref[prv + r], r // rows, r % rows).start()
        acc_ref[...] = acc0_ref[...] + expert(zn_ref[...], 1, scatter_rows)

    for k in range(n_always, n_eg):
        @pl.when(tn_ref[i] > k)
        def _():
            acc_ref[...] += expert(zn_ref[...], k)

    @pl.when(drain)
    def _():
        hand_over_results()
        base = ts_ref[i - 1]
        nv = tv_ref[i - 1]
        full = nv // rows

        def body(blk, c):
            for k in range(rows):
                row_out(tok_ref[base + blk * rows + k], blk, k).start()
            return c
        lax.fori_loop(0, full, body, 0)
        for k in range(rows - 1):

            @pl.when(full * rows + k < nv)
            def _():
                row_out(tok_ref[base + full * rows + k], full, k).start()

        @pl.when(full > 0)
        def _():
            pltpu.make_async_copy(osend.at[pl.ds(0, full)], osend.at[pl.ds(0, full)], ssem).wait()
        for k in range(rows - 1):

            @pl.when(full * rows + k < nv)
            def _():
                row_out(0, 0, 0).wait()


def _moe(hx, cls, counts, g, w1, w3, w2, g_final, tm):
    t, dx = hx.shape
    d = dx - LANES
    n_eg = EXPERTS_PER_GROUP
    de = w1.shape[2]
    tables = _dispatch_tables(cls, counts, tm)
    n_tiles = tables[2].shape[0]

    def wmap(i, tok, ts, tg, tv, tn, te):
        return (tg[i], 0, 0)

    def cmap(i, tok, ts, tg, tv, tn, te):
        return (0, 0)

    grid_spec = pltpu.PrefetchScalarGridSpec(
        num_scalar_prefetch=len(tables),
        grid=(n_tiles,),
        in_specs=[pl.BlockSpec(memory_space=pl.ANY),
                  pl.BlockSpec((1, d), cmap),
                  pl.BlockSpec((n_eg, d, de), wmap),
                  pl.BlockSpec((n_eg, d, de), wmap),
                  pl.BlockSpec((n_eg, de, d), wmap),
                  pl.BlockSpec((1, d), cmap)],
        out_specs=pl.BlockSpec(memory_space=pl.ANY),
        scratch_shapes=[pltpu.VMEM((tm // SUBLANES + 1, SUBLANES, dx), F32),
                        pltpu.VMEM((tm // SUBLANES + 1, SUBLANES, d), F32),
                        pltpu.VMEM((tm, dx), F32),
                        pltpu.VMEM((tm, d), BF16),
                        pltpu.VMEM((tm, d), F32),
                        pltpu.VMEM((tm, d), F32),
                        pltpu.SemaphoreType.DMA(()), pltpu.SemaphoreType.DMA(())])
    return pl.pallas_call(
        _moe_kernel,
        grid_spec=grid_spec,
        out_shape=jax.ShapeDtypeStruct((t, d), F32),
        compiler_params=pltpu.CompilerParams(dimension_semantics=("arbitrary",),
                                             vmem_limit_bytes=VMEM_LIMIT),
        name="moe",
    )(*tables, hx, g.reshape(1, d), w1, w3, w2,
      g_final.reshape(1, d))


def _layer(h2, mem2, batch, seq, p, l, g_final, tm):
    d = h2.shape[1]
    tm_mix = min(MIXER_TILE, seq)
    n_chunks = tm_mix // SSM_CHUNK
    n_scan = int(math.log2(n_chunks))
    tables = _ssm_tables(p["ssm_a_re"][l], p["ssm_a_im"][l], p["ssm_log_dt"][l], p["ssm_b_re"][l],
                         p["ssm_b_im"][l], p["ssm_c_re"][l], p["ssm_c_im"][l], p["ssm_d"][l],
                         SSM_CHUNK, n_scan, n_chunks // SUBLANES)
    h2 = _mixer(h2, seq, p["norm_mix"][l], p["w_in"][l], p["pool_w"][l], p["pool_scale"][l], tables,
                p["glu_w"][l], p["glu_b"][l], p["br_pool"][l], p["br_ssm"][l], p["w_out"][l],
                tm_mix)

    wqk, wvo = _kv_fold(mem2, batch, p["norm_mem"][l], p["x_wkv"][l], p["x_wq"][l], p["x_wo"][l])

    n_g, n_eg = N_EXPERT_GROUPS, EXPERTS_PER_GROUP
    w_r = jnp.concatenate([p["router_g_w"][l], p["router_e_w"][l].reshape(d, n_g * n_eg)], axis=1)
    w_r = jnp.pad(w_r, ((0, 0), (0, LANES - w_r.shape[1])))
    b_r = jnp.concatenate([p["router_g_b"][l], p["router_e_b"][l].reshape(n_g * n_eg)])
    b_r = jnp.pad(b_r, (0, LANES - b_r.shape[0])).reshape(1, LANES)
    hx, cls, counts, (w1b, w3b, w2b) = _xattn(
        h2, seq, p["norm_x"][l], wqk, wvo, p["norm_ffn"][l], w_r, b_r,
        p["exp_w1"][l], p["exp_w3"][l], p["exp_w2"][l], min(XATTN_TILE, seq))
    return _moe(hx, cls, counts, p["norm_ffn"][l], w1b, w3b, w2b, g_final, tm)


def kernel(x, mem, norm_mix, w_in, pool_w, pool_scale, ssm_a_re, ssm_a_im, ssm_log_dt, ssm_b_re,
           ssm_b_im, ssm_c_re, ssm_c_im, ssm_d, glu_w, glu_b, br_pool, br_ssm, w_out, norm_x,
           norm_mem, x_wq, x_wkv, x_wo, norm_ffn, router_g_w, router_g_b, router_e_w, router_e_b,
           exp_w1, exp_w3, exp_w2, norm_final):
    p = dict(norm_mix=norm_mix, w_in=w_in, pool_w=pool_w, pool_scale=pool_scale, ssm_a_re=ssm_a_re,
             ssm_a_im=ssm_a_im, ssm_log_dt=ssm_log_dt, ssm_b_re=ssm_b_re, ssm_b_im=ssm_b_im,
             ssm_c_re=ssm_c_re, ssm_c_im=ssm_c_im, ssm_d=ssm_d, glu_w=glu_w, glu_b=glu_b,
             br_pool=br_pool, br_ssm=br_ssm, w_out=w_out, norm_x=norm_x, norm_mem=norm_mem,
             x_wq=x_wq, x_wkv=x_wkv, x_wo=x_wo, norm_ffn=norm_ffn, router_g_w=router_g_w,
             router_g_b=router_g_b, router_e_w=router_e_w, router_e_b=router_e_b, exp_w1=exp_w1,
             exp_w3=exp_w3, exp_w2=exp_w2)
    batch, seq, d = x.shape
    assert norm_mix.shape[0] == 1, "final norm is fused into the (single) layer's last stage"
    tm = min(MOE_TILE, seq)
    assert seq % min(MIXER_TILE, seq) == 0 and seq % min(XATTN_TILE, seq) == 0 and (batch * seq) % tm == 0
    h2 = x.reshape(batch * seq, d)
    mem2 = mem.reshape(-1, d)
    out = _layer(h2, mem2, batch, seq, p, 0, norm_final, tm)
    return out.reshape(batch, seq, d)
```

```python
import functools
import math

import jax
import jax.numpy as jnp
from jax import lax
from jax.experimental import pallas as pl
from jax.experimental.pallas import tpu as pltpu

F32 = jnp.float32
BF16 = jnp.bfloat16

RMS_EPS = 1e-6
POOL_WINDOWS = (2, 4, 8, 16)
POOL_HALO = 16
SSM_GROUP_DIM = 16
SSM_STATE = 64
LANES = 128
SUBLANES = 8
OCT = LANES // SSM_GROUP_DIM
SSM_CHUNK = 4
STRAND_PAD = 8
X_HEADS = 4
MIXER_TILE = 1024
XATTN_TILE = 1024
MOE_TILE = 512
N_EXPERT_GROUPS = 4
EXPERTS_PER_GROUP = 4
EXPERT_PAIRS = ((0, 1), (0, 2), (0, 3), (1, 3), (1, 2), (2, 3))
N_CLASS_ROWS = 32
VMEM_LIMIT = 56 * 1024 * 1024


def _rms(x, g):
    return x * lax.rsqrt(jnp.mean(x * x, axis=-1, keepdims=True) + RMS_EPS) * g


def _const_spec(shape):
    nd = len(shape)
    return pl.BlockSpec(shape, lambda *_: (0,) * nd, pipeline_mode=pl.Buffered(1))


def _ssm_tables(a_re, a_im, log_dt, b_re, b_im, c_re, c_im, d_skip, chunk, n_scan, n_pow):
    hi = lax.Precision.HIGHEST
    G, N = a_re.shape
    C = SSM_GROUP_DIM
    Q = G // OCT
    dt = jnp.exp(log_dt)[:, None]
    mag = jnp.exp(a_re * dt)
    lb_re, lb_im = mag * jnp.cos(a_im * dt), mag * jnp.sin(a_im * dt)
    den = a_re * a_re + a_im * a_im
    nr, ni = lb_re - 1.0, lb_im
    f_re = ((nr * a_re + ni * a_im) / den)[..., None]
    f_im = ((ni * a_re - nr * a_im) / den)[..., None]
    bb_re = f_re * b_re - f_im * b_im
    bb_im = f_re * b_im + f_im * b_re

    pw_re, pw_im = [jnp.ones_like(lb_re)], [jnp.zeros_like(lb_re)]
    for _ in range(chunk):
        r, i = pw_re[-1], pw_im[-1]
        pw_re.append(r * lb_re - i * lb_im)
        pw_im.append(r * lb_im + i * lb_re)
    pw_re, pw_im = jnp.stack(pw_re), jnp.stack(pw_im)

    lbb_re = pw_re[:chunk, :, :, None] * bb_re - pw_im[:chunk, :, :, None] * bb_im
    lbb_im = pw_re[:chunk, :, :, None] * bb_im + pw_im[:chunk, :, :, None] * bb_re
    kern = (jnp.einsum('gon,dgnc->dgoc', c_re, lbb_re, precision=hi)
            - jnp.einsum('gon,dgnc->dgoc', c_im, lbb_im, precision=hi))
    kern = kern.at[0].add(d_skip.reshape(G, C)[:, :, None] * jnp.eye(C, dtype=F32))

    lane_g = jnp.arange(LANES) // C
    state_g = jnp.arange(OCT * N) // N
    rep_c = (jnp.arange(C)[:, None] == (jnp.arange(LANES) % C)[None, :]).astype(F32)
    rep_n = (jnp.arange(N)[:, None] == (jnp.arange(OCT * N) % N)[None, :]).astype(F32)
    same_cc = (lane_g[:, None] == lane_g[None, :]).astype(F32)
    same_cn = (lane_g[:, None] == state_g[None, :]).astype(F32)

    kt = kern.transpose(0, 1, 3, 2).reshape(chunk, Q, LANES, C)
    blk = (jnp.einsum('dqrc,cl->dqrl', kt, rep_c, precision=hi) * same_cc).astype(BF16)
    zero = jnp.zeros((Q, LANES, LANES), BF16)
    m_intra = jnp.concatenate(
        [jnp.concatenate([blk[jp - j] if jp >= j else zero for jp in range(chunk)], axis=2)
         for j in range(chunk)], axis=1)

    def state_half(lbb):
        z = lbb[::-1].transpose(0, 1, 3, 2).reshape(chunk, Q, LANES, N)
        w = jnp.einsum('jqrn,nl->qjrl', z, rep_n, precision=hi) * same_cn
        return w.reshape(Q, chunk * LANES, OCT * N).astype(BF16)
    w_state = jnp.concatenate([state_half(lbb_re), state_half(lbb_im)], axis=-1)

    p_re = c_re[None] * pw_re[1:, :, None, :] - c_im[None] * pw_im[1:, :, None, :]
    p_im = c_re[None] * pw_im[1:, :, None, :] + c_im[None] * pw_re[1:, :, None, :]
    def out_half(p):
        z = p.transpose(0, 1, 3, 2).reshape(chunk, Q, OCT * N, C)
        w = jnp.einsum('jqrc,cl->qrjl', z, rep_c, precision=hi)
        w = w * same_cn.T[None, :, None, :]
        return w.reshape(Q, OCT * N, chunk * LANES).astype(BF16)
    w_out = jnp.concatenate([out_half(p_re), out_half(-p_im)], axis=1)

    lr, li = [pw_re[chunk]], [pw_im[chunk]]
    for _ in range(n_scan - 1):
        r, i = lr[-1], li[-1]
        lr.append(r * r - i * i)
        li.append(2.0 * r * i)
    qr, qi = [jnp.ones_like(lb_re)], [jnp.zeros_like(lb_re)]
    for _ in range(n_pow - 1):
        r, i = qr[-1], qi[-1]
        qr.append(r * pw_re[chunk] - i * pw_im[chunk])
        qi.append(r * pw_im[chunk] + i * pw_re[chunk])
    lam_re = jnp.stack(lr + qr).reshape(n_scan + n_pow, G * N)
    lam_im = jnp.stack(li + qi).reshape(n_scan + n_pow, G * N)
    return m_intra, w_state, w_out, lam_re, lam_im


def _mixer_kernel(x_ref, nm_ref, win_ref, poolw_ref, pscale_ref, mi_ref, ws_ref, wh_ref,
                  lre_ref, lim_ref, gluw_ref, glub_ref, brp_ref, brs_ref, wout_ref,
                  o_ref, zext_ref, v_ref, y_ref, carry_ref, *, tiles_per_seq, chunk):
    tm, d_model = x_ref.shape
    pool_w = zext_ref.shape[1]
    n_oct = v_ref.shape[0]
    ssm_w = n_oct * LANES
    n_chunks = tm // chunk
    n_str = SUBLANES
    per = n_chunks // n_str
    span = per * chunk
    pitch = v_ref.shape[1] // n_str
    n_scan = lre_ref.shape[0] - per
    seq_tile = pl.program_id(0) % tiles_per_seq

    @pl.when(seq_tile == 0)
    def _():
        zext_ref[0:POOL_HALO, :] = jnp.zeros((POOL_HALO, pool_w), F32)
        carry_ref[...] = jnp.zeros_like(carry_ref)

    x = x_ref[...]
    u = _rms(x, nm_ref[...]).astype(BF16)
    proj = jnp.dot(u, win_ref[...], preferred_element_type=F32)
    z = proj[:, :pool_w]
    zext_ref[POOL_HALO:, :] = z
    for q in range(n_oct):
        for r in range(n_str):
            v_ref[q, r * pitch:r * pitch + span, :] = (
                proj[r * span:(r + 1) * span, pool_w + q * LANES:pool_w + (q + 1) * LANES])
    g_pool = jax.nn.sigmoid(proj[:, pool_w + ssm_w:pool_w + ssm_w + d_model])
    g_ssm = jax.nn.sigmoid(proj[:, pool_w + ssm_w + d_model:])

    pos = (seq_tile * tm + lax.broadcasted_iota(jnp.int32, (tm, 1), 0) + 1).astype(F32)
    gdim = pool_w // len(POOL_WINDOWS)
    ypool = []
    for gi, w in enumerate(POOL_WINDOWS):
        cols = slice(gi * gdim, (gi + 1) * gdim)
        acc = z[:, cols]
        for dlt in range(1, w):
            acc = acc + zext_ref[POOL_HALO - dlt:POOL_HALO - dlt + tm, cols]
        diff = acc / jnp.minimum(pos, float(w)) - z[:, cols]
        ypool.append(jnp.dot(diff.astype(BF16), poolw_ref[gi], preferred_element_type=F32))
    ypool = jnp.concatenate(ypool, axis=1) * pscale_ref[...]
    zext_ref[0:POOL_HALO, :] = zext_ref[tm:tm + POOL_HALO, :]
    p_br = jnp.dot(ypool.astype(BF16), brp_ref[...], preferred_element_type=F32)

    srow = lax.broadcasted_iota(jnp.int32, (n_str, 1), 0)
    half = OCT * SSM_STATE
    log_per = per.bit_length() - 1
    for q in range(n_oct):
        st = slice(q * half, (q + 1) * half)

        def lam(k):
            return lre_ref[k:k + 1, st], lim_ref[k:k + 1, st]

        vq = jnp.concatenate(
            [jnp.concatenate([v_ref[q, pl.ds(b * chunk + j, n_str, stride=pitch), :]
                              for j in range(chunk)], axis=1) for b in range(per)],
            axis=0).astype(BF16)
        s = jnp.dot(vq, ws_ref[q], preferred_element_type=F32)
        a_re, a_im = lam(0)
        loc_re, loc_im = [s[0:n_str, :half]], [s[0:n_str, half:]]
        for b in range(1, per):
            p_re, p_im = loc_re[-1], loc_im[-1]
            loc_re.append(s[b * n_str:(b + 1) * n_str, :half] + a_re * p_re - a_im * p_im)
            loc_im.append(s[b * n_str:(b + 1) * n_str, half:] + a_re * p_im + a_im * p_re)
        c_re, c_im = carry_ref[0:1, st], carry_ref[1:2, st]
        e_re, e_im = loc_re[-1], loc_im[-1]
        l_re, l_im = lam(log_per)
        e_re = e_re + jnp.where(srow == 0, l_re * c_re - l_im * c_im, 0.0)
        e_im = e_im + jnp.where(srow == 0, l_re * c_im + l_im * c_re, 0.0)
        for k in range(n_str.bit_length() - 1):
            sh = 1 << k
            r_re = jnp.where(srow >= sh, pltpu.roll(e_re, sh, axis=0), 0.0)
            r_im = jnp.where(srow >= sh, pltpu.roll(e_im, sh, axis=0), 0.0)
            l_re, l_im = lam(log_per + k)
            e_re, e_im = e_re + l_re * r_re - l_im * r_im, e_im + l_re * r_im + l_im * r_re
        carry_ref[0:1, st] = e_re[n_str - 1:n_str, :]
        carry_ref[1:2, st] = e_im[n_str - 1:n_str, :]
        i_re = jnp.where(srow == 0, c_re, pltpu.roll(e_re, 1, axis=0))
        i_im = jnp.where(srow == 0, c_im, pltpu.roll(e_im, 1, axis=0))
        h0_re, h0_im = [i_re], [i_im]
        for b in range(1, per):
            l_re, l_im = lam(n_scan + b)
            h0_re.append(loc_re[b - 1] + l_re * i_re - l_im * i_im)
            h0_im.append(loc_im[b - 1] + l_re * i_im + l_im * i_re)
        h0_re = jnp.concatenate(h0_re, axis=0).astype(BF16)
        h0_im = jnp.concatenate(h0_im, axis=0).astype(BF16)
        yq = (jnp.dot(vq, mi_ref[q], preferred_element_type=F32)
              + jnp.dot(h0_re, wh_ref[q, :half, :], preferred_element_type=F32)
              + jnp.dot(h0_im, wh_ref[q, half:, :], preferred_element_type=F32))
        for b in range(per):
            for j in range(chunk):
                y_ref[q, pl.ds(b * chunk + j, n_str, stride=pitch), :] = (
                    yq[b * n_str:(b + 1) * n_str, j * LANES:(j + 1) * LANES])

    y = jnp.concatenate(
        [jnp.concatenate([y_ref[q, r * pitch:r * pitch + span, :] for r in range(n_str)], axis=0)
         for q in range(n_oct)], axis=1)
    ys = jax.nn.gelu(y, approximate=True)
    glu = jnp.dot(ys.astype(BF16), gluw_ref[...], preferred_element_type=F32) + glub_ref[...]
    ys = ys * jax.nn.sigmoid(glu)
    s_br = jnp.dot(ys.astype(BF16), brs_ref[...], preferred_element_type=F32)

    merged = g_pool * p_br + g_ssm * s_br
    o_ref[...] = x + jnp.dot(merged.astype(BF16), wout_ref[...], preferred_element_type=F32)


def _mixer(x2, seq, nm, w_in, pool_w, pool_scale, tables, glu_w, glu_b, br_pool, br_ssm, w_out, tm):
    t, d = x2.shape
    mi, ws, wh, lre, lim = tables
    pool_width = pool_w.shape[0] * pool_w.shape[1]
    ssm_width = glu_w.shape[0]
    consts = [nm.reshape(1, d), w_in.astype(BF16), pool_w.astype(BF16),
              pool_scale.reshape(1, pool_width), mi, ws, wh, lre, lim,
              glu_w.astype(BF16), glu_b.reshape(1, ssm_width), br_pool.astype(BF16),
              br_ssm.astype(BF16), w_out.astype(BF16)]
    kern = functools.partial(_mixer_kernel, tiles_per_seq=seq // tm, chunk=SSM_CHUNK)
    return pl.pallas_call(
        kern,
        grid=(t // tm,),
        in_specs=[pl.BlockSpec((tm, d), lambda i: (i, 0))] + [_const_spec(c.shape) for c in consts],
        out_specs=pl.BlockSpec((tm, d), lambda i: (i, 0)),
        out_shape=jax.ShapeDtypeStruct((t, d), F32),
        scratch_shapes=[pltpu.VMEM((POOL_HALO + tm, pool_width), F32),
                        pltpu.VMEM((ssm_width // LANES, tm + SUBLANES * STRAND_PAD, LANES), F32),
                        pltpu.VMEM((ssm_width // LANES, tm + SUBLANES * STRAND_PAD, LANES), F32),
                        pltpu.VMEM((2, lre.shape[1]), F32)],
        compiler_params=pltpu.CompilerParams(dimension_semantics=("arbitrary",),
                                             vmem_limit_bytes=VMEM_LIMIT),
        name="mixer",
    )(x2, *consts)


def _kv_fold_kernel(mem_ref, g_ref, wkv_ref, wq_ref, wo_ref, wqk_ref, wvo_ref):
    m, d = mem_ref.shape
    hd = d // X_HEADS
    mn = _rms(mem_ref[...], g_ref[...]).astype(BF16)
    kv = jnp.dot(mn, wkv_ref[...], preferred_element_type=F32)
    for h in range(X_HEADS):
        k_h = kv[:, h * hd:(h + 1) * hd].astype(BF16)
        v_h = kv[:, d + h * hd:d + (h + 1) * hd].astype(BF16)
        qk = lax.dot_general(wq_ref[:, h * hd:(h + 1) * hd], k_h, (((1,), (1,)), ((), ())),
                             preferred_element_type=F32)
        wqk_ref[0, :, h * m:(h + 1) * m] = qk.astype(BF16)
        wvo_ref[0, h * m:(h + 1) * m, :] = jnp.dot(
            v_h, wo_ref[h * hd:(h + 1) * hd, :], preferred_element_type=F32).astype(BF16)


def _kv_fold(mem2, batch, g, w_kv, w_q, w_o):
    d = mem2.shape[1]
    m = mem2.shape[0] // batch
    hm = X_HEADS * m
    return pl.pallas_call(
        _kv_fold_kernel,
        grid=(batch,),
        in_specs=[pl.BlockSpec((m, d), lambda b: (b, 0)), _const_spec((1, d)),
                  _const_spec(w_kv.shape), _const_spec(w_q.shape), _const_spec(w_o.shape)],
        out_specs=[pl.BlockSpec((1, d, hm), lambda b: (b, 0, 0)),
                   pl.BlockSpec((1, hm, d), lambda b: (b, 0, 0))],
        out_shape=[jax.ShapeDtypeStruct((batch, d, hm), BF16),
                   jax.ShapeDtypeStruct((batch, hm, d), BF16)],
        compiler_params=pltpu.CompilerParams(dimension_semantics=("arbitrary",),
                                             vmem_limit_bytes=VMEM_LIMIT),
        name="kv_fold",
    )(mem2, g.reshape(1, d), w_kv.astype(BF16), w_q.astype(BF16), w_o.astype(BF16))


def _route(logits):
    n_g, n_eg = N_EXPERT_GROUPS, EXPERTS_PER_GROUP
    tm = logits.shape[0]
    lt = logits.T
    neg = jnp.float32(-jnp.inf)
    big = jnp.int32(1 << 20)

    def first_max(v):
        row = lax.broadcasted_iota(jnp.int32, v.shape, 0)
        m = jnp.max(v, axis=0, keepdims=True)
        return m, jnp.min(jnp.where(v == m, row, big), axis=0, keepdims=True)

    gl = lt[0:n_g]
    gmax, g_idx = first_max(gl)
    g_gate = 1.0 / jnp.sum(jnp.exp(gl - gmax), axis=0, keepdims=True)
    el = lt[n_g:n_g + n_eg]
    for g in range(1, n_g):
        el = jnp.where(g_idx == g, lt[n_g + g * n_eg:n_g + (g + 1) * n_eg], el)
    row = lax.broadcasted_iota(jnp.int32, el.shape, 0)
    t1, i1 = first_max(el)
    t2, i2 = first_max(jnp.where(row == i1, neg, el))
    e2 = jnp.exp(t2 - t1)
    w1 = 1.0 / (1.0 + e2)
    w2 = e2 / (1.0 + e2)
    comb = jnp.where(row == i1, w1, jnp.where(row == i2, w2, 0.0)) * g_gate
    lo, hi = jnp.minimum(i1, i2), jnp.maximum(i1, i2)
    pair = jnp.zeros_like(lo)
    for idx, (e_lo, e_hi) in enumerate(EXPERT_PAIRS):
        pair = jnp.where((lo == e_lo) & (hi == e_hi), idx, pair)
    cls = g_idx * len(EXPERT_PAIRS) + pair
    info = jnp.concatenate([comb, jnp.zeros((LANES - n_eg, tm), F32)], axis=0)
    hist = jnp.sum((cls == lax.broadcasted_iota(jnp.int32, (N_CLASS_ROWS, tm), 0)).astype(F32),
                   axis=1, keepdims=True)
    return info.T, cls, hist


def _dot_bf16x3(x, w_hilo):
    n = w_hilo.shape[1] // 2
    x_hi = x.astype(BF16)
    x_lo = (x - x_hi.astype(F32)).astype(BF16)
    both = jnp.dot(x_hi, w_hilo, preferred_element_type=F32)
    return both[:, :n] + both[:, n:] + jnp.dot(x_lo, w_hilo[:, :n], preferred_element_type=F32)


def _xattn_kernel(h_ref, g_ref, wqk_ref, wvo_ref, gffn_ref, wr_ref, br_ref, w1f_ref, w3f_ref, w2f_ref,
                  o_ref, cls_ref, hist_ref, w1b_ref, w3b_ref, w2b_ref, *, scale):
    w1b_ref[...] = w1f_ref[...].astype(BF16)
    w3b_ref[...] = w3f_ref[...].astype(BF16)
    w2b_ref[...] = w2f_ref[...].astype(BF16)

    h = h_ref[...]
    d = h.shape[1]
    m = wqk_ref.shape[2] // X_HEADS
    hn = _rms(h, g_ref[...]).astype(BF16)
    s_all = jnp.dot(hn, wqk_ref[0], preferred_element_type=F32) * scale
    probs = []
    for hh in range(X_HEADS):
        s = s_all[:, hh * m:(hh + 1) * m]
        s = s - jnp.max(s, axis=-1, keepdims=True)
        p = jnp.exp(s)
        probs.append((p / jnp.sum(p, axis=-1, keepdims=True)).astype(BF16))
    h_out = h + jnp.dot(jnp.concatenate(probs, axis=1), wvo_ref[0], preferred_element_type=F32)
    zn = _rms(h_out, gffn_ref[...])
    logits = _dot_bf16x3(zn, wr_ref[...]) + br_ref[...]
    route, cls, hist = _route(logits)
    o_ref[:, :d] = h_out
    o_ref[:, d:] = route
    cls_ref[0] = cls
    hist_ref[0] = jnp.broadcast_to(hist, hist_ref.shape[1:])


def _xattn(h2, seq, g, wqk, wvo, g_ffn, w_r, b_r, w1, w3, w2, tm):
    t, d = h2.shape
    tiles_per_seq = seq // tm
    n_tiles = t // tm
    n_e = w1.shape[0]
    assert n_tiles >= 2 * n_e, "the expert-weight cast needs two grid steps per expert"
    kern = functools.partial(_xattn_kernel, scale=(d // X_HEADS) ** -0.5)

    def half_expert(i):
        j = jnp.minimum(i, 2 * n_e - 1)
        return (j // 2, j % 2, 0)

    def half_block(w):
        return pl.BlockSpec((1, w.shape[1] // 2, w.shape[2]), half_expert)
    w_r_hi = w_r.astype(BF16)
    w_r_lo = (w_r - w_r_hi.astype(F32)).astype(BF16)
    w_r_hilo = jnp.concatenate([w_r_hi, w_r_lo], axis=1)
    hx, cls, hist, w1b, w3b, w2b = pl.pallas_call(
        kern,
        grid=(n_tiles,),
        in_specs=[pl.BlockSpec((tm, d), lambda i: (i, 0)), _const_spec((1, d)),
                  pl.BlockSpec((1,) + wqk.shape[1:], lambda i: (i // tiles_per_seq, 0, 0)),
                  pl.BlockSpec((1,) + wvo.shape[1:], lambda i: (i // tiles_per_seq, 0, 0)),
                  _const_spec((1, d)), _const_spec(w_r_hilo.shape), _const_spec(b_r.shape),
                  half_block(w1), half_block(w3), half_block(w2)],
        out_specs=[pl.BlockSpec((tm, d + LANES), lambda i: (i, 0)),
                   pl.BlockSpec((1, 1, tm), lambda i: (i, 0, 0)),
                   pl.BlockSpec((1, N_CLASS_ROWS, LANES), lambda i: (i, 0, 0)),
                   half_block(w1), half_block(w3), half_block(w2)],
        out_shape=[jax.ShapeDtypeStruct((t, d + LANES), F32),
                   jax.ShapeDtypeStruct((n_tiles, 1, tm), jnp.int32),
                   jax.ShapeDtypeStruct((n_tiles, N_CLASS_ROWS, LANES), F32),
                   jax.ShapeDtypeStruct(w1.shape, BF16), jax.ShapeDtypeStruct(w3.shape, BF16),
                   jax.ShapeDtypeStruct(w2.shape, BF16)],
        compiler_params=pltpu.CompilerParams(dimension_semantics=("arbitrary",),
                                             vmem_limit_bytes=VMEM_LIMIT),
        name="xattn",
    )(h2, g.reshape(1, d), wqk, wvo, g_ffn.reshape(1, d), w_r_hilo, b_r, w1, w3, w2)
    return hx, cls.reshape(t), jnp.sum(hist[:, :, 0], axis=0).astype(jnp.int32), (w1b, w3b, w2b)


def _dispatch_tables(cls, counts, tm):
    t = cls.shape[0]
    n_g, n_eg, n_pair = N_EXPERT_GROUPS, EXPERTS_PER_GROUP, len(EXPERT_PAIRS)
    n_tiles = t // tm + n_g
    i32 = jnp.int32
    _, tok_sorted = lax.sort((cls, jnp.arange(t, dtype=i32)), num_keys=1, is_stable=True)
    tok_sorted = jnp.concatenate([tok_sorted, jnp.zeros((tm,), i32)])
    counts = counts[:n_g * n_pair]
    c_start = jnp.cumsum(counts) - counts
    g_count = counts.reshape(n_g, n_pair).sum(axis=1)
    g_start = jnp.cumsum(g_count) - g_count
    tiles_g = (g_count + tm - 1) // tm
    tile_end = jnp.cumsum(tiles_g)
    tile_first = tile_end - tiles_g
    tile_ids = jnp.arange(n_tiles, dtype=i32)
    tile_group = jnp.minimum(jnp.sum((tile_ids[:, None] >= tile_end[None, :]).astype(i32), axis=1),
                             n_g - 1)
    k_in_group = tile_ids - tile_first[tile_group]
    tile_valid = jnp.clip(g_count[tile_group] - k_in_group * tm, 0, tm)
    tile_valid = jnp.where(tile_ids < tile_end[-1], tile_valid, 0)
    tile_start = jnp.where(tile_valid > 0, g_start[tile_group] + k_in_group * tm, 0)
    lo, hi = tile_start[:, None], (tile_start + tile_valid)[:, None]
    inter = (counts[None, :] > 0) & (c_start[None, :] < hi) & ((c_start + counts)[None, :] > lo)
    inter = inter.reshape(n_tiles, n_g, n_pair).any(axis=1)
    pair_has = jnp.array([[e in p for e in range(n_eg)] for p in EXPERT_PAIRS])
    need = (inter[:, :, None] & pair_has[None]).any(axis=1)
    tile_nexp = need.sum(axis=1)
    tile_exp = jnp.argsort(jnp.logical_not(need), axis=1, stable=True)
    return (tok_sorted, tile_start.astype(i32), tile_group.astype(i32), tile_valid.astype(i32),
            tile_nexp.astype(i32), tile_exp.reshape(-1).astype(i32))


def _moe_kernel(tok_ref, ts_ref, tg_ref, tv_ref, tn_ref, te_ref, hx_hbm, g_ref, w1_ref, w3_ref, w2_ref,
                gf_ref, out_hbm, xnext, osend, xcur, zn_ref, acc0_ref, acc_ref, gsem, ssem):
    i = pl.program_id(0)
    n_blk, rows = xnext.shape[0] - 1, xnext.shape[1]
    tm = n_blk * rows
    d = osend.shape[2]
    n_eg = w1_ref.shape[0]

    def row_in(tok, blk, k):
        return pltpu.make_async_copy(hx_hbm.at[pl.ds(tok, 1), :], xnext.at[blk, pl.ds(k, 1), :], gsem)

    def row_out(tok, blk, k):
        return pltpu.make_async_copy(osend.at[blk, pl.ds(k, 1), :], out_hbm.at[pl.ds(tok, 1), :], ssem)

    def tile_in_wait():
        pltpu.make_async_copy(xnext.at[pl.ds(0, n_blk)], xnext.at[pl.ds(0, n_blk)], gsem).wait()

    def tile_out_wait():
        pltpu.make_async_copy(osend.at[pl.ds(0, n_blk)], osend.at[pl.ds(0, n_blk)], ssem).wait()

    def anchor(width):
        zero = xnext[n_blk, 0:1, 0:width]
        osend[n_blk, 0:1, 0:width] = zero
        return zero + osend[n_blk, 0:1, 0:width]

    def expert(zn, k, move_rows=None):
        e = te_ref[i * n_eg + k]
        if move_rows is not None:
            move_rows(0)
        a = jnp.dot(zn, w1_ref[e], preferred_element_type=F32)
        if move_rows is not None:
            a = a + anchor(a.shape[1])
            move_rows(1)
        b = jnp.dot(zn, w3_ref[e], preferred_element_type=F32)
        hid = (a * jax.nn.sigmoid(a) * b).astype(BF16)
        route = xcur[:, d:]
        lane = lax.broadcasted_iota(jnp.int32, route.shape, 1)
        comb = jnp.sum(jnp.where(lane == e, route, 0.0), axis=1, keepdims=True)
        out = comb * jnp.dot(hid, w2_ref[e], preferred_element_type=F32)
        if move_rows is not None:
            out = out + anchor(d)
        return out

    busy = tv_ref[i] > 0
    after_busy = tv_ref[jnp.maximum(i - 1, 0)] > 0
    drain = jnp.logical_and(jnp.logical_not(busy), jnp.logical_and(i >= 1, after_busy))

    @pl.when(i == 0)
    def _():
        xnext[n_blk] = jnp.zeros(xnext.shape[1:], F32)
        osend[...] = jnp.zeros(osend.shape, F32)
        base = ts_ref[0]

        def body(blk, c):
            for k in range(rows):
                row_in(tok_ref[base + blk * rows + k], blk, k).start()
            return c
        lax.fori_loop(0, n_blk, body, 0)

    @pl.when(jnp.logical_or(busy, drain))
    def _():
        tile_in_wait()
        xcur[...] = xnext[0:n_blk].reshape(tm, xnext.shape[2])

    def hand_over_results():
        @pl.when(i >= 1)
        def _():
            tile_out_wait()
            osend[0:n_blk] = _rms(acc_ref[...], gf_ref[...]).reshape(n_blk, rows, d)

    n_always = 2

    @pl.when(busy)
    def _():
        h = xcur[:, :d]
        zn = _rms(h, g_ref[...]).astype(BF16)
        zn_ref[...] = zn
        nxt = ts_ref[i + 1]

        def gather_rows(part):
            for r in range(part * tm // 2, (part + 1) * tm // 2):
                row_in(tok_ref[nxt + r], r // rows, r % rows).start()
        acc0_ref[...] = h + expert(zn, 0, gather_rows)

    @pl.when(busy)
    def _():
        hand_over_results()

    @pl.when(busy)
    def _():
        prv = ts_ref[jnp.maximum(i - 1, 0)]

        def scatter_rows(part):
            for r in range(part * tm // 2, (part + 1) * tm // 2):
                row_out(tok_ref[prv + r], r // rows, r % rows).start()
        acc_ref[...] = acc0_ref[...] + expert(zn_ref[...], 1, scatter_rows)

    for k in range(n_always, n_eg):
        @pl.when(tn_ref[i] > k)
        def _():
            acc_ref[...] += expert(zn_ref[...], k)

    @pl.when(drain)
    def _():
        hand_over_results()
        base = ts_ref[i - 1]
        nv = tv_ref[i - 1]
        full = nv // rows

        def body(blk, c):
            for k in range(rows):
                row_out(tok_ref[base + blk * rows + k], blk, k).start()
            return c
        lax.fori_loop(0, full, body, 0)
        for k in range(rows - 1):

            @pl.when(full * rows + k < nv)
            def _():
                row_out(tok_ref[base + full * rows + k], full, k).start()

        @pl.when(full > 0)
        def _():
            pltpu.make_async_copy(osend.at[pl.ds(0, full)], osend.at[pl.ds(0, full)], ssem).wait()
        for k in range(rows - 1):

            @pl.when(full * rows + k < nv)
            def _():
                row_out(0, 0, 0).wait()


def _moe(hx, cls, counts, g, w1, w3, w2, g_final, tm):
    t, dx = hx.shape
    d = dx - LANES
    n_eg = EXPERTS_PER_GROUP
    de = w1.shape[2]
    tables = _dispatch_tables(cls, counts, tm)
    n_tiles = tables[2].shape[0]

    def wmap(i, tok, ts, tg, tv, tn, te):
        return (tg[i], 0, 0)

    def cmap(i, tok, ts, tg, tv, tn, te):
        return (0, 0)

    grid_spec = pltpu.PrefetchScalarGridSpec(
        num_scalar_prefetch=len(tables),
        grid=(n_tiles,),
        in_specs=[pl.BlockSpec(memory_space=pl.ANY),
                  pl.BlockSpec((1, d), cmap),
                  pl.BlockSpec((n_eg, d, de), wmap),
                  pl.BlockSpec((n_eg, d, de), wmap),
                  pl.BlockSpec((n_eg, de, d), wmap),
                  pl.BlockSpec((1, d), cmap)],
        out_specs=pl.BlockSpec(memory_space=pl.ANY),
        scratch_shapes=[pltpu.VMEM((tm // SUBLANES + 1, SUBLANES, dx), F32),
                        pltpu.VMEM((tm // SUBLANES + 1, SUBLANES, d), F32),
                        pltpu.VMEM((tm, dx), F32),
                        pltpu.VMEM((tm, d), BF16),
                        pltpu.VMEM((tm, d), F32),
                        pltpu.VMEM((tm, d), F32),
                        pltpu.SemaphoreType.DMA(()), pltpu.SemaphoreType.DMA(())])
    return pl.pallas_call(
        _moe_kernel,
        grid_spec=grid_spec,
        out_shape=jax.ShapeDtypeStruct((t, d), F32),
        compiler_params=pltpu.CompilerParams(dimension_semantics=("arbitrary",),
                                             vmem_limit_bytes=VMEM_LIMIT),
        name="moe",
    )(*tables, hx, g.reshape(1, d), w1, w3, w2,
      g_final.reshape(1, d))


def _layer(h2, mem2, batch, seq, p, l, g_final, tm):
    d = h2.shape[1]
    tm_mix = min(MIXER_TILE, seq)
    n_chunks = tm_mix // SSM_CHUNK
    n_scan = int(math.log2(n_chunks))
    tables = _ssm_tables(p["ssm_a_re"][l], p["ssm_a_im"][l], p["ssm_log_dt"][l], p["ssm_b_re"][l],
                         p["ssm_b_im"][l], p["ssm_c_re"][l], p["ssm_c_im"][l], p["ssm_d"][l],
                         SSM_CHUNK, n_scan, n_chunks // SUBLANES)
    h2 = _mixer(h2, seq, p["norm_mix"][l], p["w_in"][l], p["pool_w"][l], p["pool_scale"][l], tables,
                p["glu_w"][l], p["glu_b"][l], p["br_pool"][l], p["br_ssm"][l], p["w_out"][l],
                tm_mix)

    wqk, wvo = _kv_fold(mem2, batch, p["norm_mem"][l], p["x_wkv"][l], p["x_wq"][l], p["x_wo"][l])

    n_g, n_eg = N_EXPERT_GROUPS, EXPERTS_PER_GROUP
    w_r = jnp.concatenate([p["router_g_w"][l], p["router_e_w"][l].reshape(d, n_g * n_eg)], axis=1)
    w_r = jnp.pad(w_r, ((0, 0), (0, LANES - w_r.shape[1])))
    b_r = jnp.concatenate([p["router_g_b"][l], p["router_e_b"][l].reshape(n_g * n_eg)])
    b_r = jnp.pad(b_r, (0, LANES - b_r.shape[0])).reshape(1, LANES)
    hx, cls, counts, (w1b, w3b, w2b) = _xattn(
        h2, seq, p["norm_x"][l], wqk, wvo, p["norm_ffn"][l], w_r, b_r,
        p["exp_w1"][l], p["exp_w3"][l], p["exp_w2"][l], min(XATTN_TILE, seq))
    return _moe(hx, cls, counts, p["norm_ffn"][l], w1b, w3b, w2b, g_final, tm)


def kernel(x, mem, norm_mix, w_in, pool_w, pool_scale, ssm_a_re, ssm_a_im, ssm_log_dt, ssm_b_re,
           ssm_b_im, ssm_c_re, ssm_c_im, ssm_d, glu_w, glu_b, br_pool, br_ssm, w_out, norm_x,
           norm_mem, x_wq, x_wkv, x_wo, norm_ffn, router_g_w, router_g_b, router_e_w, router_e_b,
           exp_w1, exp_w3, exp_w2, norm_final):
    p = dict(norm_mix=norm_mix, w_in=w_in, pool_w=pool_w, pool_scale=pool_scale, ssm_a_re=ssm_a_re,
             ssm_a_im=ssm_a_im, ssm_log_dt=ssm_log_dt, ssm_b_re=ssm_b_re, ssm_b_im=ssm_b_im,
             ssm_c_re=ssm_c_re, ssm_c_im=ssm_c_im, ssm_d=ssm_d, glu_w=glu_w, glu_b=glu_b,
             br_pool=br_pool, br_ssm=br_ssm, w_out=w_out, norm_x=norm_x, norm_mem=norm_mem,
             x_wq=x_wq, x_wkv=x_wkv, x_wo=x_wo, norm_ffn=norm_ffn, router_g_w=router_g_w,
             router_g_b=router_g_b, router_e_w=router_e_w, router_e_b=router_e_b, exp_w1=exp_w1,
             exp_w3=exp_w3, exp_w2=exp_w2)
    batch, seq, d = x.shape
    assert norm_mix.shape[0] == 1, "final norm is fused into the (single) layer's last stage"
    tm = min(MOE_TILE, seq)
    assert seq % min(MIXER_TILE, seq) == 0 and seq % min(XATTN_TILE, seq) == 0 and (batch * seq) % tm == 0
    h2 = x.reshape(batch * seq, d)
    mem2 = mem.reshape(-1, d)
    out = _layer(h2, mem2, batch, seq, p, 0, norm_final, tm)
    return out.reshape(batch, seq, d)
```

```python
import functools
import math

import jax
import jax.numpy as jnp
from jax import lax
from jax.experimental import pallas as pl
from jax.experimental.pallas import tpu as pltpu

F32 = jnp.float32
BF16 = jnp.bfloat16

RMS_EPS = 1e-6
POOL_WINDOWS = (2, 4, 8, 16)
POOL_HALO = 16
SSM_GROUP_DIM = 16
SSM_STATE = 64
LANES = 128
SUBLANES = 8
OCT = LANES // SSM_GROUP_DIM
SSM_CHUNK = 4
STRAND_PAD = 8
X_HEADS = 4
MIXER_TILE = 1024
XATTN_TILE = 1024
MOE_TILE = 512
N_EXPERT_GROUPS = 4
EXPERTS_PER_GROUP = 4
EXPERT_PAIRS = ((0, 1), (0, 2), (0, 3), (1, 3), (1, 2), (2, 3))
N_CLASS_ROWS = 32
VMEM_LIMIT = 56 * 1024 * 1024


def _rms(x, g):
    return x * lax.rsqrt(jnp.mean(x * x, axis=-1, keepdims=True) + RMS_EPS) * g


def _const_spec(shape):
    nd = len(shape)
    return pl.BlockSpec(shape, lambda *_: (0,) * nd, pipeline_mode=pl.Buffered(1))


def _ssm_tables(a_re, a_im, log_dt, b_re, b_im, c_re, c_im, d_skip, chunk, n_scan, n_pow):
    hi = lax.Precision.HIGHEST
    G, N = a_re.shape
    C = SSM_GROUP_DIM
    Q = G // OCT
    dt = jnp.exp(log_dt)[:, None]
    mag = jnp.exp(a_re * dt)
    lb_re, lb_im = mag * jnp.cos(a_im * dt), mag * jnp.sin(a_im * dt)
    den = a_re * a_re + a_im * a_im
    nr, ni = lb_re - 1.0, lb_im
    f_re = ((nr * a_re + ni * a_im) / den)[..., None]
    f_im = ((ni * a_re - nr * a_im) / den)[..., None]
    bb_re = f_re * b_re - f_im * b_im
    bb_im = f_re * b_im + f_im * b_re

    pw_re, pw_im = [jnp.ones_like(lb_re)], [jnp.zeros_like(lb_re)]
    for _ in range(chunk):
        r, i = pw_re[-1], pw_im[-1]
        pw_re.append(r * lb_re - i * lb_im)
        pw_im.append(r * lb_im + i * lb_re)
    pw_re, pw_im = jnp.stack(pw_re), jnp.stack(pw_im)

    lbb_re = pw_re[:chunk, :, :, None] * bb_re - pw_im[:chunk, :, :, None] * bb_im
    lbb_im = pw_re[:chunk, :, :, None] * bb_im + pw_im[:chunk, :, :, None] * bb_re
    kern = (jnp.einsum('gon,dgnc->dgoc', c_re, lbb_re, precision=hi)
            - jnp.einsum('gon,dgnc->dgoc', c_im, lbb_im, precision=hi))
    kern = kern.at[0].add(d_skip.reshape(G, C)[:, :, None] * jnp.eye(C, dtype=F32))

    lane_g = jnp.arange(LANES) // C
    state_g = jnp.arange(OCT * N) // N
    rep_c = (jnp.arange(C)[:, None] == (jnp.arange(LANES) % C)[None, :]).astype(F32)
    rep_n = (jnp.arange(N)[:, None] == (jnp.arange(OCT * N) % N)[None, :]).astype(F32)
    same_cc = (lane_g[:, None] == lane_g[None, :]).astype(F32)
    same_cn = (lane_g[:, None] == state_g[None, :]).astype(F32)

    kt = kern.transpose(0, 1, 3, 2).reshape(chunk, Q, LANES, C)
    blk = (jnp.einsum('dqrc,cl->dqrl', kt, rep_c, precision=hi) * same_cc).astype(BF16)
    zero = jnp.zeros((Q, LANES, LANES), BF16)
    m_intra = jnp.concatenate(
        [jnp.concatenate([blk[jp - j] if jp >= j else zero for jp in range(chunk)], axis=2)
         for j in range(chunk)], axis=1)

    def state_half(lbb):
        z = lbb[::-1].transpose(0, 1, 3, 2).reshape(chunk, Q, LANES, N)
        w = jnp.einsum('jqrn,nl->qjrl', z, rep_n, precision=hi) * same_cn
        return w.reshape(Q, chunk * LANES, OCT * N).astype(BF16)
    w_state = jnp.concatenate([state_half(lbb_re), state_half(lbb_im)], axis=-1)

    p_re = c_re[None] * pw_re[1:, :, None, :] - c_im[None] * pw_im[1:, :, None, :]
    p_im = c_re[None] * pw_im[1:, :, None, :] + c_im[None] * pw_re[1:, :, None, :]
    def out_half(p):
        z = p.transpose(0, 1, 3, 2).reshape(chunk, Q, OCT * N, C)
        w = jnp.einsum('jqrc,cl->qrjl', z, rep_c, precision=hi)
        w = w * same_cn.T[None, :, None, :]
        return w.reshape(Q, OCT * N, chunk * LANES).astype(BF16)
    w_out = jnp.concatenate([out_half(p_re), out_half(-p_im)], axis=1)

    lr, li = [pw_re[chunk]], [pw_im[chunk]]
    for _ in range(n_scan - 1):
        r, i = lr[-1], li[-1]
        lr.append(r * r - i * i)
        li.append(2.0 * r * i)
    qr, qi = [jnp.ones_like(lb_re)], [jnp.zeros_like(lb_re)]
    for _ in range(n_pow - 1):
        r, i = qr[-1], qi[-1]
        qr.append(r * pw_re[chunk] - i * pw_im[chunk])
        qi.append(r * pw_im[chunk] + i * pw_re[chunk])
    lam_re = jnp.stack(lr + qr).reshape(n_scan + n_pow, G * N)
    lam_im = jnp.stack(li + qi).reshape(n_scan + n_pow, G * N)
    return m_intra, w_state, w_out, lam_re, lam_im


def _mixer_kernel(x_ref, nm_ref, win_ref, poolw_ref, pscale_ref, mi_ref, ws_ref, wh_ref,
                  lre_ref, lim_ref, gluw_ref, glub_ref, brp_ref, brs_ref, wout_ref,
                  o_ref, zext_ref, v_ref, y_ref, carry_ref, *, tiles_per_seq, chunk):
    tm, d_model = x_ref.shape
    pool_w = zext_ref.shape[1]
    n_oct = v_ref.shape[0]
    ssm_w = n_oct * LANES
    n_chunks = tm // chunk
    n_str = SUBLANES
    per = n_chunks // n_str
    span = per * chunk
    pitch = v_ref.shape[1] // n_str
    n_scan = lre_ref.shape[0] - per
    seq_tile = pl.program_id(0) % tiles_per_seq

    @pl.when(seq_tile == 0)
    def _():
        zext_ref[0:POOL_HALO, :] = jnp.zeros((POOL_HALO, pool_w), F32)
        carry_ref[...] = jnp.zeros_like(carry_ref)

    x = x_ref[...]
    u = _rms(x, nm_ref[...]).astype(BF16)
    proj = jnp.dot(u, win_ref[...], preferred_element_type=F32)
    z = proj[:, :pool_w]
    zext_ref[POOL_HALO:, :] = z
    for q in range(n_oct):
        for r in range(n_str):
            v_ref[q, r * pitch:r * pitch + span, :] = (
                proj[r * span:(r + 1) * span, pool_w + q * LANES:pool_w + (q + 1) * LANES])
    g_pool = jax.nn.sigmoid(proj[:, pool_w + ssm_w:pool_w + ssm_w + d_model])
    g_ssm = jax.nn.sigmoid(proj[:, pool_w + ssm_w + d_model:])

    pos = (seq_tile * tm + lax.broadcasted_iota(jnp.int32, (tm, 1), 0) + 1).astype(F32)
    gdim = pool_w // len(POOL_WINDOWS)
    ypool = []
    for gi, w in enumerate(POOL_WINDOWS):
        cols = slice(gi * gdim, (gi + 1) * gdim)
        acc = z[:, cols]
        for dlt in range(1, w):
            acc = acc + zext_ref[POOL_HALO - dlt:POOL_HALO - dlt + tm, cols]
        diff = acc / jnp.minimum(pos, float(w)) - z[:, cols]
        ypool.append(jnp.dot(diff.astype(BF16), poolw_ref[gi], preferred_element_type=F32))
    ypool = jnp.concatenate(ypool, axis=1) * pscale_ref[...]
    zext_ref[0:POOL_HALO, :] = zext_ref[tm:tm + POOL_HALO, :]
    p_br = jnp.dot(ypool.astype(BF16), brp_ref[...], preferred_element_type=F32)

    srow = lax.broadcasted_iota(jnp.int32, (n_str, 1), 0)
    half = OCT * SSM_STATE
    log_per = per.bit_length() - 1
    for q in range(n_oct):
        st = slice(q * half, (q + 1) * half)

        def lam(k):
            return lre_ref[k:k + 1, st], lim_ref[k:k + 1, st]

        vq = jnp.concatenate(
            [jnp.concatenate([v_ref[q, pl.ds(b * chunk + j, n_str, stride=pitch), :]
                              for j in range(chunk)], axis=1) for b in range(per)],
            axis=0).astype(BF16)
        s = jnp.dot(vq, ws_ref[q], preferred_element_type=F32)
        a_re, a_im = lam(0)
        loc_re, loc_im = [s[0:n_str, :half]], [s[0:n_str, half:]]
        for b in range(1, per):
            p_re, p_im = loc_re[-1], loc_im[-1]
            loc_re.append(s[b * n_str:(b + 1) * n_str, :half] + a_re * p_re - a_im * p_im)
            loc_im.append(s[b * n_str:(b + 1) * n_str, half:] + a_re * p_im + a_im * p_re)
        c_re, c_im = carry_ref[0:1, st], carry_ref[1:2, st]
        e_re, e_im = loc_re[-1], loc_im[-1]
        l_re, l_im = lam(log_per)
        e_re = e_re + jnp.where(srow == 0, l_re * c_re - l_im * c_im, 0.0)
        e_im = e_im + jnp.where(srow == 0, l_re * c_im + l_im * c_re, 0.0)
        for k in range(n_str.bit_length() - 1):
            sh = 1 << k
            r_re = jnp.where(srow >= sh, pltpu.roll(e_re, sh, axis=0), 0.0)
            r_im = jnp.where(srow >= sh, pltpu.roll(e_im, sh, axis=0), 0.0)
            l_re, l_im = lam(log_per + k)
            e_re, e_im = e_re + l_re * r_re - l_im * r_im, e_im + l_re * r_im + l_im * r_re
        carry_ref[0:1, st] = e_re[n_str - 1:n_str, :]
        carry_ref[1:2, st] = e_im[n_str - 1:n_str, :]
        i_re = jnp.where(srow == 0, c_re, pltpu.roll(e_re, 1, axis=0))
        i_im = jnp.where(srow == 0, c_im, pltpu.roll(e_im, 1, axis=0))
        h0_re, h0_im = [i_re], [i_im]
        for b in range(1, per):
            l_re, l_im = lam(n_scan + b)
            h0_re.append(loc_re[b - 1] + l_re * i_re - l_im * i_im)
            h0_im.append(loc_im[b - 1] + l_re * i_im + l_im * i_re)
        h0_re = jnp.concatenate(h0_re, axis=0).astype(BF16)
        h0_im = jnp.concatenate(h0_im, axis=0).astype(BF16)
        yq = (jnp.dot(vq, mi_ref[q], preferred_element_type=F32)
              + jnp.dot(h0_re, wh_ref[q, :half, :], preferred_element_type=F32)
              + jnp.dot(h0_im, wh_ref[q, half:, :], preferred_element_type=F32))
        for b in range(per):
            for j in range(chunk):
                y_ref[q, pl.ds(b * chunk + j, n_str, stride=pitch), :] = (
                    yq[b * n_str:(b + 1) * n_str, j * LANES:(j + 1) * LANES])

    y = jnp.concatenate(
        [jnp.concatenate([y_ref[q, r * pitch:r * pitch + span, :] for r in range(n_str)], axis=0)
         for q in range(n_oct)], axis=1)
    ys = jax.nn.gelu(y, approximate=True)
    glu = jnp.dot(ys.astype(BF16), gluw_ref[...], preferred_element_type=F32) + glub_ref[...]
    ys = ys * jax.nn.sigmoid(glu)
    s_br = jnp.dot(ys.astype(BF16), brs_ref[...], preferred_element_type=F32)

    merged = g_pool * p_br + g_ssm * s_br
    o_ref[...] = x + jnp.dot(merged.astype(BF16), wout_ref[...], preferred_element_type=F32)


def _mixer(x2, seq, nm, w_in, pool_w, pool_scale, tables, glu_w, glu_b, br_pool, br_ssm, w_out, tm):
    t, d = x2.shape
    mi, ws, wh, lre, lim = tables
    pool_width = pool_w.shape[0] * pool_w.shape[1]
    ssm_width = glu_w.shape[0]
    consts = [nm.reshape(1, d), w_in.astype(BF16), pool_w.astype(BF16),
              pool_scale.reshape(1, pool_width), mi, ws, wh, lre, lim,
              glu_w.astype(BF16), glu_b.reshape(1, ssm_width), br_pool.astype(BF16),
              br_ssm.astype(BF16), w_out.astype(BF16)]
    kern = functools.partial(_mixer_kernel, tiles_per_seq=seq // tm, chunk=SSM_CHUNK)
    return pl.pallas_call(
        kern,
        grid=(t // tm,),
        in_specs=[pl.BlockSpec((tm, d), lambda i: (i, 0))] + [_const_spec(c.shape) for c in consts],
        out_specs=pl.BlockSpec((tm, d), lambda i: (i, 0)),
        out_shape=jax.ShapeDtypeStruct((t, d), F32),
        scratch_shapes=[pltpu.VMEM((POOL_HALO + tm, pool_width), F32),
                        pltpu.VMEM((ssm_width // LANES, tm + SUBLANES * STRAND_PAD, LANES), F32),
                        pltpu.VMEM((ssm_width // LANES, tm + SUBLANES * STRAND_PAD, LANES), F32),
                        pltpu.VMEM((2, lre.shape[1]), F32)],
        compiler_params=pltpu.CompilerParams(dimension_semantics=("arbitrary",),
                                             vmem_limit_bytes=VMEM_LIMIT),
        name="mixer",
    )(x2, *consts)


def _kv_fold_kernel(mem_ref, g_ref, wkv_ref, wq_ref, wo_ref, wqk_ref, wvo_ref):
    m, d = mem_ref.shape
    hd = d // X_HEADS
    mn = _rms(mem_ref[...], g_ref[...]).astype(BF16)
    kv = jnp.dot(mn, wkv_ref[...], preferred_element_type=F32)
    for h in range(X_HEADS):
        k_h = kv[:, h * hd:(h + 1) * hd].astype(BF16)
        v_h = kv[:, d + h * hd:d + (h + 1) * hd].astype(BF16)
        qk = lax.dot_general(wq_ref[:, h * hd:(h + 1) * hd], k_h, (((1,), (1,)), ((), ())),
                             preferred_element_type=F32)
        wqk_ref[0, :, h * m:(h + 1) * m] = qk.astype(BF16)
        wvo_ref[0, h * m:(h + 1) * m, :] = jnp.dot(
            v_h, wo_ref[h * hd:(h + 1) * hd, :], preferred_element_type=F32).astype(BF16)


def _kv_fold(mem2, batch, g, w_kv, w_q, w_o):
    d = mem2.shape[1]
    m = mem2.shape[0] // batch
    hm = X_HEADS * m
    return pl.pallas_call(
        _kv_fold_kernel,
        grid=(batch,),
        in_specs=[pl.BlockSpec((m, d), lambda b: (b, 0)), _const_spec((1, d)),
                  _const_spec(w_kv.shape), _const_spec(w_q.shape), _const_spec(w_o.shape)],
        out_specs=[pl.BlockSpec((1, d, hm), lambda b: (b, 0, 0)),
                   pl.BlockSpec((1, hm, d), lambda b: (b, 0, 0))],
        out_shape=[jax.ShapeDtypeStruct((batch, d, hm), BF16),
                   jax.ShapeDtypeStruct((batch, hm, d), BF16)],
        compiler_params=pltpu.CompilerParams(dimension_semantics=("arbitrary",),
                                             vmem_limit_bytes=VMEM_LIMIT),
        name="kv_fold",
    )(mem2, g.reshape(1, d), w_kv.astype(BF16), w_q.astype(BF16), w_o.astype(BF16))


def _route(logits):
    n_g, n_eg = N_EXPERT_GROUPS, EXPERTS_PER_GROUP
    tm = logits.shape[0]
    lt = logits.T
    neg = jnp.float32(-jnp.inf)
    big = jnp.int32(1 << 20)

    def first_max(v):
        row = lax.broadcasted_iota(jnp.int32, v.shape, 0)
        m = jnp.max(v, axis=0, keepdims=True)
        return m, jnp.min(jnp.where(v == m, row, big), axis=0, keepdims=True)

    gl = lt[0:n_g]
    gmax, g_idx = first_max(gl)
    g_gate = 1.0 / jnp.sum(jnp.exp(gl - gmax), axis=0, keepdims=True)
    el = lt[n_g:n_g + n_eg]
    for g in range(1, n_g):
        el = jnp.where(g_idx == g, lt[n_g + g * n_eg:n_g + (g + 1) * n_eg], el)
    row = lax.broadcasted_iota(jnp.int32, el.shape, 0)
    t1, i1 = first_max(el)
    t2, i2 = first_max(jnp.where(row == i1, neg, el))
    e2 = jnp.exp(t2 - t1)
    w1 = 1.0 / (1.0 + e2)
    w2 = e2 / (1.0 + e2)
    comb = jnp.where(row == i1, w1, jnp.where(row == i2, w2, 0.0)) * g_gate
    lo, hi = jnp.minimum(i1, i2), jnp.maximum(i1, i2)
    pair = jnp.zeros_like(lo)
    for idx, (e_lo, e_hi) in enumerate(EXPERT_PAIRS):
        pair = jnp.where((lo == e_lo) & (hi == e_hi), idx, pair)
    cls = g_idx * len(EXPERT_PAIRS) + pair
    info = jnp.concatenate([comb, jnp.zeros((LANES - n_eg, tm), F32)], axis=0)
    hist = jnp.sum((cls == lax.broadcasted_iota(jnp.int32, (N_CLASS_ROWS, tm), 0)).astype(F32),
                   axis=1, keepdims=True)
    return info.T, cls, hist


def _dot_bf16x3(x, w_hilo):
    n = w_hilo.shape[1] // 2
    x_hi = x.astype(BF16)
    x_lo = (x - x_hi.astype(F32)).astype(BF16)
    both = jnp.dot(x_hi, w_hilo, preferred_element_type=F32)
    return both[:, :n] + both[:, n:] + jnp.dot(x_lo, w_hilo[:, :n], preferred_element_type=F32)


def _xattn_kernel(h_ref, g_ref, wqk_ref, wvo_ref, gffn_ref, wr_ref, br_ref, w1f_ref, w3f_ref, w2f_ref,
                  o_ref, cls_ref, hist_ref, w1b_ref, w3b_ref, w2b_ref, *, scale):
    w1b_ref[...] = w1f_ref[...].astype(BF16)
    w3b_ref[...] = w3f_ref[...].astype(BF16)
    w2b_ref[...] = w2f_ref[...].astype(BF16)

    h = h_ref[...]
    d = h.shape[1]
    m = wqk_ref.shape[2] // X_HEADS
    hn = _rms(h, g_ref[...]).astype(BF16)
    s_all = jnp.dot(hn, wqk_ref[0], preferred_element_type=F32) * scale
    probs = []
    for hh in range(X_HEADS):
        s = s_all[:, hh * m:(hh + 1) * m]
        s = s - jnp.max(s, axis=-1, keepdims=True)
        p = jnp.exp(s)
        probs.append((p / jnp.sum(p, axis=-1, keepdims=True)).astype(BF16))
    h_out = h + jnp.dot(jnp.concatenate(probs, axis=1), wvo_ref[0], preferred_element_type=F32)
    zn = _rms(h_out, gffn_ref[...])
    logits = _dot_bf16x3(zn, wr_ref[...]) + br_ref[...]
    route, cls, hist = _route(logits)
    o_ref[:, :d] = h_out
    o_ref[:, d:] = route
    cls_ref[0] = cls
    hist_ref[0] = jnp.broadcast_to(hist, hist_ref.shape[1:])


def _xattn(h2, seq, g, wqk, wvo, g_ffn, w_r, b_r, w1, w3, w2, tm):
    t, d = h2.shape
    tiles_per_seq = seq // tm
    n_tiles = t // tm
    n_e = w1.shape[0]
    assert n_tiles >= 2 * n_e, "the expert-weight cast needs two grid steps per expert"
    kern = functools.partial(_xattn_kernel, scale=(d // X_HEADS) ** -0.5)

    def half_expert(i):
        j = jnp.minimum(i, 2 * n_e - 1)
        return (j // 2, j % 2, 0)

    def half_block(w):
        return pl.BlockSpec((1, w.shape[1] // 2, w.shape[2]), half_expert)
    w_r_hi = w_r.astype(BF16)
    w_r_lo = (w_r - w_r_hi.astype(F32)).astype(BF16)
    w_r_hilo = jnp.concatenate([w_r_hi, w_r_lo], axis=1)
    hx, cls, hist, w1b, w3b, w2b = pl.pallas_call(
        kern,
        grid=(n_tiles,),
        in_specs=[pl.BlockSpec((tm, d), lambda i: (i, 0)), _const_spec((1, d)),
                  pl.BlockSpec((1,) + wqk.shape[1:], lambda i: (i // tiles_per_seq, 0, 0)),
                  pl.BlockSpec((1,) + wvo.shape[1:], lambda i: (i // tiles_per_seq, 0, 0)),
                  _const_spec((1, d)), _const_spec(w_r_hilo.shape), _const_spec(b_r.shape),
                  half_block(w1), half_block(w3), half_block(w2)],
        out_specs=[pl.BlockSpec((tm, d + LANES), lambda i: (i, 0)),
                   pl.BlockSpec((1, 1, tm), lambda i: (i, 0, 0)),
                   pl.BlockSpec((1, N_CLASS_ROWS, LANES), lambda i: (i, 0, 0)),
                   half_block(w1), half_block(w3), half_block(w2)],
        out_shape=[jax.ShapeDtypeStruct((t, d + LANES), F32),
                   jax.ShapeDtypeStruct((n_tiles, 1, tm), jnp.int32),
                   jax.ShapeDtypeStruct((n_tiles, N_CLASS_ROWS, LANES), F32),
                   jax.ShapeDtypeStruct(w1.shape, BF16), jax.ShapeDtypeStruct(w3.shape, BF16),
                   jax.ShapeDtypeStruct(w2.shape, BF16)],
        compiler_params=pltpu.CompilerParams(dimension_semantics=("arbitrary",),
                                             vmem_limit_bytes=VMEM_LIMIT),
        name="xattn",
    )(h2, g.reshape(1, d), wqk, wvo, g_ffn.reshape(1, d), w_r_hilo, b_r, w1, w3, w2)
    return hx, cls.reshape(t), jnp.sum(hist[:, :, 0], axis=0).astype(jnp.int32), (w1b, w3b, w2b)


def _dispatch_tables(cls, counts, tm):
    t = cls.shape[0]
    n_g, n_eg, n_pair = N_EXPERT_GROUPS, EXPERTS_PER_GROUP, len(EXPERT_PAIRS)
    n_tiles = t // tm + n_g
    i32 = jnp.int32
    _, tok_sorted = lax.sort((cls, jnp.arange(t, dtype=i32)), num_keys=1, is_stable=True)
    tok_sorted = jnp.concatenate([tok_sorted, jnp.zeros((tm,), i32)])
    counts = counts[:n_g * n_pair]
    c_start = jnp.cumsum(counts) - counts
    g_count = counts.reshape(n_g, n_pair).sum(axis=1)
    g_start = jnp.cumsum(g_count) - g_count
    tiles_g = (g_count + tm - 1) // tm
    tile_end = jnp.cumsum(tiles_g)
    tile_first = tile_end - tiles_g
    tile_ids = jnp.arange(n_tiles, dtype=i32)
    tile_group = jnp.minimum(jnp.sum((tile_ids[:, None] >= tile_end[None, :]).astype(i32), axis=1),
                             n_g - 1)
    k_in_group = tile_ids - tile_first[tile_group]
    tile_valid = jnp.clip(g_count[tile_group] - k_in_group * tm, 0, tm)
    tile_valid = jnp.where(tile_ids < tile_end[-1], tile_valid, 0)
    tile_start = jnp.where(tile_valid > 0, g_start[tile_group] + k_in_group * tm, 0)
    lo, hi = tile_start[:, None], (tile_start + tile_valid)[:, None]
    inter = (counts[None, :] > 0) & (c_start[None, :] < hi) & ((c_start + counts)[None, :] > lo)
    inter = inter.reshape(n_tiles, n_g, n_pair).any(axis=1)
    pair_has = jnp.array([[e in p for e in range(n_eg)] for p in EXPERT_PAIRS])
    need = (inter[:, :, None] & pair_has[None]).any(axis=1)
    tile_nexp = need.sum(axis=1)
    tile_exp = jnp.argsort(jnp.logical_not(need), axis=1, stable=True)
    return (tok_sorted, tile_start.astype(i32), tile_group.astype(i32), tile_valid.astype(i32),
            tile_nexp.astype(i32), tile_exp.reshape(-1).astype(i32))


def _moe_kernel(tok_ref, ts_ref, tg_ref, tv_ref, tn_ref, te_ref, hx_hbm, g_ref, w1_ref, w3_ref, w2_ref,
                gf_ref, out_hbm, xnext, osend, route_ref, zn_ref, acc0_ref, acc_ref, gsem, ssem):
    i = pl.program_id(0)
    n_blk, rows = xnext.shape[0] - 1, xnext.shape[1]
    tm = n_blk * rows
    d = osend.shape[2]
    n_eg = w1_ref.shape[0]

    def row_in(tok, blk, k):
        return pltpu.make_async_copy(hx_hbm.at[pl.ds(tok, 1), :], xnext.at[blk, pl.ds(k, 1), :], gsem)

    def row_out(tok, blk, k):
        return pltpu.make_async_copy(osend.at[blk, pl.ds(k, 1), :], out_hbm.at[pl.ds(tok, 1), :], ssem)

    def tile_in_wait():
        pltpu.make_async_copy(xnext.at[pl.ds(0, n_blk)], xnext.at[pl.ds(0, n_blk)], gsem).wait()

    def tile_out_wait():
        pltpu.make_async_copy(osend.at[pl.ds(0, n_blk)], osend.at[pl.ds(0, n_blk)], ssem).wait()

    def anchor(width):
        zero = xnext[n_blk, 0:1, 0:width]
        osend[n_blk, 0:1, 0:width] = zero
        return zero + osend[n_blk, 0:1, 0:width]

    def expert(zn, k, move_rows=None):
        e = te_ref[i * n_eg + k]
        if move_rows is not None:
            move_rows(0)
        a = jnp.dot(zn, w1_ref[e], preferred_element_type=F32)
        if move_rows is not None:
            a = a + anchor(a.shape[1])
            move_rows(1)
        b = jnp.dot(zn, w3_ref[e], preferred_element_type=F32)
        hid = (a * jax.nn.sigmoid(a) * b).astype(BF16)
        route = route_ref[...]
        lane = lax.broadcasted_iota(jnp.int32, route.shape, 1)
        comb = jnp.sum(jnp.where(lane == e, route, 0.0), axis=1, keepdims=True)
        out = comb * jnp.dot(hid, w2_ref[e], preferred_element_type=F32)
        if move_rows is not None:
            out = out + anchor(d)
        return out

    busy = tv_ref[i] > 0
    after_busy = tv_ref[jnp.maximum(i - 1, 0)] > 0
    drain = jnp.logical_and(jnp.logical_not(busy), jnp.logical_and(i >= 1, after_busy))

    @pl.when(i == 0)
    def _():
        xnext[n_blk] = jnp.zeros(xnext.shape[1:], F32)
        osend[...] = jnp.zeros(osend.shape, F32)
        base = ts_ref[0]

        def body(blk, c):
            for k in range(rows):
                row_in(tok_ref[base + blk * rows + k], blk, k).start()
            return c
        lax.fori_loop(0, n_blk, body, 0)

    @pl.when(jnp.logical_or(busy, drain))
    def _():
        tile_in_wait()

    def hand_over_results():
        @pl.when(i >= 1)
        def _():
            tile_out_wait()
            osend[0:n_blk] = _rms(acc_ref[...], gf_ref[...]).reshape(n_blk, rows, d)

    n_always = 2

    @pl.when(busy)
    def _():
        hand_over_results()

    @pl.when(busy)
    def _():
        x = xnext[0:n_blk].reshape(tm, xnext.shape[2])
        h = x[:, :d]
        route_ref[...] = x[:, d:]
        zn = _rms(h, g_ref[...]).astype(BF16)
        zn_ref[...] = zn
        prv = ts_ref[jnp.maximum(i - 1, 0)]

        def scatter_rows(part):
            for r in range(part * tm // 2, (part + 1) * tm // 2):
                row_out(tok_ref[prv + r], r // rows, r % rows).start()
        acc0_ref[...] = h + expert(zn, 0, scatter_rows)

    @pl.when(busy)
    def _():
        nxt = ts_ref[i + 1]

        def gather_rows(part):
            for r in range(part * tm // 2, (part + 1) * tm // 2):
                row_in(tok_ref[nxt + r], r // rows, r % rows).start()
        acc_ref[...] = acc0_ref[...] + expert(zn_ref[...], 1, gather_rows)

    for k in range(n_always, n_eg):
        @pl.when(tn_ref[i] > k)
        def _():
            acc_ref[...] += expert(zn_ref[...], k)

    @pl.when(drain)
    def _():
        hand_over_results()
        base = ts_ref[i - 1]
        nv = tv_ref[i - 1]
        full = nv // rows

        def body(blk, c):
            for k in range(rows):
                row_out(tok_ref[base + blk * rows + k], blk, k).start()
            return c
        lax.fori_loop(0, full, body, 0)
        for k in range(rows - 1):

            @pl.when(full * rows + k < nv)
            def _():
                row_out(tok_ref[base + full * rows + k], full, k).start()

        @pl.when(full > 0)
        def _():
            pltpu.make_async_copy(osend.at[pl.ds(0, full)], osend.at[pl.ds(0, full)], ssem).wait()
        for k in range(rows - 1):

            @pl.when(full * rows + k < nv)
            def _():
                row_out(0, 0, 0).wait()


def _moe(hx, cls, counts, g, w1, w3, w2, g_final, tm):
    t, dx = hx.shape
    d = dx - LANES
    n_eg = EXPERTS_PER_GROUP
    de = w1.shape[2]
    tables = _dispatch_tables(cls, counts, tm)
    n_tiles = tables[2].shape[0]

    def wmap(i, tok, ts, tg, tv, tn, te):
        return (tg[i], 0, 0)

    def cmap(i, tok, ts, tg, tv, tn, te):
        return (0, 0)

    grid_spec = pltpu.PrefetchScalarGridSpec(
        num_scalar_prefetch=len(tables),
        grid=(n_tiles,),
        in_specs=[pl.BlockSpec(memory_space=pl.ANY),
                  pl.BlockSpec((1, d), cmap),
                  pl.BlockSpec((n_eg, d, de), wmap),
                  pl.BlockSpec((n_eg, d, de), wmap),
                  pl.BlockSpec((n_eg, de, d), wmap),
                  pl.BlockSpec((1, d), cmap)],
        out_specs=pl.BlockSpec(memory_space=pl.ANY),
        scratch_shapes=[pltpu.VMEM((tm // SUBLANES + 1, SUBLANES, dx), F32),
                        pltpu.VMEM((tm // SUBLANES + 1, SUBLANES, d), F32),
                        pltpu.VMEM((tm, LANES), F32),
                        pltpu.VMEM((tm, d), BF16),
                        pltpu.VMEM((tm, d), F32),
                        pltpu.VMEM((tm, d), F32),
                        pltpu.SemaphoreType.DMA(()), pltpu.SemaphoreType.DMA(())])
    return pl.pallas_call(
        _moe_kernel,
        grid_spec=grid_spec,
        out_shape=jax.ShapeDtypeStruct((t, d), F32),
        compiler_params=pltpu.CompilerParams(dimension_semantics=("arbitrary",),
                                             vmem_limit_bytes=VMEM_LIMIT),
        name="moe",
    )(*tables, hx, g.reshape(1, d), w1, w3, w2,
      g_final.reshape(1, d))


def _layer(h2, mem2, batch, seq, p, l, g_final, tm):
    d = h2.shape[1]
    tm_mix = min(MIXER_TILE, seq)
    n_chunks = tm_mix // SSM_CHUNK
    n_scan = int(math.log2(n_chunks))
    tables = _ssm_tables(p["ssm_a_re"][l], p["ssm_a_im"][l], p["ssm_log_dt"][l], p["ssm_b_re"][l],
                         p["ssm_b_im"][l], p["ssm_c_re"][l], p["ssm_c_im"][l], p["ssm_d"][l],
                         SSM_CHUNK, n_scan, n_chunks // SUBLANES)
    h2 = _mixer(h2, seq, p["norm_mix"][l], p["w_in"][l], p["pool_w"][l], p["pool_scale"][l], tables,
                p["glu_w"][l], p["glu_b"][l], p["br_pool"][l], p["br_ssm"][l], p["w_out"][l],
                tm_mix)

    wqk, wvo = _kv_fold(mem2, batch, p["norm_mem"][l], p["x_wkv"][l], p["x_wq"][l], p["x_wo"][l])

    n_g, n_eg = N_EXPERT_GROUPS, EXPERTS_PER_GROUP
    w_r = jnp.concatenate([p["router_g_w"][l], p["router_e_w"][l].reshape(d, n_g * n_eg)], axis=1)
    w_r = jnp.pad(w_r, ((0, 0), (0, LANES - w_r.shape[1])))
    b_r = jnp.concatenate([p["router_g_b"][l], p["router_e_b"][l].reshape(n_g * n_eg)])
    b_r = jnp.pad(b_r, (0, LANES - b_r.shape[0])).reshape(1, LANES)
    hx, cls, counts, (w1b, w3b, w2b) = _xattn(
        h2, seq, p["norm_x"][l], wqk, wvo, p["norm_ffn"][l], w_r, b_r,
        p["exp_w1"][l], p["exp_w3"][l], p["exp_w2"][l], min(XATTN_TILE, seq))
    return _moe(hx, cls, counts, p["norm_ffn"][l], w1b, w3b, w2b, g_final, tm)


def kernel(x, mem, norm_mix, w_in, pool_w, pool_scale, ssm_a_re, ssm_a_im, ssm_log_dt, ssm_b_re,
           ssm_b_im, ssm_c_re, ssm_c_im, ssm_d, glu_w, glu_b, br_pool, br_ssm, w_out, norm_x,
           norm_mem, x_wq, x_wkv, x_wo, norm_ffn, router_g_w, router_g_b, router_e_w, router_e_b,
           exp_w1, exp_w3, exp_w2, norm_final):
    p = dict(norm_mix=norm_mix, w_in=w_in, pool_w=pool_w, pool_scale=pool_scale, ssm_a_re=ssm_a_re,
             ssm_a_im=ssm_a_im, ssm_log_dt=ssm_log_dt, ssm_b_re=ssm_b_re, ssm_b_im=ssm_b_im,
             ssm_c_re=ssm_c_re, ssm_c_im=ssm_c_im, ssm_d=ssm_d, glu_w=glu_w, glu_b=glu_b,
             br_pool=br_pool, br_ssm=br_ssm, w_out=w_out, norm_x=norm_x, norm_mem=norm_mem,
             x_wq=x_wq, x_wkv=x_wkv, x_wo=x_wo, norm_ffn=norm_ffn, router_g_w=router_g_w,
             router_g_b=router_g_b, router_e_w=router_e_w, router_e_b=router_e_b, exp_w1=exp_w1,
             exp_w3=exp_w3, exp_w2=exp_w2)
    batch, seq, d = x.shape
    assert norm_mix.shape[0] == 1, "final norm is fused into the (single) layer's last stage"
    tm = min(MOE_TILE, seq)
    assert seq % min(MIXER_TILE, seq) == 0 and seq % min(XATTN_TILE, seq) == 0 and (batch * seq) % tm == 0
    h2 = x.reshape(batch * seq, d)
    mem2 = mem.reshape(-1, d)
    out = _layer(h2, mem2, batch, seq, p, 0, norm_final, tm)
    return out.reshape(batch, seq, d)
```
